```python
import math
import jax, jax.numpy as jnp
from jax import lax
import numpy as np

D_MODEL = 1024
BATCH = 16
SEQ = 2048
DEPTH = 4

D_FF = 2816
NORM_EPS = 1e-6
F_MIN = 1e-6
CHUNK = 64
HG_HEADS = 4
HG_DK = 128
HG_DV = 128
HG_QK = HG_HEADS * HG_DK
HG_WIDTH = HG_HEADS * HG_DV
S5_WIDTH = D_MODEL - HG_WIDTH
S5_GROUP = 16
S5_GROUPS = S5_WIDTH // S5_GROUP
S5_STATE = 64
EV_IN = 2 * HG_QK + 2 * HG_WIDTH + S5_WIDTH
GDN_HEADS = 8
GDN_DK = 128
GDN_DV = 128
GDN_QK = GDN_HEADS * GDN_DK
GDN_V = GDN_HEADS * GDN_DV
CONV_W = 4
OD_IN = 2 * GDN_QK + 2 * GDN_V + 2 * GDN_HEADS
N_EVEN = (DEPTH + 1) // 2
N_ODD = DEPTH // 2

kernel_name = 'hybrid_hgrn2_s5_gdn_macaron'


def rmsnorm(x, w):
    xf = x.astype(jnp.float32)
    y = xf * lax.rsqrt(jnp.mean(xf * xf, axis=-1, keepdims=True) + NORM_EPS)
    return (y * w.astype(jnp.float32)).astype(x.dtype)


def swiglu(h, w_gate, w_up, w_down):
    return (jax.nn.silu(h @ w_gate) * (h @ w_up)) @ w_down


def split_chunks(t, n_heads):
    b, l, _ = t.shape
    t = t.reshape(b, l // CHUNK, CHUNK, n_heads, -1)
    return t.transpose(1, 0, 3, 2, 4)


def merge_chunks(t):
    nc, b, h, c, d = t.shape
    return t.transpose(1, 0, 3, 2, 4).reshape(b, nc * c, h, d)


def masked_exp(mask, z):
    return jnp.where(mask, jnp.exp(jnp.where(mask, z, 0.0)), 0.0)


def gated_head_norm(o, gate, w):
    o = o * lax.rsqrt(jnp.mean(o * o, axis=-1, keepdims=True) + NORM_EPS) * w.astype(jnp.float32)
    o = o * jax.nn.silu(gate.astype(jnp.float32)).reshape(o.shape)
    return o.reshape(o.shape[0], o.shape[1], -1)


def l2norm(t):
    return t * lax.rsqrt(jnp.sum(t * t, axis=-1, keepdims=True) + NORM_EPS)


def hgrn2_mix(q_lin, f_lin, i_val, g_lin, lb, norm_w):
    f32 = jnp.float32
    q = jax.nn.silu(q_lin.astype(f32))
    f = lb + (1.0 - lb) * jax.nn.sigmoid(f_lin.astype(f32))
    log_f = jnp.log(jnp.maximum(f, F_MIN))
    k = 1.0 - f
    v = i_val.astype(f32)
    qc = split_chunks(q, HG_HEADS)
    kc = split_chunks(k, HG_HEADS)
    vc = split_chunks(v, HG_HEADS)
    lfc = split_chunks(log_f, HG_HEADS)
    causal = jnp.tril(jnp.ones((CHUNK, CHUNK), dtype=bool))[:, :, None]
    s0 = jnp.zeros((q.shape[0], HG_HEADS, HG_DK, HG_DV), f32)

    def step(S, inp):
        q_c, k_c, v_c, lf_c = inp
        b = jnp.cumsum(lf_c, axis=2)
        diff = b[:, :, :, None, :] - b[:, :, None, :, :]
        decay = masked_exp(causal, diff)
        attn = jnp.einsum('bhtk,bhtsk,bhsk->bhts', q_c, decay, k_c)
        o = attn @ v_c + jnp.einsum('bhtk,bhkv->bhtv', q_c * jnp.exp(b), S)
        b_last = b[:, :, -1:, :]
        S = jnp.exp(b_last[:, :, 0, :, None]) * S + jnp.einsum(
            'bhsk,bhsv->bhkv', k_c * jnp.exp(b_last - b), v_c)
        return S, o

    _, o = lax.scan(step, s0, (qc, kc, vc, lfc))
    return gated_head_norm(merge_chunks(o), g_lin, norm_w)


def s5_mix(u, a_re, a_im, b_re, b_im, c_re, c_im, d, log_dt, w_glu):
    f32 = jnp.float32
    bsz, l, _ = u.shape
    uf = u.astype(f32).reshape(bsz, l, S5_GROUPS, S5_GROUP)
    a_re = a_re.astype(f32)
    a_im = a_im.astype(f32)
    dt = jnp.exp(log_dt.astype(f32))[:, None]
    mag = jnp.exp(dt * a_re)
    ang = dt * a_im
    abar_re = mag * jnp.cos(ang)
    abar_im = mag * jnp.sin(ang)
    den = a_re * a_re + a_im * a_im
    zr = abar_re - 1.0
    zi = abar_im
    coef_re = ((zr * a_re + zi * a_im) / den)[..., None]
    coef_im = ((zi * a_re - zr * a_im) / den)[..., None]
    b_re = b_re.astype(f32)
    b_im = b_im.astype(f32)
    bb_re = coef_re * b_re - coef_im * b_im
    bb_im = coef_re * b_im + coef_im * b_re
    bu_re = jnp.einsum('blgp,gnp->blgn', uf, bb_re)
    bu_im = jnp.einsum('blgp,gnp->blgn', uf, bb_im)
    ar = jnp.broadcast_to(abar_re, (1, l, S5_GROUPS, S5_STATE))
    ai = jnp.broadcast_to(abar_im, (1, l, S5_GROUPS, S5_STATE))

    def combine(e1, e2):
        ar1, ai1, br1, bi1 = e1
        ar2, ai2, br2, bi2 = e2
        return (ar2 * ar1 - ai2 * ai1,
                ar2 * ai1 + ai2 * ar1,
                ar2 * br1 - ai2 * bi1 + br2,
                ar2 * bi1 + ai2 * br1 + bi2)

    _, _, h_re, h_im = lax.associative_scan(combine, (ar, ai, bu_re, bu_im), axis=1)
    y = (jnp.einsum('gpn,blgn->blgp', c_re.astype(f32), h_re)
         - jnp.einsum('gpn,blgn->blgp', c_im.astype(f32), h_im)
         + d.astype(f32).reshape(S5_GROUPS, S5_GROUP) * uf)
    y = jax.nn.gelu(y.reshape(bsz, l, S5_WIDTH))
    return y * jax.nn.sigmoid(y @ w_glu.astype(f32))


def causal_dwconv(x, w):
    return lax.conv_general_dilated(
        x, w[:, None, :].astype(x.dtype), window_strides=(1,), padding=[(CONV_W - 1, 0)],
        dimension_numbers=('NWC', 'WIO', 'NWC'), feature_group_count=x.shape[-1])


def gdn_mix(proj, conv_w, a_log, dt_bias, norm_w):
    f32 = jnp.float32
    n_qkv = 2 * GDN_QK + GDN_V
    qkv = jax.nn.silu(causal_dwconv(proj[..., :n_qkv], conv_w)).astype(f32)
    gate = proj[..., n_qkv:n_qkv + GDN_V]
    beta = jax.nn.sigmoid(proj[..., n_qkv + GDN_V:n_qkv + GDN_V + GDN_HEADS].astype(f32))
    a_lin = proj[..., n_qkv + GDN_V + GDN_HEADS:].astype(f32)
    log_alpha = -jnp.exp(a_log.astype(f32)) * jax.nn.softplus(a_lin + dt_bias.astype(f32))
    q = split_chunks(qkv[..., :GDN_QK], GDN_HEADS)
    k = split_chunks(qkv[..., GDN_QK:2 * GDN_QK], GDN_HEADS)
    v = split_chunks(qkv[..., 2 * GDN_QK:], GDN_HEADS)
    q = l2norm(q) * (GDN_DK ** -0.5)
    k = l2norm(k)
    beta = split_chunks(beta, GDN_HEADS)[..., 0]
    g = jnp.cumsum(split_chunks(log_alpha, GDN_HEADS)[..., 0], axis=-1)
    lower = jnp.tril(jnp.ones((CHUNK, CHUNK), dtype=bool))
    strict = jnp.tril(jnp.ones((CHUNK, CHUNK), dtype=bool), k=-1)
    l_mask = masked_exp(lower, g[..., :, None] - g[..., None, :])
    kb = k * beta[..., None]
    vb = v * beta[..., None]
    m = jnp.where(strict, jnp.einsum('...ik,...jk->...ij', kb, k) * l_mask, 0.0)
    eye = jnp.eye(CHUNK, dtype=f32)
    t_inv = lax.linalg.triangular_solve(eye + m, jnp.broadcast_to(eye, m.shape),
                                        left_side=True, lower=True, unit_diagonal=True)
    u_c = t_inv @ vb
    w_c = t_inv @ (kb * jnp.exp(g)[..., None])
    attn = jnp.einsum('...ik,...jk->...ij', q, k) * l_mask
    s0 = jnp.zeros((proj.shape[0], GDN_HEADS, GDN_DK, GDN_DV), f32)

    def step(S, inp):
        q_c, k_c, uu, ww, at, g_c = inp
        v_new = uu - ww @ S
        o = (q_c * jnp.exp(g_c)[..., None]) @ S + at @ v_new
        g_last = g_c[..., -1:]
        S = S * jnp.exp(g_last)[..., None] + jnp.einsum(
            'bhsk,bhsv->bhkv', k_c * jnp.exp(g_last - g_c)[..., None], v_new)
        return S, o

    _, o = lax.scan(step, s0, (q, k, u_c, w_c, attn, g))
    return gated_head_norm(merge_chunks(o), gate, norm_w)


def _fwd_setup_inputs(seed: int = 0) -> dict:
    key = jax.random.key(seed)
    ks = jax.random.split(key, 32)
    f32 = jnp.float32

    def nrm(k, shape, scale):
        return jax.random.normal(k, shape, f32) * scale

    def gain(k, shape):
        return 1.0 + 0.02 * jax.random.normal(k, shape, f32)

    n_idx = jnp.arange(S5_STATE, dtype=f32)
    gdn_dt = jnp.exp(jax.random.uniform(ks[27], (N_ODD, GDN_HEADS), f32, math.log(1e-3), math.log(1e-1)))
    return {
        'x': nrm(ks[0], (BATCH, SEQ, D_MODEL), 1.0),
        'ffn1_norm': gain(ks[1], (DEPTH, D_MODEL)),
        'ffn1_w_gate': nrm(ks[2], (DEPTH, D_MODEL, D_FF), D_MODEL ** -0.5),
        'ffn1_w_up': nrm(ks[3], (DEPTH, D_MODEL, D_FF), D_MODEL ** -0.5),
        'ffn1_w_down': nrm(ks[4], (DEPTH, D_FF, D_MODEL), D_FF ** -0.5),
        'mix_norm': gain(ks[5], (DEPTH, D_MODEL)),
        'ffn2_norm': gain(ks[6], (DEPTH, D_MODEL)),
        'ffn2_w_gate': nrm(ks[7], (DEPTH, D_MODEL, D_FF), D_MODEL ** -0.5),
        'ffn2_w_up': nrm(ks[8], (DEPTH, D_MODEL, D_FF), D_MODEL ** -0.5),
        'ffn2_w_down': nrm(ks[9], (DEPTH, D_FF, D_MODEL), D_FF ** -0.5),
        'ev_w_in': nrm(ks[10], (N_EVEN, D_MODEL, EV_IN), D_MODEL ** -0.5),
        'hg_lb_logits': nrm(ks[11], (N_EVEN, HG_QK), 0.1),
        'hg_norm_w': gain(ks[12], (N_EVEN, HG_DV)),
        's5_a_re': -0.5 + nrm(ks[13], (N_EVEN, S5_GROUPS, S5_STATE), 0.01),
        's5_a_im': math.pi * n_idx + nrm(ks[14], (N_EVEN, S5_GROUPS, S5_STATE), 0.01),
        's5_b_re': nrm(ks[15], (N_EVEN, S5_GROUPS, S5_STATE, S5_GROUP), (2 * S5_GROUP) ** -0.5),
        's5_b_im': nrm(ks[16], (N_EVEN, S5_GROUPS, S5_STATE, S5_GROUP), (2 * S5_GROUP) ** -0.5),
        's5_c_re': nrm(ks[17], (N_EVEN, S5_GROUPS, S5_GROUP, S5_STATE), (2 * S5_STATE) ** -0.5),
        's5_c_im': nrm(ks[18], (N_EVEN, S5_GROUPS, S5_GROUP, S5_STATE), (2 * S5_STATE) ** -0.5),
        's5_d': nrm(ks[19], (N_EVEN, S5_WIDTH), 1.0),
        's5_log_dt': jax.random.uniform(ks[20], (N_EVEN, S5_GROUPS), f32, math.log(1e-3), math.log(1e-1)),
        's5_w_glu': nrm(ks[21], (N_EVEN, S5_WIDTH, S5_WIDTH), S5_WIDTH ** -0.5),
        'ev_w_out': nrm(ks[22], (N_EVEN, HG_WIDTH + S5_WIDTH, D_MODEL), (HG_WIDTH + S5_WIDTH) ** -0.5),
        'od_w_in': nrm(ks[23], (N_ODD, D_MODEL, OD_IN), D_MODEL ** -0.5),
        'gdn_conv_w': nrm(ks[24], (N_ODD, CONV_W, 2 * GDN_QK + GDN_V), CONV_W ** -0.5),
        'gdn_a_log': jnp.log(jax.random.uniform(ks[25], (N_ODD, GDN_HEADS), f32, 1.0, 16.0)),
        'gdn_dt_bias': gdn_dt + jnp.log(-jnp.expm1(-gdn_dt)),
        'gdn_norm_w': gain(ks[26], (N_ODD, GDN_DV)),
        'od_w_out': nrm(ks[28], (N_ODD, GDN_V, D_MODEL), GDN_V ** -0.5),
        'final_norm': gain(ks[29], (D_MODEL,)),
    }


def _fwd_reference(x, ffn1_norm, ffn1_w_gate, ffn1_w_up, ffn1_w_down, mix_norm,
              ffn2_norm, ffn2_w_gate, ffn2_w_up, ffn2_w_down,
              ev_w_in, hg_lb_logits, hg_norm_w, s5_a_re, s5_a_im, s5_b_re, s5_b_im,
              s5_c_re, s5_c_im, s5_d, s5_log_dt, s5_w_glu, ev_w_out,
              od_w_in, gdn_conv_w, gdn_a_log, gdn_dt_bias, gdn_norm_w, od_w_out,
              final_norm):
    p = jax.nn.softmax(hg_lb_logits.astype(jnp.float32), axis=0)
    lbs = jnp.cumsum(p, axis=0) - p[0]
    for layer in range(DEPTH):
        x = x + 0.5 * swiglu(rmsnorm(x, ffn1_norm[layer]), ffn1_w_gate[layer],
                             ffn1_w_up[layer], ffn1_w_down[layer])
        h = rmsnorm(x, mix_norm[layer])
        j = layer // 2
        if layer % 2 == 0:
            proj = h @ ev_w_in[j]
            y_a = hgrn2_mix(proj[..., :HG_QK],
                            proj[..., HG_QK:2 * HG_QK],
                            proj[..., 2 * HG_QK:2 * HG_QK + HG_WIDTH],
                            proj[..., 2 * HG_QK + HG_WIDTH:2 * HG_QK + 2 * HG_WIDTH],
                            lbs[j], hg_norm_w[j])
            y_b = s5_mix(proj[..., 2 * HG_QK + 2 * HG_WIDTH:], s5_a_re[j], s5_a_im[j],
                         s5_b_re[j], s5_b_im[j], s5_c_re[j], s5_c_im[j], s5_d[j],
                         s5_log_dt[j], s5_w_glu[j])
            y = (jnp.concatenate([y_a, y_b], axis=-1) @ ev_w_out[j]).astype(x.dtype)
        else:
            proj = h @ od_w_in[j]
            y = (gdn_mix(proj, gdn_conv_w[j], gdn_a_log[j], gdn_dt_bias[j], gdn_norm_w[j])
                 @ od_w_out[j]).astype(x.dtype)
        x = x + y
        x = x + 0.5 * swiglu(rmsnorm(x, ffn2_norm[layer]), ffn2_w_gate[layer],
                             ffn2_w_up[layer], ffn2_w_down[layer])
    return rmsnorm(x, final_norm)


import jax as _jax
import jax.numpy as _jnp

TWIN_FORMAT = 'train_step'
FWD_PARAMS = ['x', 'ffn1_norm', 'ffn1_w_gate', 'ffn1_w_up', 'ffn1_w_down', 'mix_norm', 'ffn2_norm', 'ffn2_w_gate', 'ffn2_w_up', 'ffn2_w_down', 'ev_w_in', 'hg_lb_logits', 'hg_norm_w', 's5_a_re', 's5_a_im', 's5_b_re', 's5_b_im', 's5_c_re', 's5_c_im', 's5_d', 's5_log_dt', 's5_w_glu', 'ev_w_out', 'od_w_in', 'gdn_conv_w', 'gdn_a_log', 'gdn_dt_bias', 'gdn_norm_w', 'od_w_out', 'final_norm']
TWIN_WEIGHTS = ['ffn1_norm', 'ffn1_w_gate', 'ffn1_w_up', 'ffn1_w_down', 'mix_norm', 'ffn2_norm', 'ffn2_w_gate', 'ffn2_w_up', 'ffn2_w_down', 'ev_w_in', 'hg_lb_logits', 'hg_norm_w', 's5_a_re', 's5_a_im', 's5_b_re', 's5_b_im', 's5_c_re', 's5_c_im', 's5_d', 's5_log_dt', 's5_w_glu', 'ev_w_out', 'od_w_in', 'gdn_conv_w', 'gdn_a_log', 'gdn_dt_bias', 'gdn_norm_w', 'od_w_out', 'final_norm']
TWIN_DIFF_INPUT = 'x'
TWIN_INPUTS = ['x', 'ffn1_norm', 'ffn1_w_gate', 'ffn1_w_up', 'ffn1_w_down', 'mix_norm', 'ffn2_norm', 'ffn2_w_gate', 'ffn2_w_up', 'ffn2_w_down', 'ev_w_in', 'hg_lb_logits', 'hg_norm_w', 's5_a_re', 's5_a_im', 's5_b_re', 's5_b_im', 's5_c_re', 's5_c_im', 's5_d', 's5_log_dt', 's5_w_glu', 'ev_w_out', 'od_w_in', 'gdn_conv_w', 'gdn_a_log', 'gdn_dt_bias', 'gdn_norm_w', 'od_w_out', 'final_norm', 'loss_target', 'm_ffn1_norm', 'm_ffn1_w_gate', 'm_ffn1_w_up', 'm_ffn1_w_down', 'm_mix_norm', 'm_ffn2_norm', 'm_ffn2_w_gate', 'm_ffn2_w_up', 'm_ffn2_w_down', 'm_ev_w_in', 'm_hg_lb_logits', 'm_hg_norm_w', 'm_s5_a_re', 'm_s5_a_im', 'm_s5_b_re', 'm_s5_b_im', 'm_s5_c_re', 'm_s5_c_im', 'm_s5_d', 'm_s5_log_dt', 'm_s5_w_glu', 'm_ev_w_out', 'm_od_w_in', 'm_gdn_conv_w', 'm_gdn_a_log', 'm_gdn_dt_bias', 'm_gdn_norm_w', 'm_od_w_out', 'm_final_norm', 'v_ffn1_norm', 'v_ffn1_w_gate', 'v_ffn1_w_up', 'v_ffn1_w_down', 'v_mix_norm', 'v_ffn2_norm', 'v_ffn2_w_gate', 'v_ffn2_w_up', 'v_ffn2_w_down', 'v_ev_w_in', 'v_hg_lb_logits', 'v_hg_norm_w', 'v_s5_a_re', 'v_s5_a_im', 'v_s5_b_re', 'v_s5_b_im', 'v_s5_c_re', 'v_s5_c_im', 'v_s5_d', 'v_s5_log_dt', 'v_s5_w_glu', 'v_ev_w_out', 'v_od_w_in', 'v_gdn_conv_w', 'v_gdn_a_log', 'v_gdn_dt_bias', 'v_gdn_norm_w', 'v_od_w_out', 'v_final_norm']
TWIN_OUTPUTS = ['loss', 'grad_x', 'grad_ffn1_norm', 'grad_ffn1_w_gate', 'grad_ffn1_w_up', 'grad_ffn1_w_down', 'grad_mix_norm', 'grad_ffn2_norm', 'grad_ffn2_w_gate', 'grad_ffn2_w_up', 'grad_ffn2_w_down', 'grad_ev_w_in', 'grad_hg_lb_logits', 'grad_hg_norm_w', 'grad_s5_a_re', 'grad_s5_a_im', 'grad_s5_b_re', 'grad_s5_b_im', 'grad_s5_c_re', 'grad_s5_c_im', 'grad_s5_d', 'grad_s5_log_dt', 'grad_s5_w_glu', 'grad_ev_w_out', 'grad_od_w_in', 'grad_gdn_conv_w', 'grad_gdn_a_log', 'grad_gdn_dt_bias', 'grad_gdn_norm_w', 'grad_od_w_out', 'grad_final_norm', 'delta_ffn1_norm', 'delta_ffn1_w_gate', 'delta_ffn1_w_up', 'delta_ffn1_w_down', 'delta_mix_norm', 'delta_ffn2_norm', 'delta_ffn2_w_gate', 'delta_ffn2_w_up', 'delta_ffn2_w_down', 'delta_ev_w_in', 'delta_hg_lb_logits', 'delta_hg_norm_w', 'delta_s5_a_re', 'delta_s5_a_im', 'delta_s5_b_re', 'delta_s5_b_im', 'delta_s5_c_re', 'delta_s5_c_im', 'delta_s5_d', 'delta_s5_log_dt', 'delta_s5_w_glu', 'delta_ev_w_out', 'delta_od_w_in', 'delta_gdn_conv_w', 'delta_gdn_a_log', 'delta_gdn_dt_bias', 'delta_gdn_norm_w', 'delta_od_w_out', 'delta_final_norm', 'new_m_ffn1_norm', 'new_m_ffn1_w_gate', 'new_m_ffn1_w_up', 'new_m_ffn1_w_down', 'new_m_mix_norm', 'new_m_ffn2_norm', 'new_m_ffn2_w_gate', 'new_m_ffn2_w_up', 'new_m_ffn2_w_down', 'new_m_ev_w_in', 'new_m_hg_lb_logits', 'new_m_hg_norm_w', 'new_m_s5_a_re', 'new_m_s5_a_im', 'new_m_s5_b_re', 'new_m_s5_b_im', 'new_m_s5_c_re', 'new_m_s5_c_im', 'new_m_s5_d', 'new_m_s5_log_dt', 'new_m_s5_w_glu', 'new_m_ev_w_out', 'new_m_od_w_in', 'new_m_gdn_conv_w', 'new_m_gdn_a_log', 'new_m_gdn_dt_bias', 'new_m_gdn_norm_w', 'new_m_od_w_out', 'new_m_final_norm', 'new_v_ffn1_norm', 'new_v_ffn1_w_gate', 'new_v_ffn1_w_up', 'new_v_ffn1_w_down', 'new_v_mix_norm', 'new_v_ffn2_norm', 'new_v_ffn2_w_gate', 'new_v_ffn2_w_up', 'new_v_ffn2_w_down', 'new_v_ev_w_in', 'new_v_hg_lb_logits', 'new_v_hg_norm_w', 'new_v_s5_a_re', 'new_v_s5_a_im', 'new_v_s5_b_re', 'new_v_s5_b_im', 'new_v_s5_c_re', 'new_v_s5_c_im', 'new_v_s5_d', 'new_v_s5_log_dt', 'new_v_s5_w_glu', 'new_v_ev_w_out', 'new_v_od_w_in', 'new_v_gdn_conv_w', 'new_v_gdn_a_log', 'new_v_gdn_dt_bias', 'new_v_gdn_norm_w', 'new_v_od_w_out', 'new_v_final_norm']
TWIN_LEAF_KINDS = {'loss': 'loss', 'grad_x': 'grad_x', 'grad_ffn1_norm': 'grad_w', 'grad_ffn1_w_gate': 'grad_w', 'grad_ffn1_w_up': 'grad_w', 'grad_ffn1_w_down': 'grad_w', 'grad_mix_norm': 'grad_w', 'grad_ffn2_norm': 'grad_w', 'grad_ffn2_w_gate': 'grad_w', 'grad_ffn2_w_up': 'grad_w', 'grad_ffn2_w_down': 'grad_w', 'grad_ev_w_in': 'grad_w', 'grad_hg_lb_logits': 'grad_w', 'grad_hg_norm_w': 'grad_w', 'grad_s5_a_re': 'grad_w', 'grad_s5_a_im': 'grad_w', 'grad_s5_b_re': 'grad_w', 'grad_s5_b_im': 'grad_w', 'grad_s5_c_re': 'grad_w', 'grad_s5_c_im': 'grad_w', 'grad_s5_d': 'grad_w', 'grad_s5_log_dt': 'grad_w', 'grad_s5_w_glu': 'grad_w', 'grad_ev_w_out': 'grad_w', 'grad_od_w_in': 'grad_w', 'grad_gdn_conv_w': 'grad_w', 'grad_gdn_a_log': 'grad_w', 'grad_gdn_dt_bias': 'grad_w', 'grad_gdn_norm_w': 'grad_w', 'grad_od_w_out': 'grad_w', 'grad_final_norm': 'grad_w', 'delta_ffn1_norm': 'delta_w', 'delta_ffn1_w_gate': 'delta_w', 'delta_ffn1_w_up': 'delta_w', 'delta_ffn1_w_down': 'delta_w', 'delta_mix_norm': 'delta_w', 'delta_ffn2_norm': 'delta_w', 'delta_ffn2_w_gate': 'delta_w', 'delta_ffn2_w_up': 'delta_w', 'delta_ffn2_w_down': 'delta_w', 'delta_ev_w_in': 'delta_w', 'delta_hg_lb_logits': 'delta_w', 'delta_hg_norm_w': 'delta_w', 'delta_s5_a_re': 'delta_w', 'delta_s5_a_im': 'delta_w', 'delta_s5_b_re': 'delta_w', 'delta_s5_b_im': 'delta_w', 'delta_s5_c_re': 'delta_w', 'delta_s5_c_im': 'delta_w', 'delta_s5_d': 'delta_w', 'delta_s5_log_dt': 'delta_w', 'delta_s5_w_glu': 'delta_w', 'delta_ev_w_out': 'delta_w', 'delta_od_w_in': 'delta_w', 'delta_gdn_conv_w': 'delta_w', 'delta_gdn_a_log': 'delta_w', 'delta_gdn_dt_bias': 'delta_w', 'delta_gdn_norm_w': 'delta_w', 'delta_od_w_out': 'delta_w', 'delta_final_norm': 'delta_w', 'new_m_ffn1_norm': 'new_m', 'new_m_ffn1_w_gate': 'new_m', 'new_m_ffn1_w_up': 'new_m', 'new_m_ffn1_w_down': 'new_m', 'new_m_mix_norm': 'new_m', 'new_m_ffn2_norm': 'new_m', 'new_m_ffn2_w_gate': 'new_m', 'new_m_ffn2_w_up': 'new_m', 'new_m_ffn2_w_down': 'new_m', 'new_m_ev_w_in': 'new_m', 'new_m_hg_lb_logits': 'new_m', 'new_m_hg_norm_w': 'new_m', 'new_m_s5_a_re': 'new_m', 'new_m_s5_a_im': 'new_m', 'new_m_s5_b_re': 'new_m', 'new_m_s5_b_im': 'new_m', 'new_m_s5_c_re': 'new_m', 'new_m_s5_c_im': 'new_m', 'new_m_s5_d': 'new_m', 'new_m_s5_log_dt': 'new_m', 'new_m_s5_w_glu': 'new_m', 'new_m_ev_w_out': 'new_m', 'new_m_od_w_in': 'new_m', 'new_m_gdn_conv_w': 'new_m', 'new_m_gdn_a_log': 'new_m', 'new_m_gdn_dt_bias': 'new_m', 'new_m_gdn_norm_w': 'new_m', 'new_m_od_w_out': 'new_m', 'new_m_final_norm': 'new_m', 'new_v_ffn1_norm': 'new_v', 'new_v_ffn1_w_gate': 'new_v', 'new_v_ffn1_w_up': 'new_v', 'new_v_ffn1_w_down': 'new_v', 'new_v_mix_norm': 'new_v', 'new_v_ffn2_norm': 'new_v', 'new_v_ffn2_w_gate': 'new_v', 'new_v_ffn2_w_up': 'new_v', 'new_v_ffn2_w_down': 'new_v', 'new_v_ev_w_in': 'new_v', 'new_v_hg_lb_logits': 'new_v', 'new_v_hg_norm_w': 'new_v', 'new_v_s5_a_re': 'new_v', 'new_v_s5_a_im': 'new_v', 'new_v_s5_b_re': 'new_v', 'new_v_s5_b_im': 'new_v', 'new_v_s5_c_re': 'new_v', 'new_v_s5_c_im': 'new_v', 'new_v_s5_d': 'new_v', 'new_v_s5_log_dt': 'new_v', 'new_v_s5_w_glu': 'new_v', 'new_v_ev_w_out': 'new_v', 'new_v_od_w_in': 'new_v', 'new_v_gdn_conv_w': 'new_v', 'new_v_gdn_a_log': 'new_v', 'new_v_gdn_dt_bias': 'new_v', 'new_v_gdn_norm_w': 'new_v', 'new_v_od_w_out': 'new_v', 'new_v_final_norm': 'new_v'}


def _forward(args):
    return _fwd_reference(*[args[k] for k in FWD_PARAMS])


def _output_shape():
    out = _jax.eval_shape(lambda: _forward(_fwd_setup_inputs(0)))
    return out.shape, out.dtype

N_MICROBATCH = 1
ADAM_LR = 0.001
ADAM_B1 = 0.9
ADAM_B2 = 0.999
ADAM_EPS = 1e-08
ADAM_WD = 0.01
ADAM_STEP = 10
PER_EXAMPLE_BATCH_AXIS = {'x': 0, 'loss_target': 0}
SHARED_INPUTS = []
_WEIGHT_DTYPES = {'ffn1_norm': _jnp.float32, 'ffn1_w_gate': _jnp.float32, 'ffn1_w_up': _jnp.float32, 'ffn1_w_down': _jnp.float32, 'mix_norm': _jnp.float32, 'ffn2_norm': _jnp.float32, 'ffn2_w_gate': _jnp.float32, 'ffn2_w_up': _jnp.float32, 'ffn2_w_down': _jnp.float32, 'ev_w_in': _jnp.float32, 'hg_lb_logits': _jnp.float32, 'hg_norm_w': _jnp.float32, 's5_a_re': _jnp.float32, 's5_a_im': _jnp.float32, 's5_b_re': _jnp.float32, 's5_b_im': _jnp.float32, 's5_c_re': _jnp.float32, 's5_c_im': _jnp.float32, 's5_d': _jnp.float32, 's5_log_dt': _jnp.float32, 's5_w_glu': _jnp.float32, 'ev_w_out': _jnp.float32, 'od_w_in': _jnp.float32, 'gdn_conv_w': _jnp.float32, 'gdn_a_log': _jnp.float32, 'gdn_dt_bias': _jnp.float32, 'gdn_norm_w': _jnp.float32, 'od_w_out': _jnp.float32, 'final_norm': _jnp.float32}
MOMENT_SCALE = {'ffn1_norm': 8.760298e-02, 'ffn1_w_gate': 3.815931e-02, 'ffn1_w_up': 3.689837e-02, 'ffn1_w_down': 6.124465e-02, 'mix_norm': 1.353543e-01, 'ffn2_norm': 7.060866e-02, 'ffn2_w_gate': 3.021020e-02, 'ffn2_w_up': 2.927218e-02, 'ffn2_w_down': 4.850048e-02, 'ev_w_in': 8.375972e-02, 'hg_lb_logits': 7.410187e-03, 'hg_norm_w': 2.347997e-01, 's5_a_re': 4.203620e-03, 's5_a_im': 4.652079e-03, 's5_b_re': 2.631195e-03, 's5_b_im': 2.607352e-03, 's5_c_re': 5.272265e-03, 's5_c_im': 5.221609e-03, 's5_d': 7.969724e-02, 's5_log_dt': 2.270474e+00, 's5_w_glu': 2.064816e-02, 'ev_w_out': 9.742888e-02, 'od_w_in': 6.986497e-02, 'gdn_conv_w': 6.473997e-02, 'gdn_a_log': 3.972569e-01, 'gdn_dt_bias': 3.839914e-01, 'gdn_norm_w': 2.465035e-01, 'od_w_out': 8.193124e-02, 'final_norm': 3.202531e+01}


def _to_microbatches(a, axis):
    t = _jnp.moveaxis(a, axis, 0)
    t = t.reshape((N_MICROBATCH, t.shape[0] // N_MICROBATCH) + t.shape[1:])
    return _jnp.moveaxis(t, 1, axis + 1)


def setup_inputs(seed: int = 0) -> dict:
    inp = _fwd_setup_inputs(seed)
    key = _jax.random.fold_in(_jax.random.key(seed), 7919)
    shape, _ = _output_shape()
    out = dict(inp)
    out["loss_target"] = _jax.random.normal(_jax.random.fold_in(key, 0), shape, _jnp.float32)
    for i, name in enumerate(TWIN_WEIGHTS):
        w = inp[name].astype(_jnp.float32)
        if MOMENT_SCALE is None:
            s = _jnp.sqrt(_jnp.mean(_jnp.square(w)) + 1e-30)
        else:
            s = MOMENT_SCALE[name]
        km, kv = _jax.random.split(_jax.random.fold_in(key, i + 1))
        out[name] = w
        out["m_" + name] = s * _jax.random.normal(km, w.shape, _jnp.float32)
        out["v_" + name] = (s * s) * _jax.random.uniform(kv, w.shape, _jnp.float32, 0.5, 1.5)
    if N_MICROBATCH > 1:
        for name, axis in PER_EXAMPLE_BATCH_AXIS.items():
            out[name] = _to_microbatches(out[name], axis)
    return {'x': out['x'], 'ffn1_norm': out['ffn1_norm'], 'ffn1_w_gate': out['ffn1_w_gate'], 'ffn1_w_up': out['ffn1_w_up'], 'ffn1_w_down': out['ffn1_w_down'], 'mix_norm': out['mix_norm'], 'ffn2_norm': out['ffn2_norm'], 'ffn2_w_gate': out['ffn2_w_gate'], 'ffn2_w_up': out['ffn2_w_up'], 'ffn2_w_down': out['ffn2_w_down'], 'ev_w_in': out['ev_w_in'], 'hg_lb_logits': out['hg_lb_logits'], 'hg_norm_w': out['hg_norm_w'], 's5_a_re': out['s5_a_re'], 's5_a_im': out['s5_a_im'], 's5_b_re': out['s5_b_re'], 's5_b_im': out['s5_b_im'], 's5_c_re': out['s5_c_re'], 's5_c_im': out['s5_c_im'], 's5_d': out['s5_d'], 's5_log_dt': out['s5_log_dt'], 's5_w_glu': out['s5_w_glu'], 'ev_w_out': out['ev_w_out'], 'od_w_in': out['od_w_in'], 'gdn_conv_w': out['gdn_conv_w'], 'gdn_a_log': out['gdn_a_log'], 'gdn_dt_bias': out['gdn_dt_bias'], 'gdn_norm_w': out['gdn_norm_w'], 'od_w_out': out['od_w_out'], 'final_norm': out['final_norm'], 'loss_target': out['loss_target'], 'm_ffn1_norm': out['m_ffn1_norm'], 'm_ffn1_w_gate': out['m_ffn1_w_gate'], 'm_ffn1_w_up': out['m_ffn1_w_up'], 'm_ffn1_w_down': out['m_ffn1_w_down'], 'm_mix_norm': out['m_mix_norm'], 'm_ffn2_norm': out['m_ffn2_norm'], 'm_ffn2_w_gate': out['m_ffn2_w_gate'], 'm_ffn2_w_up': out['m_ffn2_w_up'], 'm_ffn2_w_down': out['m_ffn2_w_down'], 'm_ev_w_in': out['m_ev_w_in'], 'm_hg_lb_logits': out['m_hg_lb_logits'], 'm_hg_norm_w': out['m_hg_norm_w'], 'm_s5_a_re': out['m_s5_a_re'], 'm_s5_a_im': out['m_s5_a_im'], 'm_s5_b_re': out['m_s5_b_re'], 'm_s5_b_im': out['m_s5_b_im'], 'm_s5_c_re': out['m_s5_c_re'], 'm_s5_c_im': out['m_s5_c_im'], 'm_s5_d': out['m_s5_d'], 'm_s5_log_dt': out['m_s5_log_dt'], 'm_s5_w_glu': out['m_s5_w_glu'], 'm_ev_w_out': out['m_ev_w_out'], 'm_od_w_in': out['m_od_w_in'], 'm_gdn_conv_w': out['m_gdn_conv_w'], 'm_gdn_a_log': out['m_gdn_a_log'], 'm_gdn_dt_bias': out['m_gdn_dt_bias'], 'm_gdn_norm_w': out['m_gdn_norm_w'], 'm_od_w_out': out['m_od_w_out'], 'm_final_norm': out['m_final_norm'], 'v_ffn1_norm': out['v_ffn1_norm'], 'v_ffn1_w_gate': out['v_ffn1_w_gate'], 'v_ffn1_w_up': out['v_ffn1_w_up'], 'v_ffn1_w_down': out['v_ffn1_w_down'], 'v_mix_norm': out['v_mix_norm'], 'v_ffn2_norm': out['v_ffn2_norm'], 'v_ffn2_w_gate': out['v_ffn2_w_gate'], 'v_ffn2_w_up': out['v_ffn2_w_up'], 'v_ffn2_w_down': out['v_ffn2_w_down'], 'v_ev_w_in': out['v_ev_w_in'], 'v_hg_lb_logits': out['v_hg_lb_logits'], 'v_hg_norm_w': out['v_hg_norm_w'], 'v_s5_a_re': out['v_s5_a_re'], 'v_s5_a_im': out['v_s5_a_im'], 'v_s5_b_re': out['v_s5_b_re'], 'v_s5_b_im': out['v_s5_b_im'], 'v_s5_c_re': out['v_s5_c_re'], 'v_s5_c_im': out['v_s5_c_im'], 'v_s5_d': out['v_s5_d'], 'v_s5_log_dt': out['v_s5_log_dt'], 'v_s5_w_glu': out['v_s5_w_glu'], 'v_ev_w_out': out['v_ev_w_out'], 'v_od_w_in': out['v_od_w_in'], 'v_gdn_conv_w': out['v_gdn_conv_w'], 'v_gdn_a_log': out['v_gdn_a_log'], 'v_gdn_dt_bias': out['v_gdn_dt_bias'], 'v_gdn_norm_w': out['v_gdn_norm_w'], 'v_od_w_out': out['v_od_w_out'], 'v_final_norm': out['v_final_norm']}


def _loss(weights, diff, rest, loss_target):
    with _jax.named_scope("forward"):
        args = {**rest, TWIN_DIFF_INPUT: diff, **{k: w.astype(_WEIGHT_DTYPES[k]) for k, w in weights.items()}}
        y = _forward(args)
    with _jax.named_scope("loss_head"):
        err = _jnp.square(y.astype(_jnp.float32) - loss_target)
        return 0.5 * _jnp.sum(_jnp.mean(err, axis=-1)) if err.ndim else 0.5 * err


def _adamw(w, g, m, v):
    m = ADAM_B1 * m + (1.0 - ADAM_B1) * g
    v = ADAM_B2 * v + (1.0 - ADAM_B2) * _jnp.square(g)
    m_hat = m / (1.0 - ADAM_B1 ** ADAM_STEP)
    v_hat = v / (1.0 - ADAM_B2 ** ADAM_STEP)
    delta = -ADAM_LR * (m_hat / (_jnp.sqrt(v_hat) + ADAM_EPS) + ADAM_WD * w)
    return delta, m, v


def reference(x, ffn1_norm, ffn1_w_gate, ffn1_w_up, ffn1_w_down, mix_norm, ffn2_norm, ffn2_w_gate, ffn2_w_up, ffn2_w_down, ev_w_in, hg_lb_logits, hg_norm_w, s5_a_re, s5_a_im, s5_b_re, s5_b_im, s5_c_re, s5_c_im, s5_d, s5_log_dt, s5_w_glu, ev_w_out, od_w_in, gdn_conv_w, gdn_a_log, gdn_dt_bias, gdn_norm_w, od_w_out, final_norm, loss_target, m_ffn1_norm, m_ffn1_w_gate, m_ffn1_w_up, m_ffn1_w_down, m_mix_norm, m_ffn2_norm, m_ffn2_w_gate, m_ffn2_w_up, m_ffn2_w_down, m_ev_w_in, m_hg_lb_logits, m_hg_norm_w, m_s5_a_re, m_s5_a_im, m_s5_b_re, m_s5_b_im, m_s5_c_re, m_s5_c_im, m_s5_d, m_s5_log_dt, m_s5_w_glu, m_ev_w_out, m_od_w_in, m_gdn_conv_w, m_gdn_a_log, m_gdn_dt_bias, m_gdn_norm_w, m_od_w_out, m_final_norm, v_ffn1_norm, v_ffn1_w_gate, v_ffn1_w_up, v_ffn1_w_down, v_mix_norm, v_ffn2_norm, v_ffn2_w_gate, v_ffn2_w_up, v_ffn2_w_down, v_ev_w_in, v_hg_lb_logits, v_hg_norm_w, v_s5_a_re, v_s5_a_im, v_s5_b_re, v_s5_b_im, v_s5_c_re, v_s5_c_im, v_s5_d, v_s5_log_dt, v_s5_w_glu, v_ev_w_out, v_od_w_in, v_gdn_conv_w, v_gdn_a_log, v_gdn_dt_bias, v_gdn_norm_w, v_od_w_out, v_final_norm):
    given = dict(x=x, ffn1_norm=ffn1_norm, ffn1_w_gate=ffn1_w_gate, ffn1_w_up=ffn1_w_up, ffn1_w_down=ffn1_w_down, mix_norm=mix_norm, ffn2_norm=ffn2_norm, ffn2_w_gate=ffn2_w_gate, ffn2_w_up=ffn2_w_up, ffn2_w_down=ffn2_w_down, ev_w_in=ev_w_in, hg_lb_logits=hg_lb_logits, hg_norm_w=hg_norm_w, s5_a_re=s5_a_re, s5_a_im=s5_a_im, s5_b_re=s5_b_re, s5_b_im=s5_b_im, s5_c_re=s5_c_re, s5_c_im=s5_c_im, s5_d=s5_d, s5_log_dt=s5_log_dt, s5_w_glu=s5_w_glu, ev_w_out=ev_w_out, od_w_in=od_w_in, gdn_conv_w=gdn_conv_w, gdn_a_log=gdn_a_log, gdn_dt_bias=gdn_dt_bias, gdn_norm_w=gdn_norm_w, od_w_out=od_w_out, final_norm=final_norm, loss_target=loss_target, m_ffn1_norm=m_ffn1_norm, m_ffn1_w_gate=m_ffn1_w_gate, m_ffn1_w_up=m_ffn1_w_up, m_ffn1_w_down=m_ffn1_w_down, m_mix_norm=m_mix_norm, m_ffn2_norm=m_ffn2_norm, m_ffn2_w_gate=m_ffn2_w_gate, m_ffn2_w_up=m_ffn2_w_up, m_ffn2_w_down=m_ffn2_w_down, m_ev_w_in=m_ev_w_in, m_hg_lb_logits=m_hg_lb_logits, m_hg_norm_w=m_hg_norm_w, m_s5_a_re=m_s5_a_re, m_s5_a_im=m_s5_a_im, m_s5_b_re=m_s5_b_re, m_s5_b_im=m_s5_b_im, m_s5_c_re=m_s5_c_re, m_s5_c_im=m_s5_c_im, m_s5_d=m_s5_d, m_s5_log_dt=m_s5_log_dt, m_s5_w_glu=m_s5_w_glu, m_ev_w_out=m_ev_w_out, m_od_w_in=m_od_w_in, m_gdn_conv_w=m_gdn_conv_w, m_gdn_a_log=m_gdn_a_log, m_gdn_dt_bias=m_gdn_dt_bias, m_gdn_norm_w=m_gdn_norm_w, m_od_w_out=m_od_w_out, m_final_norm=m_final_norm, v_ffn1_norm=v_ffn1_norm, v_ffn1_w_gate=v_ffn1_w_gate, v_ffn1_w_up=v_ffn1_w_up, v_ffn1_w_down=v_ffn1_w_down, v_mix_norm=v_mix_norm, v_ffn2_norm=v_ffn2_norm, v_ffn2_w_gate=v_ffn2_w_gate, v_ffn2_w_up=v_ffn2_w_up, v_ffn2_w_down=v_ffn2_w_down, v_ev_w_in=v_ev_w_in, v_hg_lb_logits=v_hg_lb_logits, v_hg_norm_w=v_hg_norm_w, v_s5_a_re=v_s5_a_re, v_s5_a_im=v_s5_a_im, v_s5_b_re=v_s5_b_re, v_s5_b_im=v_s5_b_im, v_s5_c_re=v_s5_c_re, v_s5_c_im=v_s5_c_im, v_s5_d=v_s5_d, v_s5_log_dt=v_s5_log_dt, v_s5_w_glu=v_s5_w_glu, v_ev_w_out=v_ev_w_out, v_od_w_in=v_od_w_in, v_gdn_conv_w=v_gdn_conv_w, v_gdn_a_log=v_gdn_a_log, v_gdn_dt_bias=v_gdn_dt_bias, v_gdn_norm_w=v_gdn_norm_w, v_od_w_out=v_od_w_out, v_final_norm=v_final_norm)
    weights = {n: given[n] for n in TWIN_WEIGHTS}
    shared = {n: given[n] for n in SHARED_INPUTS}
    per_example = {n: given[n] for n in ['x']}
    grad_fn = _jax.value_and_grad(_loss, argnums=(0, 1))

    def one_microbatch(ex, loss_target):
        ex = dict(ex)
        diff = ex.pop(TWIN_DIFF_INPUT)
        return grad_fn(weights, diff, {**shared, **ex}, loss_target)

    if N_MICROBATCH == 1:
        loss, (grad_w, grad_x) = one_microbatch(per_example, given["loss_target"])
    else:
        def body(carry, xs):
            loss_sum, grad_sum = carry
            l_k, (gw_k, gx_k) = one_microbatch(xs[0], xs[1])
            with _jax.named_scope("update"):
                return (loss_sum + l_k, _jax.tree.map(_jnp.add, grad_sum, gw_k)), gx_k

        init = (_jnp.zeros((), _jnp.float32), _jax.tree.map(_jnp.zeros_like, weights))
        (loss, grad_w), grad_x = _jax.lax.scan(body, init, (per_example, given["loss_target"]))
    with _jax.named_scope("update"):
        delta_w, new_m, new_v = {}, {}, {}
        for n in TWIN_WEIGHTS:
            delta_w[n], new_m[n], new_v[n] = _adamw(weights[n], grad_w[n], given["m_" + n], given["v_" + n])
    return (loss, grad_x, *[grad_w[n] for n in TWIN_WEIGHTS], *[delta_w[n] for n in TWIN_WEIGHTS],
            *[new_m[n] for n in TWIN_WEIGHTS], *[new_v[n] for n in TWIN_WEIGHTS])
```

```python
import functools
import math

import jax
import jax.numpy as jnp
import numpy as np
from jax import lax
from jax.experimental import pallas as pl
from jax.experimental.pallas import tpu as pltpu

F32 = jnp.float32
BF16 = jnp.bfloat16
HI = lax.Precision.HIGHEST
MESH_ID = pl.DeviceIdType.MESH

D_MODEL = 1024
D_FF = 2816
DEPTH = 4
NORM_EPS = 1e-6
F_MIN = 1e-6
CHUNK = 64
HEAD = 128
HG_HEADS = 4
HG_W = 512
S5_W = 512
S5_GROUP = 16
S5_GROUPS = 32
S5_STATE = 64
S5_N = S5_GROUPS * S5_STATE
GDN_HEADS = 8
GDN_W = 1024
CONV_W = 4
N_CHIPS = 4
N_DEV = 8
LANE = 128
SUBLANE = 8
VMEM_LIMIT = 56 * 1024 * 1024
PACK_COLS = 1024
PACK_ROWS = 21504
SMALL_ROWS = 288

ADAM_LR = 0.001
ADAM_B1 = 0.9
ADAM_B2 = 0.999
ADAM_EPS = 1e-08
ADAM_WD = 0.01
ADAM_STEP = 10

SHARDED = (("ffn1_w_gate", 2), ("ffn1_w_up", 2), ("ffn1_w_down", 1), ("ffn2_w_gate", 2), ("ffn2_w_up", 2),
           ("ffn2_w_down", 1), ("ev_w_in", 2), ("s5_w_glu", 1), ("ev_w_out", 1), ("od_w_in", 2),
           ("gdn_conv_w", 2), ("od_w_out", 1))
SMALL = ("ffn1_norm", "mix_norm", "ffn2_norm", "hg_lb_logits", "hg_norm_w", "s5_a_re", "s5_a_im", "s5_b_re",
         "s5_b_im", "s5_c_re", "s5_c_im", "s5_d", "s5_log_dt", "gdn_a_log", "gdn_dt_bias", "gdn_norm_w",
         "final_norm")
WEIGHTS = ("ffn1_norm", "ffn1_w_gate", "ffn1_w_up", "ffn1_w_down", "mix_norm", "ffn2_norm", "ffn2_w_gate",
           "ffn2_w_up", "ffn2_w_down", "ev_w_in", "hg_lb_logits", "hg_norm_w", "s5_a_re", "s5_a_im", "s5_b_re",
           "s5_b_im", "s5_c_re", "s5_c_im", "s5_d", "s5_log_dt", "s5_w_glu", "ev_w_out", "od_w_in", "gdn_conv_w",
           "gdn_a_log", "gdn_dt_bias", "gdn_norm_w", "od_w_out", "final_norm")


def _cparams(sem=None):
    return pltpu.CompilerParams(dimension_semantics=sem, vmem_limit_bytes=VMEM_LIMIT)


def _dg(a, b, ca, cb, hi):
    if hi:
        return lax.dot_general(a.astype(F32), b.astype(F32), (((ca,), (cb,)), ((), ())), precision=HI,
                               preferred_element_type=F32)
    return lax.dot_general(a.astype(BF16), b.astype(BF16), (((ca,), (cb,)), ((), ())), preferred_element_type=F32)


@functools.partial(jax.custom_vjp, nondiff_argnums=(2,))
def mm_nn(a, b, hi=False):
    return _dg(a, b, 1, 0, hi)


@functools.partial(jax.custom_vjp, nondiff_argnums=(2,))
def mm_nt(a, b, hi=False):
    return _dg(a, b, 1, 1, hi)


@functools.partial(jax.custom_vjp, nondiff_argnums=(2,))
def mm_tn(a, b, hi=False):
    return _dg(a, b, 0, 0, hi)


mm_nn.defvjp(lambda a, b, hi: (_dg(a, b, 1, 0, hi), (a, b)),
             lambda hi, r, g: (mm_nt(g, r[1], hi), mm_tn(r[0], g, hi)))
mm_nt.defvjp(lambda a, b, hi: (_dg(a, b, 1, 1, hi), (a, b)),
             lambda hi, r, g: (mm_nn(g, r[1], hi), mm_tn(g, r[0], hi)))
mm_tn.defvjp(lambda a, b, hi: (_dg(a, b, 0, 0, hi), (a, b)),
             lambda hi, r, g: (mm_nt(r[1], g, hi), mm_nn(r[0], g, hi)))


def _sigmoid(x):
    return 1.0 / (1.0 + jnp.exp(-x))


def _silu(x):
    return x * _sigmoid(x)


def _softplus(x):
    return jnp.maximum(x, 0.0) + jnp.log(1.0 + jnp.exp(-jnp.abs(x)))


def _gelu(x):
    return 0.5 * x * (1.0 + jnp.tanh(math.sqrt(2.0 / math.pi) * (x + 0.044715 * (x * x * x))))


def _iota(shape, dim):
    return lax.broadcasted_iota(jnp.int32, shape, dim)


def _l2norm(t):
    return t * lax.rsqrt(jnp.sum(t * t, axis=-1, keepdims=True) + NORM_EPS)


def gated_head_norm(o, gate, nw):
    o = o * lax.rsqrt(jnp.mean(o * o, axis=-1, keepdims=True) + NORM_EPS) * nw
    return o * _silu(gate)


HG_SUB = 8
HG_MID = 3


def hg_chunk(st, ql, fl, v, gl, lb, nw):
    c = CHUNK
    q = _silu(ql)
    f = lb + (1.0 - lb) * _sigmoid(fl)
    lf = jnp.log(jnp.maximum(f, F_MIN))
    k = 1.0 - f
    t_i = _iota((2 * c, c), 0)
    s_i = _iota((2 * c, c), 1)
    lim = jnp.where(t_i < c, t_i, ((t_i - c) // HG_SUB) * HG_SUB + HG_MID)
    cum = mm_nn(jnp.where(s_i <= lim, 1.0, 0.0).astype(F32), lf, True)
    b, rb = cum[:c], cum[c:]
    qp = q * jnp.exp(b - rb)
    row = _iota((c, HEAD), 0)
    blocks = []
    for i in range(c // HG_SUB):
        r_i = jnp.tile(rb[i * HG_SUB:(i + 1) * HG_SUB], (c // HG_SUB, 1))
        seen = row < (i + 1) * HG_SUB
        kp = k * jnp.exp(jnp.where(seen, r_i - b, 0.0))
        blocks.append(mm_nt(qp[i * HG_SUB:(i + 1) * HG_SUB], kp))
    a = jnp.concatenate(blocks, axis=0)
    a = jnp.where(_iota((c, c), 1) <= _iota((c, c), 0), a, 0.0)
    o = mm_nn(a, v) + mm_nt(q * jnp.exp(b), st)
    bl = jnp.sum(lf, axis=0, keepdims=True)
    st_new = st * jnp.exp(bl) + mm_tn(v, k * jnp.exp(bl - b))
    return st_new, gated_head_norm(o, gl, nw)


def gdn_chunk(st, qc, kc, vc, gate, bd, ab, dtb, nw, mb, ma):
    c = CHUNK
    beta = _sigmoid(jnp.sum(bd * mb, axis=-1, keepdims=True))
    a_lin = jnp.sum(bd * ma, axis=-1, keepdims=True)
    a_log = jnp.sum(ab * ma, axis=-1, keepdims=True)
    dt_b = jnp.sum(dtb * ma, axis=-1, keepdims=True)
    la = -jnp.exp(a_log) * _softplus(a_lin + dt_b)
    la_b = jnp.broadcast_to(la, (c, HEAD))
    t_i = _iota((c, c), 0)
    s_i = _iota((c, c), 1)
    lower = s_i <= t_i
    strict = s_i < t_i
    g = mm_nn(jnp.where(lower, 1.0, 0.0).astype(F32), la_b, True)
    g_row = mm_nt(jnp.full((c, HEAD), 1.0 / HEAD, F32), g, True)
    d = g[:, :c] - g_row
    lm = jnp.where(lower, jnp.exp(jnp.where(lower, d, 0.0)), 0.0)
    q = _l2norm(qc) * (HEAD ** -0.5)
    k = _l2norm(kc)
    kb = k * beta
    vb = vc * beta
    m = jnp.where(strict, mm_nt(kb, k) * lm, 0.0)
    eye = jnp.where(s_i == t_i, 1.0, 0.0).astype(F32)
    t_inv = eye - m
    p = m
    for _ in range(int(math.log2(c)) - 1):
        p = mm_nn(p, p, True)
        t_inv = t_inv + mm_nn(t_inv, p, True)
    u = mm_nn(t_inv, vb)
    w = mm_nn(t_inv, kb * jnp.exp(g))
    attn = mm_nt(q, k) * lm
    v_new = u - mm_nt(w, st)
    o = mm_nt(q * jnp.exp(g), st) + mm_nn(attn, v_new)
    g_last = jnp.sum(la_b, axis=0, keepdims=True)
    st_new = st * jnp.exp(g_last) + mm_tn(v_new, k * jnp.exp(g_last - g))
    return st_new, gated_head_norm(o, gate, nw)


def s5_post(ych, u, d, wglu):
    y = _gelu(ych + d * u)
    return y * _sigmoid(mm_nn(y, wglu))


def _pick(n, cands):
    for c in cands:
        if n % c == 0:
            return c
    return n


def _mm(a, b, mode, *, name, out_dtype=F32, res=None, scale=1.0, tm=None, tn=None, tk=None):
    if mode == "nn":
        (m, k), (k2, n) = a.shape, b.shape
    elif mode == "nt":
        (m, k), (n, k2) = a.shape, b.shape
    else:
        (k, m), (k2, n) = a.shape, b.shape
    assert k == k2, (a.shape, b.shape, mode)
    tm = tm or _pick(m, (512, 1408, 256, 128) if mode == "tn" else (512, 704, 256, 128))
    tn = tn or _pick(n, (1024, 1408, 512, 256, 128))
    tk = tk or _pick(k, (1024, 1408, 512, 256, 128))
    nk = k // tk
    grid = (n // tn, m // tm, nk)
    if mode == "tn":
        a_spec = pl.BlockSpec((tk, tm), lambda j, i, kk: (kk, i))
    else:
        a_spec = pl.BlockSpec((tm, tk), lambda j, i, kk: (i, kk))
    if mode == "nt":
        b_spec = pl.BlockSpec((tn, tk), lambda j, i, kk: (j, kk))
    else:
        b_spec = pl.BlockSpec((tk, tn), lambda j, i, kk: (kk, j))
    o_spec = pl.BlockSpec((tm, tn), lambda j, i, kk: (i, j))
    ca = 0 if mode == "tn" else 1
    cb = 1 if mode == "nt" else 0
    has_res = res is not None

    def body(*refs):
        if has_res:
            a_ref, b_ref, r_ref, o_ref, acc = refs
        else:
            a_ref, b_ref, o_ref, acc = refs
        kk = pl.program_id(2)

        @pl.when(kk == 0)
        def _():
            acc[...] = jnp.zeros_like(acc)

        acc[...] += _dg(a_ref[...], b_ref[...], ca, cb, False)

        @pl.when(kk == nk - 1)
        def _():
            r = acc[...] * scale if scale != 1.0 else acc[...]
            if has_res:
                r = r_ref[...] + r
            o_ref[...] = r.astype(out_dtype)

    in_specs = [a_spec, b_spec] + ([o_spec] if has_res else [])
    args = (a, b) + ((res,) if has_res else ())
    return pl.pallas_call(
        body, out_shape=jax.ShapeDtypeStruct((m, n), out_dtype), grid=grid, in_specs=in_specs, out_specs=o_spec,
        scratch_shapes=[pltpu.VMEM((tm, tn), F32)], name=name,
        compiler_params=_cparams(("parallel", "parallel", "arbitrary")))(*args)


def _rms_fwd(x, w, *, name):
    t, d = x.shape
    tm = _pick(t, (512, 256, 128))

    def body(x_ref, w_ref, o_ref):
        xv = x_ref[...]
        r = lax.rsqrt(jnp.mean(xv * xv, axis=-1, keepdims=True) + NORM_EPS)
        o_ref[...] = (xv * r * w_ref[...]).astype(BF16)

    return pl.pallas_call(
        body, out_shape=jax.ShapeDtypeStruct((t, d), BF16), grid=(t // tm,),
        in_specs=[pl.BlockSpec((tm, d), lambda i: (i, 0)), pl.BlockSpec((1, d), lambda i: (0, 0))],
        out_specs=pl.BlockSpec((tm, d), lambda i: (i, 0)), name=name, compiler_params=_cparams(("parallel",)))(x, w)


def _rms_bwd(x, w, dh, dres, *, name):
    t, d = x.shape
    tm = _pick(t, (512, 256, 128))

    def body(x_ref, w_ref, dh_ref, dr_ref, dx_ref, dw_ref):
        xv = x_ref[...]
        r = lax.rsqrt(jnp.mean(xv * xv, axis=-1, keepdims=True) + NORM_EPS)
        xh = xv * r
        dhv = dh_ref[...]
        dxh = dhv * w_ref[...]
        dx_ref[...] = dr_ref[...] + r * (dxh - xh * jnp.mean(dxh * xh, axis=-1, keepdims=True))

        @pl.when(pl.program_id(0) == 0)
        def _():
            dw_ref[...] = jnp.zeros_like(dw_ref)

        dw_ref[...] += jnp.sum(dhv * xh, axis=0, keepdims=True)

    row = pl.BlockSpec((tm, d), lambda i: (i, 0))
    vec = pl.BlockSpec((1, d), lambda i: (0, 0))
    return pl.pallas_call(
        body, out_shape=[jax.ShapeDtypeStruct((t, d), F32), jax.ShapeDtypeStruct((1, d), F32)], grid=(t // tm,),
        in_specs=[row, vec, row, row], out_specs=[row, vec], name=name,
        compiler_params=_cparams(("arbitrary",)))(x, w, dh, dres)


def _final_loss(x, w, target, *, name):
    t, d = x.shape
    tm = _pick(t, (512, 256, 128))

    def body(x_ref, w_ref, t_ref, l_ref, dx_ref, dw_ref):
        xv = x_ref[...]
        wv = w_ref[...]
        r = lax.rsqrt(jnp.mean(xv * xv, axis=-1, keepdims=True) + NORM_EPS)
        xh = xv * r
        err = xh * wv - t_ref[...]
        dy = err * (1.0 / d)
        dxh = dy * wv
        dx_ref[...] = r * (dxh - xh * jnp.mean(dxh * xh, axis=-1, keepdims=True))

        @pl.when(pl.program_id(0) == 0)
        def _():
            dw_ref[...] = jnp.zeros_like(dw_ref)
            l_ref[...] = jnp.zeros_like(l_ref)

        dw_ref[...] += jnp.sum(dy * xh, axis=0, keepdims=True)
        l_ref[...] += 0.5 * jnp.sum(jnp.mean(err * err, axis=-1, keepdims=True), axis=0, keepdims=True)

    row = pl.BlockSpec((tm, d), lambda i: (i, 0))
    vec = pl.BlockSpec((1, d), lambda i: (0, 0))
    return pl.pallas_call(
        body, out_shape=[jax.ShapeDtypeStruct((SUBLANE, LANE), F32), jax.ShapeDtypeStruct((t, d), F32),
                         jax.ShapeDtypeStruct((1, d), F32)], grid=(t // tm,),
        in_specs=[row, vec, row], out_specs=[pl.BlockSpec((SUBLANE, LANE), lambda i: (0, 0)), row, vec], name=name,
        compiler_params=_cparams(("arbitrary",)))(x, w, target)


def _ffn_up(h, wg, wu, *, name):
    t, d = h.shape
    f = wg.shape[1]
    tm = _pick(t, (512, 256, 128))
    tn = _pick(f, (1408, 512, 256, 128))

    def body(h_ref, wg_ref, wu_ref, g_ref, u_ref, a_ref):
        hv = h_ref[...]
        g = _dg(hv, wg_ref[...], 1, 0, False)
        u = _dg(hv, wu_ref[...], 1, 0, False)
        g_ref[...] = g
        u_ref[...] = u
        a_ref[...] = (_silu(g) * u).astype(BF16)

    hs = pl.BlockSpec((tm, d), lambda j, i: (i, 0))
    ws = pl.BlockSpec((d, tn), lambda j, i: (0, j))
    os_ = pl.BlockSpec((tm, tn), lambda j, i: (i, j))
    return pl.pallas_call(
        body, out_shape=[jax.ShapeDtypeStruct((t, f), F32), jax.ShapeDtypeStruct((t, f), F32),
                         jax.ShapeDtypeStruct((t, f), BF16)], grid=(f // tn, t // tm),
        in_specs=[hs, ws, ws], out_specs=[os_, os_, os_], name=name,
        compiler_params=_cparams(("parallel", "parallel")))(h, wg, wu)


def _ffn_dact(dxo, wd, g, u, *, name):
    t, d = dxo.shape
    f = wd.shape[0]
    tm = _pick(t, (512, 256, 128))
    tn = _pick(f, (1408, 512, 256, 128))

    def body(dx_ref, wd_ref, g_ref, u_ref, dg_ref, du_ref):
        da = 0.5 * _dg(dx_ref[...], wd_ref[...], 1, 1, False)
        gv = g_ref[...]
        sg = _sigmoid(gv)
        dg_ref[...] = (da * u_ref[...] * (sg * (1.0 + gv * (1.0 - sg)))).astype(BF16)
        du_ref[...] = (da * (gv * sg)).astype(BF16)

    xs = pl.BlockSpec((tm, d), lambda j, i: (i, 0))
    ws = pl.BlockSpec((tn, d), lambda j, i: (j, 0))
    os_ = pl.BlockSpec((tm, tn), lambda j, i: (i, j))
    return pl.pallas_call(
        body, out_shape=[jax.ShapeDtypeStruct((t, f), BF16), jax.ShapeDtypeStruct((t, f), BF16)],
        grid=(f // tn, t // tm), in_specs=[xs, ws, os_, os_], out_specs=[os_, os_], name=name,
        compiler_params=_cparams(("parallel", "parallel")))(dxo, wd, g, u)


def _ffn_fwd(x, nw, wg, wu, wd, tag):
    h = _rms_fwd(x, nw, name=f"rms_fwd_{tag}")
    g, u, a = _ffn_up(h, wg, wu, name=f"ffn_up_{tag}")
    y = _mm(a, wd, "nn", res=x, scale=0.5, name=f"ffn_down_{tag}")
    return y, (x, h, g, u, a)


def _ffn_bwd(saved, nw, wg, wu, wd, dxo, tag):
    x, h, g, u, a = saved
    dg, du = _ffn_dact(dxo, wd, g, u, name=f"ffn_dact_{tag}")
    dwd = _mm(a, dxo, "tn", scale=0.5, name=f"ffn_dwd_{tag}")
    dwg = _mm(h, dg, "tn", name=f"ffn_dwg_{tag}")
    dwu = _mm(h, du, "tn", name=f"ffn_dwu_{tag}")
    dh = _mm(dg, wg, "nt", name=f"ffn_dh1_{tag}")
    dh = _mm(du, wu, "nt", res=dh, name=f"ffn_dh2_{tag}")
    dx, dnw = _rms_bwd(x, nw, dh, dxo, name=f"rms_bwd_{tag}")
    return dx, dnw, dwg, dwu, dwd


def _hg_fwd(proj, lbs, nw, *, name):
    bsz, l, _ = proj.shape
    nc = l // CHUNK

    def body(p_ref, lb_ref, nw_ref, y_ref, s_ref, st):
        @pl.when(pl.program_id(2) == 0)
        def _():
            st[...] = jnp.zeros_like(st)

        s_ref[...] = st[...]
        p = p_ref[...]
        st_new, y = hg_chunk(st[...], p[:, 0:128], p[:, 128:256], p[:, 256:384], p[:, 384:512], lb_ref[...],
                             nw_ref[...])
        st[...] = st_new
        y_ref[...] = y

    return pl.pallas_call(
        body,
        out_shape=[jax.ShapeDtypeStruct((bsz, l, HG_W), F32),
                   jax.ShapeDtypeStruct((bsz, HG_HEADS, nc, HEAD, HEAD), F32)],
        grid=(bsz, HG_HEADS, nc),
        in_specs=[pl.BlockSpec((None, CHUNK, 512), lambda b, h, c: (b, c, h)),
                  pl.BlockSpec((1, HEAD), lambda b, h, c: (0, h)),
                  pl.BlockSpec((1, HEAD), lambda b, h, c: (0, 0))],
        out_specs=[pl.BlockSpec((None, CHUNK, HEAD), lambda b, h, c: (b, c, h)),
                   pl.BlockSpec((None, None, None, HEAD, HEAD), lambda b, h, c: (b, h, c, 0, 0))],
        scratch_shapes=[pltpu.VMEM((HEAD, HEAD), F32)], name=name,
        compiler_params=_cparams(("parallel", "parallel", "arbitrary")))(proj, lbs, nw)


def _hg_bwd(proj, lbs, nw, states, dy, *, name):
    bsz, l, _ = proj.shape
    nc = l // CHUNK

    def body(p_ref, lb_ref, nw_ref, s_ref, dy_ref, dp_ref, dlb_ref, dnw_ref, dst):
        @pl.when(pl.program_id(2) == 0)
        def _():
            dst[...] = jnp.zeros_like(dst)
            dlb_ref[...] = jnp.zeros_like(dlb_ref)
            dnw_ref[...] = jnp.zeros_like(dnw_ref)

        p = p_ref[...]
        _, vjp = jax.vjp(hg_chunk, s_ref[...], p[:, 0:128], p[:, 128:256], p[:, 256:384], p[:, 384:512],
                         lb_ref[...], nw_ref[...])
        d_st, dq, df, dv, dgl, dlb, dnw = vjp((dst[...], dy_ref[...]))
        dst[...] = d_st
        dp_ref[...] = jnp.concatenate([dq, df, dv, dgl], axis=1)
        dlb_ref[...] += dlb
        dnw_ref[...] += dnw

    rev = lambda c: nc - 1 - c
    return pl.pallas_call(
        body,
        out_shape=[jax.ShapeDtypeStruct((bsz, l, 4 * HG_W), F32),
                   jax.ShapeDtypeStruct((bsz, HG_HEADS, 1, HEAD), F32),
                   jax.ShapeDtypeStruct((bsz, HG_HEADS, 1, HEAD), F32)],
        grid=(bsz, HG_HEADS, nc),
        in_specs=[pl.BlockSpec((None, CHUNK, 512), lambda b, h, c: (b, rev(c), h)),
                  pl.BlockSpec((1, HEAD), lambda b, h, c: (0, h)),
                  pl.BlockSpec((1, HEAD), lambda b, h, c: (0, 0)),
                  pl.BlockSpec((None, None, None, HEAD, HEAD), lambda b, h, c: (b, h, rev(c), 0, 0)),
                  pl.BlockSpec((None, CHUNK, HEAD), lambda b, h, c: (b, rev(c), h))],
        out_specs=[pl.BlockSpec((None, CHUNK, 512), lambda b, h, c: (b, rev(c), h)),
                   pl.BlockSpec((None, None, 1, HEAD), lambda b, h, c: (b, h, 0, 0)),
                   pl.BlockSpec((None, None, 1, HEAD), lambda b, h, c: (b, h, 0, 0))],
        scratch_shapes=[pltpu.VMEM((HEAD, HEAD), F32)], name=name,
        compiler_params=_cparams(("parallel", "parallel", "arbitrary")))(proj, lbs, nw, states, dy)


def _head_masks(h):
    lane = _iota((1, HEAD), 1)
    return jnp.where(lane == h, 1.0, 0.0).astype(F32), jnp.where(lane == GDN_HEADS + h, 1.0, 0.0).astype(F32)


def _gdn_fwd(qkvg, bd, ab, dtb, nw, *, name):
    bsz, l, _ = qkvg.shape
    nc = l // CHUNK

    def body(p_ref, bd_ref, ab_ref, dtb_ref, nw_ref, y_ref, s_ref, st):
        @pl.when(pl.program_id(2) == 0)
        def _():
            st[...] = jnp.zeros_like(st)

        s_ref[...] = st[...]
        p = p_ref[...]
        mb, ma = _head_masks(pl.program_id(1))
        st_new, y = gdn_chunk(st[...], p[:, 0:128], p[:, 128:256], p[:, 256:384], p[:, 384:512], bd_ref[...],
                              ab_ref[...], dtb_ref[...], nw_ref[...], mb, ma)
        st[...] = st_new
        y_ref[...] = y

    vec = pl.BlockSpec((1, HEAD), lambda b, h, c: (0, 0))
    return pl.pallas_call(
        body,
        out_shape=[jax.ShapeDtypeStruct((bsz, l, GDN_W), F32),
                   jax.ShapeDtypeStruct((bsz, GDN_HEADS, nc, HEAD, HEAD), F32)],
        grid=(bsz, GDN_HEADS, nc),
        in_specs=[pl.BlockSpec((None, CHUNK, 512), lambda b, h, c: (b, c, h)),
                  pl.BlockSpec((None, CHUNK, HEAD), lambda b, h, c: (b, c, 0)), vec, vec, vec],
        out_specs=[pl.BlockSpec((None, CHUNK, HEAD), lambda b, h, c: (b, c, h)),
                   pl.BlockSpec((None, None, None, HEAD, HEAD), lambda b, h, c: (b, h, c, 0, 0))],
        scratch_shapes=[pltpu.VMEM((HEAD, HEAD), F32)], name=name,
        compiler_params=_cparams(("parallel", "parallel", "arbitrary")))(qkvg, bd, ab, dtb, nw)


def _gdn_bwd(qkvg, bd, ab, dtb, nw, states, dy, *, name):
    bsz, l, _ = qkvg.shape
    nc = l // CHUNK

    def body(p_ref, bd_ref, ab_ref, dtb_ref, nw_ref, s_ref, dy_ref, dp_ref, dbd_ref, dab_ref, ddt_ref, dnw_ref,
             dst):
        @pl.when(pl.program_id(2) == 0)
        def _():
            dst[...] = jnp.zeros_like(dst)
            dab_ref[...] = jnp.zeros_like(dab_ref)
            ddt_ref[...] = jnp.zeros_like(ddt_ref)
            dnw_ref[...] = jnp.zeros_like(dnw_ref)

        p = p_ref[...]
        mb, ma = _head_masks(pl.program_id(1))
        fn = lambda st, q, k, v, gt, bdv, abv, dtv, nwv: gdn_chunk(st, q, k, v, gt, bdv, abv, dtv, nwv, mb, ma)
        _, vjp = jax.vjp(fn, s_ref[...], p[:, 0:128], p[:, 128:256], p[:, 256:384], p[:, 384:512], bd_ref[...],
                         ab_ref[...], dtb_ref[...], nw_ref[...])
        d_st, dq, dk, dv, dgt, dbd, dab, ddt, dnw = vjp((dst[...], dy_ref[...]))
        dst[...] = d_st
        dp_ref[...] = jnp.concatenate([dq, dk, dv, dgt], axis=1)
        dbd_ref[...] = dbd
        dab_ref[...] += dab
        ddt_ref[...] += ddt
        dnw_ref[...] += dnw

    rev = lambda c: nc - 1 - c
    vec = pl.BlockSpec((1, HEAD), lambda b, h, c: (0, 0))
    acc = pl.BlockSpec((None, None, 1, HEAD), lambda b, h, c: (b, h, 0, 0))
    acc_shape = jax.ShapeDtypeStruct((bsz, GDN_HEADS, 1, HEAD), F32)
    return pl.pallas_call(
        body,
        out_shape=[jax.ShapeDtypeStruct((bsz, l, 4 * GDN_W), F32),
                   jax.ShapeDtypeStruct((bsz, GDN_HEADS, l, HEAD), F32), acc_shape, acc_shape, acc_shape],
        grid=(bsz, GDN_HEADS, nc),
        in_specs=[pl.BlockSpec((None, CHUNK, 512), lambda b, h, c: (b, rev(c), h)),
                  pl.BlockSpec((None, CHUNK, HEAD), lambda b, h, c: (b, rev(c), 0)), vec, vec, vec,
                  pl.BlockSpec((None, None, None, HEAD, HEAD), lambda b, h, c: (b, h, rev(c), 0, 0)),
                  pl.BlockSpec((None, CHUNK, HEAD), lambda b, h, c: (b, rev(c), h))],
        out_specs=[pl.BlockSpec((None, CHUNK, 512), lambda b, h, c: (b, rev(c), h)),
                   pl.BlockSpec((None, None, CHUNK, HEAD), lambda b, h, c: (b, h, rev(c), 0)), acc, acc, acc],
        scratch_shapes=[pltpu.VMEM((HEAD, HEAD), F32)], name=name,
        compiler_params=_cparams(("parallel", "parallel", "arbitrary")))(qkvg, bd, ab, dtb, nw, states, dy)


CONV_TL = 256
QKV_LANES = 3 * HEAD


def _conv_fwd(x, w, *, name):
    bsz, l, wd = x.shape
    tl = min(CONV_TL, l)
    nr = l // tl
    hb = tl // SUBLANE

    def body(x_ref, xp_ref, w_ref, o_ref):
        xv = x_ref[...]
        halo = jnp.where(pl.program_id(2) > 0, xp_ref[...], 0.0)
        xc = jnp.concatenate([halo, xv], axis=0)
        wv = w_ref[...]
        z = wv[CONV_W - 1:CONV_W] * xv
        for j in range(CONV_W - 1):
            z = z + wv[j:j + 1] * pltpu.roll(xc, CONV_W - 1 - j, 0)[SUBLANE:]
        o_ref[...] = jnp.where(_iota((tl, 512), 1) < QKV_LANES, _silu(z), xv)

    return pl.pallas_call(
        body, out_shape=jax.ShapeDtypeStruct(x.shape, F32), grid=(bsz, wd // 512, nr),
        in_specs=[pl.BlockSpec((None, tl, 512), lambda b, h, r: (b, r, h)),
                  pl.BlockSpec((None, SUBLANE, 512), lambda b, h, r: (b, jnp.maximum(r * hb - 1, 0), h)),
                  pl.BlockSpec((SUBLANE, 512), lambda b, h, r: (0, h))],
        out_specs=pl.BlockSpec((None, tl, 512), lambda b, h, r: (b, r, h)), name=name,
        compiler_params=_cparams(("parallel", "parallel", "arbitrary")))(x, x, w)


def _conv_bwd(x, w, dy, *, name):
    bsz, l, wd = x.shape
    tl = min(CONV_TL, l)
    nr = l // tl
    hb = tl // SUBLANE

    def body(x_ref, xp_ref, xn_ref, w_ref, dy_ref, dyn_ref, dx_ref, dw_ref):
        r = pl.program_id(2)
        xv = x_ref[...]
        prev = jnp.where(r > 0, xp_ref[...], 0.0)
        last = r == nr - 1
        nxt = jnp.where(last, 0.0, xn_ref[...])
        dyn = jnp.where(last, 0.0, dyn_ref[...])
        xc = jnp.concatenate([prev, xv, nxt], axis=0)
        wv = w_ref[...]
        shifted = [pltpu.roll(xc, CONV_W - 1 - j, 0) for j in range(CONV_W - 1)] + [xc]
        z = wv[0:1] * shifted[0]
        for j in range(1, CONV_W):
            z = z + wv[j:j + 1] * shifted[j]
        z = z[SUBLANE:]
        sg = _sigmoid(z)
        dyc = jnp.concatenate([dy_ref[...], dyn], axis=0)
        dz = dyc * (sg * (1.0 + z * (1.0 - sg)))
        dx = wv[CONV_W - 1:CONV_W] * dz[:tl]
        for j in range(CONV_W - 1):
            s = CONV_W - 1 - j
            dx = dx + wv[j:j + 1] * pltpu.roll(dz, tl + SUBLANE - s, 0)[:tl]
        conv_lane = _iota((tl, 512), 1) < QKV_LANES
        dx_ref[...] = jnp.where(conv_lane, dx, dy_ref[...])

        @pl.when(r == 0)
        def _():
            dw_ref[...] = jnp.zeros_like(dw_ref)

        dzt = dz[:tl]
        rows = [jnp.sum(dzt * shifted[j][SUBLANE:SUBLANE + tl], axis=0, keepdims=True) for j in range(CONV_W)]
        rows.append(jnp.zeros((SUBLANE - CONV_W, 512), F32))
        dw_ref[...] += jnp.concatenate(rows, axis=0)

    blk = pl.BlockSpec((None, tl, 512), lambda b, h, r: (b, r, h))
    prev = pl.BlockSpec((None, SUBLANE, 512), lambda b, h, r: (b, jnp.maximum(r * hb - 1, 0), h))
    nxt = pl.BlockSpec((None, SUBLANE, 512), lambda b, h, r: (b, jnp.minimum((r + 1) * hb, l // SUBLANE - 1), h))
    return pl.pallas_call(
        body, out_shape=[jax.ShapeDtypeStruct(x.shape, F32), jax.ShapeDtypeStruct((bsz, SUBLANE, wd), F32)],
        grid=(bsz, wd // 512, nr),
        in_specs=[blk, prev, nxt, pl.BlockSpec((SUBLANE, 512), lambda b, h, r: (0, h)), blk, nxt],
        out_specs=[blk, pl.BlockSpec((None, SUBLANE, 512), lambda b, h, r: (b, 0, h))], name=name,
        compiler_params=_cparams(("parallel", "parallel", "arbitrary")))(x, x, x, w, dy, dy)


S5_LC = 256
S5_TN = 512


def _s5_scan(xr, xi, pw, *, reverse, name):
    bsz, l, n = xr.shape
    lc = min(S5_LC, l)
    nc = l // lc
    steps = int(math.log2(lc))

    def shift(v, s):
        if reverse:
            if s >= SUBLANE:
                return jnp.concatenate([v[s:], jnp.zeros((s, v.shape[1]), F32)], axis=0)
            return jnp.where(_iota(v.shape, 0) < lc - s, pltpu.roll(v, lc - s, 0), 0.0)
        if s >= SUBLANE:
            return jnp.concatenate([jnp.zeros((s, v.shape[1]), F32), v[:lc - s]], axis=0)
        return jnp.where(_iota(v.shape, 0) >= s, pltpu.roll(v, s, 0), 0.0)

    def body(xr_ref, xi_ref, pw_ref, hr_ref, hi_ref, cr, ci):
        @pl.when(pl.program_id(2) == 0)
        def _():
            cr[...] = jnp.zeros_like(cr)
            ci[...] = jnp.zeros_like(ci)

        vr, vi = xr_ref[...], xi_ref[...]
        for k in range(steps):
            s = 1 << k
            ar, ai = pw_ref[0, s - 1:s, :], pw_ref[1, s - 1:s, :]
            sr, si = shift(vr, s), shift(vi, s)
            vr, vi = vr + ar * sr - ai * si, vi + ar * si + ai * sr
        pr, pi = pw_ref[2], pw_ref[3]
        c_r, c_i = cr[...], ci[...]
        vr, vi = vr + pr * c_r - pi * c_i, vi + pr * c_i + pi * c_r
        hr_ref[...] = vr
        hi_ref[...] = vi
        edge = 0 if reverse else lc - 1
        cr[...] = vr[edge:edge + 1]
        ci[...] = vi[edge:edge + 1]

    tmap = (lambda c: nc - 1 - c) if reverse else (lambda c: c)
    blk = pl.BlockSpec((None, lc, S5_TN), lambda b, j, c: (b, tmap(c), j))
    return pl.pallas_call(
        body, out_shape=[jax.ShapeDtypeStruct(xr.shape, F32), jax.ShapeDtypeStruct(xr.shape, F32)],
        grid=(bsz, n // S5_TN, nc),
        in_specs=[blk, blk, pl.BlockSpec((4, lc, S5_TN), lambda b, j, c: (0, 0, j))], out_specs=[blk, blk],
        scratch_shapes=[pltpu.VMEM((1, S5_TN), F32), pltpu.VMEM((1, S5_TN), F32)], name=name,
        compiler_params=_cparams(("parallel", "parallel", "arbitrary")))(xr, xi, pw)


def _s5_dabar(gr, gi, hr, hi, *, name):
    bsz, l, n = gr.shape
    lc = min(S5_LC, l)
    nc = l // lc
    hb = lc // SUBLANE

    def body(gr_ref, gi_ref, hr_ref, hi_ref, hrp_ref, hip_ref, o_ref):
        c = pl.program_id(2)

        @pl.when(c == 0)
        def _():
            o_ref[...] = jnp.zeros_like(o_ref)

        def prev(h_ref, hp_ref):
            first = jnp.where(c > 0, hp_ref[SUBLANE - 1:SUBLANE, :], 0.0)
            return jnp.where(_iota((lc, S5_TN), 0) == 0, first, pltpu.roll(h_ref[...], 1, 0))

        pr, pi = prev(hr_ref, hrp_ref), prev(hi_ref, hip_ref)
        g_r, g_i = gr_ref[...], gi_ref[...]
        d_re = jnp.sum(g_r * pr + g_i * pi, axis=0, keepdims=True)
        d_im = jnp.sum(g_i * pr - g_r * pi, axis=0, keepdims=True)
        o_ref[...] += jnp.concatenate([d_re, d_im, jnp.zeros((SUBLANE - 2, S5_TN), F32)], axis=0)

    blk = pl.BlockSpec((None, lc, S5_TN), lambda b, j, c: (b, c, j))
    prv = pl.BlockSpec((None, SUBLANE, S5_TN), lambda b, j, c: (b, jnp.maximum(c * hb - 1, 0), j))
    return pl.pallas_call(
        body, out_shape=jax.ShapeDtypeStruct((bsz, SUBLANE, n), F32), grid=(bsz, n // S5_TN, nc),
        in_specs=[blk, blk, blk, blk, prv, prv],
        out_specs=pl.BlockSpec((None, SUBLANE, S5_TN), lambda b, j, c: (b, 0, j)), name=name,
        compiler_params=_cparams(("parallel", "parallel", "arbitrary")))(gr, gi, hr, hi, hr, hi)


def _s5_post_fwd(ych, u, d, wglu, *, name):
    t, w = ych.shape
    tm = _pick(t, (512, 256, 128))

    def body(y_ref, u_ref, d_ref, w_ref, o_ref):
        o_ref[...] = s5_post(y_ref[...], u_ref[...], d_ref[...], w_ref[...])

    row = pl.BlockSpec((tm, w), lambda i: (i, 0))
    return pl.pallas_call(
        body, out_shape=jax.ShapeDtypeStruct((t, w), F32), grid=(t // tm,),
        in_specs=[row, row, pl.BlockSpec((1, w), lambda i: (0, 0)), pl.BlockSpec((w, w), lambda i: (0, 0))],
        out_specs=row, name=name, compiler_params=_cparams(("parallel",)))(ych, u, d, wglu)


def _s5_post_bwd(ych, u, d, wglu, dout, *, name):
    t, w = ych.shape
    tm = _pick(t, (512, 256, 128))

    def body(y_ref, u_ref, d_ref, w_ref, do_ref, dy_ref, du_ref, dd_ref, dw_ref):
        _, vjp = jax.vjp(s5_post, y_ref[...], u_ref[...], d_ref[...], w_ref[...])
        dy, du, dd, dw = vjp(do_ref[...])
        dy_ref[...] = dy
        du_ref[...] = du

        @pl.when(pl.program_id(0) == 0)
        def _():
            dd_ref[...] = jnp.zeros_like(dd_ref)
            dw_ref[...] = jnp.zeros_like(dw_ref)

        dd_ref[...] += dd
        dw_ref[...] += dw

    row = pl.BlockSpec((tm, w), lambda i: (i, 0))
    vec = pl.BlockSpec((1, w), lambda i: (0, 0))
    mat = pl.BlockSpec((w, w), lambda i: (0, 0))
    return pl.pallas_call(
        body, out_shape=[jax.ShapeDtypeStruct((t, w), F32), jax.ShapeDtypeStruct((t, w), F32),
                         jax.ShapeDtypeStruct((1, w), F32), jax.ShapeDtypeStruct((w, w), F32)], grid=(t // tm,),
        in_specs=[row, row, vec, mat, row], out_specs=[row, row, vec, mat], name=name,
        compiler_params=_cparams(("arbitrary",)))(ych, u, d, wglu, dout)


def _s5_disc(a_re, a_im, b_re, b_im, log_dt):
    dt = jnp.exp(log_dt)[:, None]
    mag = jnp.exp(dt * a_re)
    ang = dt * a_im
    abar_re = mag * jnp.cos(ang)
    abar_im = mag * jnp.sin(ang)
    den = a_re * a_re + a_im * a_im
    zr = abar_re - 1.0
    zi = abar_im
    coef_re = ((zr * a_re + zi * a_im) / den)[..., None]
    coef_im = ((zi * a_re - zr * a_im) / den)[..., None]
    return abar_re, abar_im, coef_re * b_re - coef_im * b_im, coef_re * b_im + coef_im * b_re


def _s5_powers(a_re, a_im, log_dt, lc):
    dt = jnp.exp(log_dt)[:, None]
    n = jnp.arange(1, lc + 1, dtype=F32)[:, None, None]
    mag = jnp.exp(n * (dt * a_re)[None])
    ang = n * (dt * a_im)[None]
    pr = (mag * jnp.cos(ang)).reshape(lc, S5_N)
    pi = (mag * jnp.sin(ang)).reshape(lc, S5_N)
    fwd = jnp.stack([pr, pi, pr, pi])
    bwd = jnp.stack([pr, -pi, pr[::-1], -pi[::-1]])
    return fwd, bwd


def _block_diag_in(bb):
    eye = jnp.eye(S5_GROUPS, dtype=bb.dtype)
    return jnp.einsum("gnp,gh->gphn", bb, eye).reshape(S5_W, S5_N)


def _block_diag_out(cc):
    eye = jnp.eye(S5_GROUPS, dtype=cc.dtype)
    return jnp.einsum("gpn,gh->gnhp", cc, eye).reshape(S5_N, S5_W)


def _adamw(w, g, m, v, *, name):
    shape = w.shape
    cols = shape[-1] if w.ndim > 1 else shape[0]
    rows = w.size // cols
    tr = _pick(rows, (512, 352, 256, 128, 64, 32, 16, 8)) if rows % SUBLANE == 0 else rows
    flat = lambda t: t.reshape(rows, cols)

    def body(w_ref, g_ref, m_ref, v_ref, d_ref, nm_ref, nv_ref):
        gv = g_ref[...]
        mn = ADAM_B1 * m_ref[...] + (1.0 - ADAM_B1) * gv
        vn = ADAM_B2 * v_ref[...] + (1.0 - ADAM_B2) * jnp.square(gv)
        m_hat = mn / (1.0 - ADAM_B1 ** ADAM_STEP)
        v_hat = vn / (1.0 - ADAM_B2 ** ADAM_STEP)
        d_ref[...] = -ADAM_LR * (m_hat / (jnp.sqrt(v_hat) + ADAM_EPS) + ADAM_WD * w_ref[...])
        nm_ref[...] = mn
        nv_ref[...] = vn

    blk = pl.BlockSpec((tr, cols), lambda i: (i, 0))
    sds = jax.ShapeDtypeStruct((rows, cols), F32)
    outs = pl.pallas_call(body, out_shape=[sds, sds, sds], grid=(rows // tr,), in_specs=[blk] * 4,
                          out_specs=[blk] * 3, name=name,
                          compiler_params=_cparams(("parallel",)))(flat(w), flat(g), flat(m), flat(v))
    return tuple(o.reshape(shape) for o in outs)


ANY = pl.BlockSpec(memory_space=pl.ANY)


def _coords():
    x, y, c = lax.axis_index("x"), lax.axis_index("y"), lax.axis_index("c")
    chips = [(1 - x, y), (x, 1 - y), (1 - x, 1 - y)]
    return x, y, c, chips


def _all_gather_chips(pack, *, name):
    r, w = pack.shape
    rh = r // 2

    def body(p_ref, o_ref, send, recv, local):
        x, y, c, chips = _coords()
        sibling = (x, y, 1 - c)

        def part(px, py, half):
            return o_ref.at[2 * px + py, pl.ds(half * rh, rh), :]

        def copy(k, src, dst, to):
            return pltpu.make_async_remote_copy(src_ref=src, dst_ref=dst, send_sem=send.at[k], recv_sem=recv.at[k],
                                                device_id=to, device_id_type=MESH_ID)

        mine = pltpu.make_async_copy(p_ref, o_ref.at[2 * x + y], local)
        mine.start()
        first = [copy(k, p_ref.at[pl.ds(c * rh, rh), :], part(x, y, c), (*chip, c)) for k, chip in enumerate(chips)]
        for cp in first:
            cp.start()
        passed = [copy(3 + k, part(*chip, c), part(*chip, c), sibling) for k, chip in enumerate(chips)]
        for k, chip in enumerate(chips):
            copy(k, part(*chip, c), part(*chip, c), (x, y, c)).wait_recv()
            passed[k].start()
        for k, chip in enumerate(chips):
            copy(3 + k, part(*chip, 1 - c), part(*chip, 1 - c), (x, y, c)).wait_recv()
        for cp in first + passed:
            cp.wait_send()
        mine.wait()

    return pl.pallas_call(
        body, out_shape=jax.ShapeDtypeStruct((N_CHIPS, r, w), pack.dtype), in_specs=[ANY], out_specs=ANY,
        scratch_shapes=[pltpu.SemaphoreType.DMA((6,)), pltpu.SemaphoreType.DMA((6,)), pltpu.SemaphoreType.DMA],
        name=name)(pack)


def _swap_halves(g, *, name):
    n, r, w = g.shape
    rh = r // 2

    def body(g_ref, o_ref, send, recv):
        x, y, c, _ = _coords()
        cp = pltpu.make_async_remote_copy(src_ref=g_ref.at[:, pl.ds((1 - c) * rh, rh), :], dst_ref=o_ref,
                                          send_sem=send, recv_sem=recv, device_id=(x, y, 1 - c),
                                          device_id_type=MESH_ID)
        cp.start()
        cp.wait()

    return pl.pallas_call(
        body, out_shape=jax.ShapeDtypeStruct((n, rh, w), g.dtype), in_specs=[ANY], out_specs=ANY,
        scratch_shapes=[pltpu.SemaphoreType.DMA, pltpu.SemaphoreType.DMA], name=name)(g)


def _add_sibling(g, got, *, name):
    n, r, w = g.shape
    rh = r // 2
    tr = 512
    nb = rh // tr
    cj = jnp.stack([lax.axis_index("c"), 2 * lax.axis_index("x") + lax.axis_index("y")]).astype(jnp.int32)

    def wire_body(cj_ref, g_ref, r_ref, a_ref):
        a_ref[...] = (g_ref[...] + r_ref[...]).astype(BF16)

    a16 = pl.pallas_call(
        wire_body, out_shape=jax.ShapeDtypeStruct((n, rh, w), BF16),
        grid_spec=pltpu.PrefetchScalarGridSpec(
            num_scalar_prefetch=1, grid=(n, nb),
            in_specs=[pl.BlockSpec((None, tr, w), lambda j, i, cj_ref: (j, cj_ref[0] * nb + i, 0)),
                      pl.BlockSpec((None, tr, w), lambda j, i, cj_ref: (j, i, 0))],
            out_specs=pl.BlockSpec((None, tr, w), lambda j, i, cj_ref: (j, i, 0))),
        name=name + "_wire", compiler_params=_cparams(("parallel", "parallel")))(cj, g, got)

    def own_body(cj_ref, g_ref, r_ref, o_ref):
        o_ref[...] = g_ref[...] + r_ref[...]

    own = pl.pallas_call(
        own_body, out_shape=jax.ShapeDtypeStruct((rh, w), F32),
        grid_spec=pltpu.PrefetchScalarGridSpec(
            num_scalar_prefetch=1, grid=(nb,),
            in_specs=[pl.BlockSpec((None, tr, w), lambda i, cj_ref: (cj_ref[1], cj_ref[0] * nb + i, 0)),
                      pl.BlockSpec((None, tr, w), lambda i, cj_ref: (cj_ref[1], i, 0))],
            out_specs=pl.BlockSpec((tr, w), lambda i, cj_ref: (i, 0))),
        name=name + "_own", compiler_params=_cparams(("parallel",)))(cj, g, got)
    return a16, own


def _exchange_chips(a16, *, name):
    n, rh, w = a16.shape

    def body(a_ref, o_ref, send, recv):
        x, y, c, chips = _coords()
        cps = [pltpu.make_async_remote_copy(src_ref=a_ref.at[2 * chip[0] + chip[1]], dst_ref=o_ref.at[k],
                                            send_sem=send.at[k], recv_sem=recv.at[k], device_id=(*chip, c),
                                            device_id_type=MESH_ID) for k, chip in enumerate(chips)]
        for cp in cps:
            cp.start()
        for cp in cps:
            cp.wait()

    return pl.pallas_call(
        body, out_shape=jax.ShapeDtypeStruct((N_CHIPS - 1, rh, w), a16.dtype), in_specs=[ANY], out_specs=ANY,
        scratch_shapes=[pltpu.SemaphoreType.DMA((3,)), pltpu.SemaphoreType.DMA((3,))], name=name)(a16)


def _add_chips(own, got, *, name):
    rh, w = own.shape
    tr = 512

    def body(o_ref, g_ref, s_ref):
        s = o_ref[...]
        for k in range(N_CHIPS - 1):
            s = s + g_ref[k].astype(F32)
        s_ref[...] = s

    return pl.pallas_call(
        body, out_shape=jax.ShapeDtypeStruct((rh, w), F32), grid=(rh // tr,),
        in_specs=[pl.BlockSpec((tr, w), lambda i: (i, 0)), pl.BlockSpec((N_CHIPS - 1, tr, w), lambda i: (0, i, 0))],
        out_specs=pl.BlockSpec((tr, w), lambda i: (i, 0)), name=name, compiler_params=_cparams(("parallel",)))(own, got)


def _join_halves(half, *, name):
    rh, w = half.shape

    def body(h_ref, o_ref, send, recv, local):
        x, y, c, _ = _coords()
        mine = pltpu.make_async_copy(h_ref, o_ref.at[pl.ds(c * rh, rh), :], local)
        mine.start()
        cp = pltpu.make_async_remote_copy(src_ref=h_ref, dst_ref=o_ref.at[pl.ds(c * rh, rh), :], send_sem=send,
                                          recv_sem=recv, device_id=(x, y, 1 - c), device_id_type=MESH_ID)
        cp.start()
        pltpu.make_async_remote_copy(src_ref=h_ref, dst_ref=o_ref.at[pl.ds((1 - c) * rh, rh), :], send_sem=send,
                                     recv_sem=recv, device_id=(x, y, 1 - c), device_id_type=MESH_ID).wait_recv()
        cp.wait_send()
        mine.wait()

    return pl.pallas_call(
        body, out_shape=jax.ShapeDtypeStruct((2 * rh, w), half.dtype), in_specs=[ANY], out_specs=ANY,
        scratch_shapes=[pltpu.SemaphoreType.DMA, pltpu.SemaphoreType.DMA, pltpu.SemaphoreType.DMA], name=name)(half)


def _gather_all(vec, *, name):
    s, w = vec.shape

    def body(v_ref, o_ref, send, recv, local):
        x, y, c, _ = _coords()
        me = 4 * x + 2 * y + c
        mine = pltpu.make_async_copy(v_ref, o_ref.at[me], local)
        mine.start()
        cps = []
        for k in range(1, N_DEV):
            px = 1 - x if k & 4 else x
            py = 1 - y if k & 2 else y
            pc = 1 - c if k & 1 else c
            peer = 4 * px + 2 * py + pc
            cps.append((pltpu.make_async_remote_copy(src_ref=v_ref, dst_ref=o_ref.at[me], send_sem=send.at[k - 1],
                                                     recv_sem=recv.at[k - 1], device_id=(px, py, pc),
                                                     device_id_type=MESH_ID),
                        pltpu.make_async_remote_copy(src_ref=v_ref, dst_ref=o_ref.at[peer], send_sem=send.at[k - 1],
                                                     recv_sem=recv.at[k - 1], device_id=(px, py, pc),
                                                     device_id_type=MESH_ID)))
        for snd, _ in cps:
            snd.start()
        for snd, rcv in cps:
            rcv.wait_recv()
            snd.wait_send()
        mine.wait()

    return pl.pallas_call(
        body, out_shape=jax.ShapeDtypeStruct((N_DEV, s, w), vec.dtype), in_specs=[ANY], out_specs=ANY,
        scratch_shapes=[pltpu.SemaphoreType.DMA((N_DEV - 1,)), pltpu.SemaphoreType.DMA((N_DEV - 1,)),
                        pltpu.SemaphoreType.DMA], name=name)(vec)


def _sum_all(parts, *, name):
    n, s, w = parts.shape
    tr = _pick(s, (96, 72, 48, 32, 24, 16, 8))

    def body(p_ref, o_ref):
        acc = p_ref[0]
        for k in range(1, n):
            acc = acc + p_ref[k]
        o_ref[...] = acc

    return pl.pallas_call(
        body, out_shape=jax.ShapeDtypeStruct((s, w), F32), grid=(s // tr,),
        in_specs=[pl.BlockSpec((n, tr, w), lambda i: (0, i, 0))], out_specs=pl.BlockSpec((tr, w), lambda i: (i, 0)),
        name=name, compiler_params=_cparams(("parallel",)))(parts)


def _pack_rows(pieces, rows):
    flat = jnp.concatenate([p.reshape(p.shape[0], -1) for p in pieces], axis=1)
    pad = rows * PACK_COLS - flat.shape[1]
    return jnp.pad(flat, ((0, 0), (0, pad))).reshape(flat.shape[0], rows, PACK_COLS)


def _unshard(piece, axis):
    return jnp.concatenate([piece[j] for j in range(N_CHIPS)], axis=axis)


def _to_shards(full, axis):
    return jnp.stack(jnp.split(full, N_CHIPS, axis=axis))


def _by_head(w, parts, heads):
    r = w.shape[0]
    return w.reshape(r, parts, heads, HEAD).transpose(0, 2, 1, 3).reshape(r, parts * heads * HEAD)


def _by_part(w, parts, heads):
    r = w.shape[0]
    return w.reshape(r, heads, parts, HEAD).transpose(0, 2, 1, 3).reshape(r, parts * heads * HEAD)


def kernel(x, ffn1_norm, ffn1_w_gate, ffn1_w_up, ffn1_w_down, mix_norm, ffn2_norm, ffn2_w_gate, ffn2_w_up, ffn2_w_down, ev_w_in, hg_lb_logits, hg_norm_w, s5_a_re, s5_a_im, s5_b_re, s5_b_im, s5_c_re, s5_c_im, s5_d, s5_log_dt, s5_w_glu, ev_w_out, od_w_in, gdn_conv_w, gdn_a_log, gdn_dt_bias, gdn_norm_w, od_w_out, final_norm, loss_target, m_ffn1_norm, m_ffn1_w_gate, m_ffn1_w_up, m_ffn1_w_down, m_mix_norm, m_ffn2_norm, m_ffn2_w_gate, m_ffn2_w_up, m_ffn2_w_down, m_ev_w_in, m_hg_lb_logits, m_hg_norm_w, m_s5_a_re, m_s5_a_im, m_s5_b_re, m_s5_b_im, m_s5_c_re, m_s5_c_im, m_s5_d, m_s5_log_dt, m_s5_w_glu, m_ev_w_out, m_od_w_in, m_gdn_conv_w, m_gdn_a_log, m_gdn_dt_bias, m_gdn_norm_w, m_od_w_out, m_final_norm, v_ffn1_norm, v_ffn1_w_gate, v_ffn1_w_up, v_ffn1_w_down, v_mix_norm, v_ffn2_norm, v_ffn2_w_gate, v_ffn2_w_up, v_ffn2_w_down, v_ev_w_in, v_hg_lb_logits, v_hg_norm_w, v_s5_a_re, v_s5_a_im, v_s5_b_re, v_s5_b_im, v_s5_c_re, v_s5_c_im, v_s5_d, v_s5_log_dt, v_s5_w_glu, v_ev_w_out, v_od_w_in, v_gdn_conv_w, v_gdn_a_log, v_gdn_dt_bias, v_gdn_norm_w, v_od_w_out, v_final_norm):
    args = locals()
    wts = {n: args[n] for n in WEIGHTS}
    mom = {n: args["m_" + n] for n in WEIGHTS}
    var = {n: args["v_" + n] for n in WEIGHTS}
    bsz, seq, _ = x.shape
    t = bsz * seq

    def wire(n):
        if n == "gdn_conv_w":
            return lax.bitcast_convert_type(wts[n], BF16)
        return wts[n].astype(BF16)

    pack = _pack_rows([wire(n)[None] for n, _ in SHARDED], PACK_ROWS)[0]
    gathered = _all_gather_chips(pack, name="gather_weights").reshape(N_CHIPS, -1)
    full, off = {}, 0
    for n, axis in SHARDED:
        shp = wts[n].shape + ((2,) if n == "gdn_conv_w" else ())
        size = int(np.prod(shp))
        piece = gathered[:, off:off + size].reshape((N_CHIPS,) + shp)
        off += size
        if n == "gdn_conv_w":
            piece = lax.bitcast_convert_type(piece, F32)
        full[n] = _unshard(piece, axis)

    ev_cols = lambda w: jnp.concatenate([_by_head(w[:, :4 * HG_W], 4, HG_HEADS), w[:, 4 * HG_W:]], axis=1)
    ev_cols_back = lambda w: jnp.concatenate([_by_part(w[:, :4 * HG_W], 4, HG_HEADS), w[:, 4 * HG_W:]], axis=1)

    lbs_fn = lambda lg: (lambda p: jnp.cumsum(p, axis=0) - p[0])(jax.nn.softmax(lg, axis=0))
    lbs, lbs_vjp = jax.vjp(lbs_fn, hg_lb_logits)
    lc = min(S5_LC, seq)

    xs = x.reshape(t, D_MODEL)
    saved = []
    for layer in range(DEPTH):
        j = layer // 2
        rec = {}
        xs, rec["ffn1"] = _ffn_fwd(xs, ffn1_norm[layer][None], full["ffn1_w_gate"][layer], full["ffn1_w_up"][layer],
                                   full["ffn1_w_down"][layer], f"a{layer}")
        rec["x_mix"] = xs
        h = _rms_fwd(xs, mix_norm[layer][None], name=f"rms_fwd_mix{layer}")
        rec["h"] = h
        if layer % 2 == 0:
            w_in = ev_cols(full["ev_w_in"][j])
            proj = _mm(h, w_in, "nn", name=f"ev_proj{layer}").reshape(bsz, seq, -1)
            rec["proj"] = proj
            y_a, rec["hg_states"] = _hg_fwd(proj, lbs[j][None], hg_norm_w[j][None], name=f"hgrn2_fwd{layer}")
            disc, rec["disc_vjp"] = jax.vjp(_s5_disc, s5_a_re[j], s5_a_im[j], s5_b_re[j], s5_b_im[j], s5_log_dt[j])
            wb_re, wb_im = _block_diag_in(disc[2]).astype(BF16), _block_diag_in(disc[3]).astype(BF16)
            wc_re = _block_diag_out(s5_c_re[j]).astype(BF16)
            wc_im = _block_diag_out(-s5_c_im[j]).astype(BF16)
            pw_f, pw_b = _s5_powers(s5_a_re[j], s5_a_im[j], s5_log_dt[j], lc)
            u = proj[:, :, 4 * HG_W:].reshape(t, S5_W)
            bu_re = _mm(u, wb_re, "nn", name=f"s5_bu_re{layer}").reshape(bsz, seq, S5_N)
            bu_im = _mm(u, wb_im, "nn", name=f"s5_bu_im{layer}").reshape(bsz, seq, S5_N)
            h_re, h_im = _s5_scan(bu_re, bu_im, pw_f, reverse=False, name=f"s5_scan_fwd{layer}")
            h_re2, h_im2 = h_re.reshape(t, S5_N), h_im.reshape(t, S5_N)
            ych = _mm(h_re2, wc_re, "nn", name=f"s5_out_re{layer}")
            ych = _mm(h_im2, wc_im, "nn", res=ych, name=f"s5_out_im{layer}")
            w_glu = full["s5_w_glu"][j]
            y_b = _s5_post_fwd(ych, u, s5_d[j][None], w_glu, name=f"s5_post_fwd{layer}")
            rec.update(u=u, h_re=h_re, h_im=h_im, ych=ych, wb=(wb_re, wb_im), wc=(wc_re, wc_im), pw_b=pw_b,
                       y_a=y_a.reshape(t, HG_W), y_b=y_b)
            w_out = full["ev_w_out"][j]
            xs = _mm(rec["y_a"], w_out[:HG_W], "nn", res=xs, name=f"ev_out_a{layer}")
            xs = _mm(y_b, w_out[HG_W:], "nn", res=xs, name=f"ev_out_b{layer}")
        else:
            w_in = full["od_w_in"][j]
            w_main = _by_head(w_in[:, :4 * GDN_W], 4, GDN_HEADS)
            w_bd = jnp.pad(w_in[:, 4 * GDN_W:], ((0, 0), (0, HEAD - 2 * GDN_HEADS)))
            proj = _mm(h, w_main, "nn", name=f"od_proj{layer}").reshape(bsz, seq, -1)
            bd = _mm(h, w_bd, "nn", name=f"od_proj_bd{layer}").reshape(bsz, seq, HEAD)
            conv_w = full["gdn_conv_w"][j].reshape(CONV_W, 3, GDN_HEADS, HEAD).transpose(0, 2, 1, 3)
            conv_w = jnp.pad(conv_w, ((0, SUBLANE - CONV_W), (0, 0), (0, 1), (0, 0))).reshape(SUBLANE, 4 * GDN_W)
            qkvg = _conv_fwd(proj, conv_w, name=f"gdn_conv_fwd{layer}")
            ab = jnp.pad(gdn_a_log[j][None], ((0, 0), (GDN_HEADS, HEAD - 2 * GDN_HEADS)))
            dtb = jnp.pad(gdn_dt_bias[j][None], ((0, 0), (GDN_HEADS, HEAD - 2 * GDN_HEADS)))
            y, states = _gdn_fwd(qkvg, bd, ab, dtb, gdn_norm_w[j][None], name=f"gdn_fwd{layer}")
            rec.update(proj=proj, bd=bd, conv_w=conv_w, qkvg=qkvg, ab=ab, dtb=dtb, states=states,
                       y=y.reshape(t, GDN_W), w_main=w_main, w_bd=w_bd)
            xs = _mm(rec["y"], full["od_w_out"][j], "nn", res=xs, name=f"od_out{layer}")
        xs, rec["ffn2"] = _ffn_fwd(xs, ffn2_norm[layer][None], full["ffn2_w_gate"][layer], full["ffn2_w_up"][layer],
                                   full["ffn2_w_down"][layer], f"b{layer}")
        saved.append(rec)

    loss_part, dx, d_final = _final_loss(xs, final_norm[None], loss_target.reshape(t, D_MODEL), name="loss_head")
    loss = lax.psum(loss_part[0, 0], ("x", "y", "c"))

    grads = {n: [None] * wts[n].shape[0] for n in WEIGHTS if n != "final_norm"}
    grads["final_norm"] = d_final[0]
    d_lbs = [None] * 2
    for layer in reversed(range(DEPTH)):
        j = layer // 2
        rec = saved[layer]
        dx, dn, dwg, dwu, dwd = _ffn_bwd(rec["ffn2"], ffn2_norm[layer][None], full["ffn2_w_gate"][layer],
                                         full["ffn2_w_up"][layer], full["ffn2_w_down"][layer], dx, f"b{layer}")
        grads["ffn2_norm"][layer], grads["ffn2_w_gate"][layer] = dn[0], dwg
        grads["ffn2_w_up"][layer], grads["ffn2_w_down"][layer] = dwu, dwd
        h = rec["h"]
        if layer % 2 == 0:
            w_out = full["ev_w_out"][j]
            dy_a = _mm(dx, w_out[:HG_W], "nt", name=f"ev_dya{layer}")
            dy_b = _mm(dx, w_out[HG_W:], "nt", name=f"ev_dyb{layer}")
            dw_out = jnp.concatenate([_mm(rec["y_a"], dx, "tn", name=f"ev_dwo_a{layer}"),
                                      _mm(rec["y_b"], dx, "tn", name=f"ev_dwo_b{layer}")], axis=0)
            grads["ev_w_out"][j] = dw_out
            wb_re, wb_im = rec["wb"]
            wc_re, wc_im = rec["wc"]
            dych, du, dd, dwglu = _s5_post_bwd(rec["ych"], rec["u"], s5_d[j][None], full["s5_w_glu"][j], dy_b,
                                               name=f"s5_post_bwd{layer}")
            grads["s5_d"][j], grads["s5_w_glu"][j] = dd[0], dwglu
            h_re2, h_im2 = rec["h_re"].reshape(t, S5_N), rec["h_im"].reshape(t, S5_N)
            dwc_re = _mm(h_re2, dych, "tn", name=f"s5_dwc_re{layer}")
            dwc_im = _mm(h_im2, dych, "tn", name=f"s5_dwc_im{layer}")
            dh_re = _mm(dych, wc_re, "nt", name=f"s5_dh_re{layer}").reshape(bsz, seq, S5_N)
            dh_im = _mm(dych, wc_im, "nt", name=f"s5_dh_im{layer}").reshape(bsz, seq, S5_N)
            g_re, g_im = _s5_scan(dh_re, dh_im, rec["pw_b"], reverse=True, name=f"s5_scan_bwd{layer}")
            da = _s5_dabar(g_re, g_im, rec["h_re"], rec["h_im"], name=f"s5_dabar{layer}").sum(axis=0)
            g_re2, g_im2 = g_re.reshape(t, S5_N), g_im.reshape(t, S5_N)
            dwb_re = _mm(rec["u"], g_re2, "tn", name=f"s5_dwb_re{layer}")
            dwb_im = _mm(rec["u"], g_im2, "tn", name=f"s5_dwb_im{layer}")
            du = _mm(g_re2, wb_re, "nt", res=du, name=f"s5_du_re{layer}")
            du = _mm(g_im2, wb_im, "nt", res=du, name=f"s5_du_im{layer}")
            diag_in = lambda m: jnp.einsum("gpgn->gnp", m.reshape(S5_GROUPS, S5_GROUP, S5_GROUPS, S5_STATE))
            diag_out = lambda m: jnp.einsum("gngp->gpn", m.reshape(S5_GROUPS, S5_STATE, S5_GROUPS, S5_GROUP))
            d_disc = (da[0].reshape(S5_GROUPS, S5_STATE), da[1].reshape(S5_GROUPS, S5_STATE), diag_in(dwb_re),
                      diag_in(dwb_im))
            ga_re, ga_im, gb_re, gb_im, g_dt = rec["disc_vjp"](d_disc)
            grads["s5_a_re"][j], grads["s5_a_im"][j], grads["s5_b_re"][j] = ga_re, ga_im, gb_re
            grads["s5_b_im"][j], grads["s5_log_dt"][j] = gb_im, g_dt
            grads["s5_c_re"][j], grads["s5_c_im"][j] = diag_out(dwc_re), -diag_out(dwc_im)
            dproj_hg, dlb, dnw = _hg_bwd(rec["proj"], lbs[j][None], hg_norm_w[j][None], rec["hg_states"],
                                         dy_a.reshape(bsz, seq, HG_W), name=f"hgrn2_bwd{layer}")
            d_lbs[j] = dlb.sum(axis=0).reshape(HG_W)
            grads["hg_norm_w"][j] = dnw.sum(axis=(0, 1, 2))
            dproj = jnp.concatenate([dproj_hg.reshape(t, 4 * HG_W), du], axis=1)
            w_in = ev_cols(full["ev_w_in"][j])
            dw_in = _mm(h, dproj, "tn", name=f"ev_dwin{layer}")
            grads["ev_w_in"][j] = ev_cols_back(dw_in)
            dh = _mm(dproj, w_in, "nt", name=f"ev_dh{layer}")
        else:
            dy = _mm(dx, full["od_w_out"][j], "nt", name=f"od_dy{layer}")
            grads["od_w_out"][j] = _mm(rec["y"], dx, "tn", name=f"od_dwo{layer}")
            dqkvg, dbd_h, dab, ddt, dnw = _gdn_bwd(rec["qkvg"], rec["bd"], rec["ab"], rec["dtb"],
                                                   gdn_norm_w[j][None], rec["states"],
                                                   dy.reshape(bsz, seq, GDN_W), name=f"gdn_bwd{layer}")
            grads["gdn_norm_w"][j] = dnw.sum(axis=(0, 1, 2))
            grads["gdn_a_log"][j] = dab.sum(axis=(0, 1, 2))[GDN_HEADS:2 * GDN_HEADS]
            grads["gdn_dt_bias"][j] = ddt.sum(axis=(0, 1, 2))[GDN_HEADS:2 * GDN_HEADS]
            dbd = dbd_h.sum(axis=1).reshape(t, HEAD)
            dproj, dconv = _conv_bwd(rec["proj"], rec["conv_w"], dqkvg, name=f"gdn_conv_bwd{layer}")
            dconv = dconv.sum(axis=0)[:CONV_W].reshape(CONV_W, GDN_HEADS, 4, HEAD)[:, :, :3]
            grads["gdn_conv_w"][j] = dconv.transpose(0, 2, 1, 3).reshape(CONV_W, 3 * GDN_W)
            dproj = dproj.reshape(t, 4 * GDN_W)
            dw_main = _by_part(_mm(h, dproj, "tn", name=f"od_dwin{layer}"), 4, GDN_HEADS)
            dw_bd = _mm(h, dbd, "tn", name=f"od_dwbd{layer}")[:, :2 * GDN_HEADS]
            grads["od_w_in"][j] = jnp.concatenate([dw_main, dw_bd], axis=1)
            dh = _mm(dproj, rec["w_main"], "nt", name=f"od_dh{layer}")
            dh = _mm(dbd, rec["w_bd"], "nt", res=dh, name=f"od_dh_bd{layer}")
        dx, dn = _rms_bwd(rec["x_mix"], mix_norm[layer][None], dh, dx, name=f"rms_bwd_mix{layer}")
        grads["mix_norm"][layer] = dn[0]
        dx, dn, dwg, dwu, dwd = _ffn_bwd(rec["ffn1"], ffn1_norm[layer][None], full["ffn1_w_gate"][layer],
                                         full["ffn1_w_up"][layer], full["ffn1_w_down"][layer], dx, f"a{layer}")
        grads["ffn1_norm"][layer], grads["ffn1_w_gate"][layer] = dn[0], dwg
        grads["ffn1_w_up"][layer], grads["ffn1_w_down"][layer] = dwu, dwd
    grad_x = dx.reshape(x.shape)
    (grads["hg_lb_logits"],) = lbs_vjp(jnp.stack(d_lbs))
    for n in WEIGHTS:
        if isinstance(grads[n], list):
            grads[n] = jnp.stack(grads[n])

    gpack = _pack_rows([_to_shards(grads[n], axis) for n, axis in SHARDED], PACK_ROWS)
    got = _swap_halves(gpack, name="reduce_swap_halves")
    a16, own = _add_sibling(gpack, got, name="reduce_add_sibling")
    others = _exchange_chips(a16, name="reduce_exchange_chips")
    half = _add_chips(own, others, name="reduce_add_chips")
    red = _join_halves(half, name="reduce_join_halves").reshape(-1)
    off = 0
    for n, _ in SHARDED:
        size = int(np.prod(wts[n].shape))
        grads[n] = red[off:off + size].reshape(wts[n].shape)
        off += size

    small = jnp.concatenate([grads[n].reshape(-1) for n in SMALL])
    small = jnp.pad(small, (0, SMALL_ROWS * PACK_COLS - small.shape[0])).reshape(SMALL_ROWS, PACK_COLS)
    small = _sum_all(_gather_all(small, name="gather_small_grads"), name="sum_small_grads").reshape(-1)
    off = 0
    for n in SMALL:
        size = int(np.prod(wts[n].shape))
        grads[n] = small[off:off + size].reshape(wts[n].shape)
        off += size

    small_w = jnp.concatenate([wts[n].reshape(-1) for n in SMALL])
    small_m = jnp.concatenate([mom[n].reshape(-1) for n in SMALL])
    small_v = jnp.concatenate([var[n].reshape(-1) for n in SMALL])
    n_small = small_w.shape[0]
    padv = lambda a, fill: jnp.pad(a, (0, SMALL_ROWS * PACK_COLS - n_small), constant_values=fill).reshape(
        SMALL_ROWS, PACK_COLS)
    sd, sm, sv = _adamw(padv(small_w, 0.0), small.reshape(SMALL_ROWS, PACK_COLS), padv(small_m, 0.0),
                        padv(small_v, 1.0), name="adamw_small")
    delta, new_m, new_v = {}, {}, {}
    off = 0
    for n in SMALL:
        size = int(np.prod(wts[n].shape))
        delta[n] = sd.reshape(-1)[off:off + size].reshape(wts[n].shape)
        new_m[n] = sm.reshape(-1)[off:off + size].reshape(wts[n].shape)
        new_v[n] = sv.reshape(-1)[off:off + size].reshape(wts[n].shape)
        off += size
    for n, _ in SHARDED:
        delta[n], new_m[n], new_v[n] = _adamw(wts[n], grads[n], mom[n], var[n], name=f"adamw_{n}")

    return (loss, grad_x, *[grads[n] for n in WEIGHTS], *[delta[n] for n in WEIGHTS],
            *[new_m[n] for n in WEIGHTS], *[new_v[n] for n in WEIGHTS])
```

```python
import functools
import math

import jax
import jax.numpy as jnp
import numpy as np
from jax import lax
from jax.experimental import pallas as pl
from jax.experimental.pallas import tpu as pltpu

F32 = jnp.float32
BF16 = jnp.bfloat16
HI = lax.Precision.HIGHEST
MESH_ID = pl.DeviceIdType.MESH

D_MODEL = 1024
D_FF = 2816
DEPTH = 4
NORM_EPS = 1e-6
F_MIN = 1e-6
CHUNK = 64
HEAD = 128
HG_HEADS = 4
HG_W = 512
S5_W = 512
S5_GROUP = 16
S5_GROUPS = 32
S5_STATE = 64
S5_N = S5_GROUPS * S5_STATE
GDN_HEADS = 8
GDN_W = 1024
CONV_W = 4
N_CHIPS = 4
N_DEV = 8
LANE = 128
SUBLANE = 8
VMEM_LIMIT = 56 * 1024 * 1024
PACK_COLS = 1024
ROW_BLOCK = 512
ROW_BLOCKS = 10
SMALL_ROWS = 288

ADAM_LR = 0.001
ADAM_B1 = 0.9
ADAM_B2 = 0.999
ADAM_EPS = 1e-08
ADAM_WD = 0.01
ADAM_STEP = 10

SHARDED = (("ffn1_w_gate", 2), ("ffn1_w_up", 2), ("ffn1_w_down", 1), ("ffn2_w_gate", 2), ("ffn2_w_up", 2),
           ("ffn2_w_down", 1), ("ev_w_in", 2), ("s5_w_glu", 1), ("ev_w_out", 1), ("od_w_in", 2),
           ("gdn_conv_w", 2), ("od_w_out", 1))
WIDE = ("ffn1_w_gate", "ffn1_w_up", "ffn2_w_gate", "ffn2_w_up")
ROWS = (("ffn1_w_down", 1), ("ffn2_w_down", 1), ("ev_w_out", 1), ("od_w_out", 1), ("s5_w_glu", 1), ("ev_w_in", 2),
        ("od_w_in", 2), ("gdn_conv_w", 2))
SMALL = ("ffn1_norm", "mix_norm", "ffn2_norm", "hg_lb_logits", "hg_norm_w", "s5_a_re", "s5_a_im", "s5_b_re",
         "s5_b_im", "s5_c_re", "s5_c_im", "s5_d", "s5_log_dt", "gdn_a_log", "gdn_dt_bias", "gdn_norm_w",
         "final_norm")
WEIGHTS = ("ffn1_norm", "ffn1_w_gate", "ffn1_w_up", "ffn1_w_down", "mix_norm", "ffn2_norm", "ffn2_w_gate",
           "ffn2_w_up", "ffn2_w_down", "ev_w_in", "hg_lb_logits", "hg_norm_w", "s5_a_re", "s5_a_im", "s5_b_re",
           "s5_b_im", "s5_c_re", "s5_c_im", "s5_d", "s5_log_dt", "s5_w_glu", "ev_w_out", "od_w_in", "gdn_conv_w",
           "gdn_a_log", "gdn_dt_bias", "gdn_norm_w", "od_w_out", "final_norm")


def _cparams(sem=None):
    return pltpu.CompilerParams(dimension_semantics=sem, vmem_limit_bytes=VMEM_LIMIT)


def _dg(a, b, ca, cb, hi):
    if hi:
        return lax.dot_general(a.astype(F32), b.astype(F32), (((ca,), (cb,)), ((), ())), precision=HI,
                               preferred_element_type=F32)
    return lax.dot_general(a.astype(BF16), b.astype(BF16), (((ca,), (cb,)), ((), ())), preferred_element_type=F32)


@functools.partial(jax.custom_vjp, nondiff_argnums=(2,))
def mm_nn(a, b, hi=False):
    return _dg(a, b, 1, 0, hi)


@functools.partial(jax.custom_vjp, nondiff_argnums=(2,))
def mm_nt(a, b, hi=False):
    return _dg(a, b, 1, 1, hi)


@functools.partial(jax.custom_vjp, nondiff_argnums=(2,))
def mm_tn(a, b, hi=False):
    return _dg(a, b, 0, 0, hi)


mm_nn.defvjp(lambda a, b, hi: (_dg(a, b, 1, 0, hi), (a, b)),
             lambda hi, r, g: (mm_nt(g, r[1], hi), mm_tn(r[0], g, hi)))
mm_nt.defvjp(lambda a, b, hi: (_dg(a, b, 1, 1, hi), (a, b)),
             lambda hi, r, g: (mm_nn(g, r[1], hi), mm_tn(g, r[0], hi)))
mm_tn.defvjp(lambda a, b, hi: (_dg(a, b, 0, 0, hi), (a, b)),
             lambda hi, r, g: (mm_nt(r[1], g, hi), mm_nn(r[0], g, hi)))


def _sigmoid(x):
    return 1.0 / (1.0 + jnp.exp(-x))


def _silu(x):
    return x * _sigmoid(x)


def _softplus(x):
    return jnp.maximum(x, 0.0) + jnp.log(1.0 + jnp.exp(-jnp.abs(x)))


def _gelu(x):
    return 0.5 * x * (1.0 + jnp.tanh(math.sqrt(2.0 / math.pi) * (x + 0.044715 * (x * x * x))))


def _iota(shape, dim):
    return lax.broadcasted_iota(jnp.int32, shape, dim)


def _l2norm(t):
    return t * lax.rsqrt(jnp.sum(t * t, axis=-1, keepdims=True) + NORM_EPS)


def gated_head_norm(o, gate, nw):
    o = o * lax.rsqrt(jnp.mean(o * o, axis=-1, keepdims=True) + NORM_EPS) * nw
    return o * _silu(gate)


HG_SUB = 8
HG_MID = 3


def hg_chunk(st, ql, fl, v, gl, lb, nw):
    c = CHUNK
    q = _silu(ql)
    f = lb + (1.0 - lb) * _sigmoid(fl)
    lf = jnp.log(jnp.maximum(f, F_MIN))
    k = 1.0 - f
    t_i = _iota((2 * c, c), 0)
    s_i = _iota((2 * c, c), 1)
    lim = jnp.where(t_i < c, t_i, ((t_i - c) // HG_SUB) * HG_SUB + HG_MID)
    cum = mm_nn(jnp.where(s_i <= lim, 1.0, 0.0).astype(F32), lf, True)
    b, rb = cum[:c], cum[c:]
    qp = q * jnp.exp(b - rb)
    row = _iota((c, HEAD), 0)
    blocks = []
    for i in range(c // HG_SUB):
        r_i = jnp.tile(rb[i * HG_SUB:(i + 1) * HG_SUB], (c // HG_SUB, 1))
        seen = row < (i + 1) * HG_SUB
        kp = k * jnp.exp(jnp.where(seen, r_i - b, 0.0))
        blocks.append(mm_nt(qp[i * HG_SUB:(i + 1) * HG_SUB], kp))
    a = jnp.concatenate(blocks, axis=0)
    a = jnp.where(_iota((c, c), 1) <= _iota((c, c), 0), a, 0.0)
    o = mm_nn(a, v) + mm_nt(q * jnp.exp(b), st)
    bl = jnp.sum(lf, axis=0, keepdims=True)
    st_new = st * jnp.exp(bl) + mm_tn(v, k * jnp.exp(bl - b))
    return st_new, gated_head_norm(o, gl, nw)


def gdn_chunk(st, qc, kc, vc, gate, bd, ab, dtb, nw, mb, ma):
    c = CHUNK
    beta = _sigmoid(jnp.sum(bd * mb, axis=-1, keepdims=True))
    a_lin = jnp.sum(bd * ma, axis=-1, keepdims=True)
    a_log = jnp.sum(ab * ma, axis=-1, keepdims=True)
    dt_b = jnp.sum(dtb * ma, axis=-1, keepdims=True)
    la = -jnp.exp(a_log) * _softplus(a_lin + dt_b)
    la_b = jnp.broadcast_to(la, (c, HEAD))
    t_i = _iota((c, c), 0)
    s_i = _iota((c, c), 1)
    lower = s_i <= t_i
    strict = s_i < t_i
    g = mm_nn(jnp.where(lower, 1.0, 0.0).astype(F32), la_b, True)
    g_row = mm_nt(jnp.full((c, HEAD), 1.0 / HEAD, F32), g, True)
    d = g[:, :c] - g_row
    lm = jnp.where(lower, jnp.exp(jnp.where(lower, d, 0.0)), 0.0)
    q = _l2norm(qc) * (HEAD ** -0.5)
    k = _l2norm(kc)
    kb = k * beta
    vb = vc * beta
    m = jnp.where(strict, mm_nt(kb, k) * lm, 0.0)
    eye = jnp.where(s_i == t_i, 1.0, 0.0).astype(F32)
    t_inv = eye - m
    p = m
    for _ in range(int(math.log2(c)) - 1):
        p = mm_nn(p, p, True)
        t_inv = t_inv + mm_nn(t_inv, p, True)
    u = mm_nn(t_inv, vb)
    w = mm_nn(t_inv, kb * jnp.exp(g))
    attn = mm_nt(q, k) * lm
    v_new = u - mm_nt(w, st)
    o = mm_nt(q * jnp.exp(g), st) + mm_nn(attn, v_new)
    g_last = jnp.sum(la_b, axis=0, keepdims=True)
    st_new = st * jnp.exp(g_last) + mm_tn(v_new, k * jnp.exp(g_last - g))
    return st_new, gated_head_norm(o, gate, nw)


def s5_post(ych, u, d, wglu):
    y = _gelu(ych + d * u)
    return y * _sigmoid(mm_nn(y, wglu))


def _pick(n, cands):
    for c in cands:
        if n % c == 0:
            return c
    return n


def _mm(a, b, mode, *, name, out_dtype=F32, res=None, scale=1.0, tm=None, tn=None, tk=None):
    if mode == "nn":
        (m, k), (k2, n) = a.shape, b.shape
    elif mode == "nt":
        (m, k), (n, k2) = a.shape, b.shape
    else:
        (k, m), (k2, n) = a.shape, b.shape
    assert k == k2, (a.shape, b.shape, mode)
    tm = tm or _pick(m, (512, 1408, 256, 128) if mode == "tn" else (512, 704, 256, 128))
    tn = tn or _pick(n, (1024, 1408, 512, 256, 128))
    tk = tk or _pick(k, (1024, 1408, 512, 256, 128))
    nk = k // tk
    grid = (n // tn, m // tm, nk)
    if mode == "tn":
        a_spec = pl.BlockSpec((tk, tm), lambda j, i, kk: (kk, i))
    else:
        a_spec = pl.BlockSpec((tm, tk), lambda j, i, kk: (i, kk))
    if mode == "nt":
        b_spec = pl.BlockSpec((tn, tk), lambda j, i, kk: (j, kk))
    else:
        b_spec = pl.BlockSpec((tk, tn), lambda j, i, kk: (kk, j))
    o_spec = pl.BlockSpec((tm, tn), lambda j, i, kk: (i, j))
    ca = 0 if mode == "tn" else 1
    cb = 1 if mode == "nt" else 0
    has_res = res is not None

    def body(*refs):
        if has_res:
            a_ref, b_ref, r_ref, o_ref, acc = refs
        else:
            a_ref, b_ref, o_ref, acc = refs
        kk = pl.program_id(2)

        @pl.when(kk == 0)
        def _():
            acc[...] = jnp.zeros_like(acc)

        acc[...] += _dg(a_ref[...], b_ref[...], ca, cb, False)

        @pl.when(kk == nk - 1)
        def _():
            r = acc[...] * scale if scale != 1.0 else acc[...]
            if has_res:
                r = r_ref[...] + r
            o_ref[...] = r.astype(out_dtype)

    in_specs = [a_spec, b_spec] + ([o_spec] if has_res else [])
    args = (a, b) + ((res,) if has_res else ())
    return pl.pallas_call(
        body, out_shape=jax.ShapeDtypeStruct((m, n), out_dtype), grid=grid, in_specs=in_specs, out_specs=o_spec,
        scratch_shapes=[pltpu.VMEM((tm, tn), F32)], name=name,
        compiler_params=_cparams(("parallel", "parallel", "arbitrary")))(*args)


def _rms_fwd(x, w, *, name):
    t, d = x.shape
    tm = _pick(t, (512, 256, 128))

    def body(x_ref, w_ref, o_ref):
        xv = x_ref[...]
        r = lax.rsqrt(jnp.mean(xv * xv, axis=-1, keepdims=True) + NORM_EPS)
        o_ref[...] = (xv * r * w_ref[...]).astype(BF16)

    return pl.pallas_call(
        body, out_shape=jax.ShapeDtypeStruct((t, d), BF16), grid=(t // tm,),
        in_specs=[pl.BlockSpec((tm, d), lambda i: (i, 0)), pl.BlockSpec((1, d), lambda i: (0, 0))],
        out_specs=pl.BlockSpec((tm, d), lambda i: (i, 0)), name=name, compiler_params=_cparams(("parallel",)))(x, w)


def _rms_bwd(x, w, dh, dres, *, name):
    t, d = x.shape
    tm = _pick(t, (512, 256, 128))

    def body(x_ref, w_ref, dh_ref, dr_ref, dx_ref, dw_ref):
        xv = x_ref[...]
        r = lax.rsqrt(jnp.mean(xv * xv, axis=-1, keepdims=True) + NORM_EPS)
        xh = xv * r
        dhv = dh_ref[...]
        dxh = dhv * w_ref[...]
        dx_ref[...] = dr_ref[...] + r * (dxh - xh * jnp.mean(dxh * xh, axis=-1, keepdims=True))

        @pl.when(pl.program_id(0) == 0)
        def _():
            dw_ref[...] = jnp.zeros_like(dw_ref)

        dw_ref[...] += jnp.sum(dhv * xh, axis=0, keepdims=True)

    row = pl.BlockSpec((tm, d), lambda i: (i, 0))
    vec = pl.BlockSpec((1, d), lambda i: (0, 0))
    return pl.pallas_call(
        body, out_shape=[jax.ShapeDtypeStruct((t, d), F32), jax.ShapeDtypeStruct((1, d), F32)], grid=(t // tm,),
        in_specs=[row, vec, row, row], out_specs=[row, vec], name=name,
        compiler_params=_cparams(("arbitrary",)))(x, w, dh, dres)


def _final_loss(x, w, target, *, name):
    t, d = x.shape
    tm = _pick(t, (512, 256, 128))

    def body(x_ref, w_ref, t_ref, l_ref, dx_ref, dw_ref):
        xv = x_ref[...]
        wv = w_ref[...]
        r = lax.rsqrt(jnp.mean(xv * xv, axis=-1, keepdims=True) + NORM_EPS)
        xh = xv * r
        err = xh * wv - t_ref[...]
        dy = err * (1.0 / d)
        dxh = dy * wv
        dx_ref[...] = r * (dxh - xh * jnp.mean(dxh * xh, axis=-1, keepdims=True))

        @pl.when(pl.program_id(0) == 0)
        def _():
            dw_ref[...] = jnp.zeros_like(dw_ref)
            l_ref[...] = jnp.zeros_like(l_ref)

        dw_ref[...] += jnp.sum(dy * xh, axis=0, keepdims=True)
        l_ref[...] += 0.5 * jnp.sum(jnp.mean(err * err, axis=-1, keepdims=True), axis=0, keepdims=True)

    row = pl.BlockSpec((tm, d), lambda i: (i, 0))
    vec = pl.BlockSpec((1, d), lambda i: (0, 0))
    return pl.pallas_call(
        body, out_shape=[jax.ShapeDtypeStruct((SUBLANE, LANE), F32), jax.ShapeDtypeStruct((t, d), F32),
                         jax.ShapeDtypeStruct((1, d), F32)], grid=(t // tm,),
        in_specs=[row, vec, row], out_specs=[pl.BlockSpec((SUBLANE, LANE), lambda i: (0, 0)), row, vec], name=name,
        compiler_params=_cparams(("arbitrary",)))(x, w, target)


def _ffn_up(h, wg, wu, *, name):
    t, d = h.shape
    f = wg.shape[1]
    tm = _pick(t, (512, 256, 128))
    tn = _pick(f, (1408, 512, 256, 128))

    def body(h_ref, wg_ref, wu_ref, g_ref, u_ref, a_ref):
        hv = h_ref[...]
        g = _dg(hv, wg_ref[...], 1, 0, False)
        u = _dg(hv, wu_ref[...], 1, 0, False)
        g_ref[...] = g
        u_ref[...] = u
        a_ref[...] = (_silu(g) * u).astype(BF16)

    hs = pl.BlockSpec((tm, d), lambda j, i: (i, 0))
    ws = pl.BlockSpec((d, tn), lambda j, i: (0, j))
    os_ = pl.BlockSpec((tm, tn), lambda j, i: (i, j))
    return pl.pallas_call(
        body, out_shape=[jax.ShapeDtypeStruct((t, f), F32), jax.ShapeDtypeStruct((t, f), F32),
                         jax.ShapeDtypeStruct((t, f), BF16)], grid=(f // tn, t // tm),
        in_specs=[hs, ws, ws], out_specs=[os_, os_, os_], name=name,
        compiler_params=_cparams(("parallel", "parallel")))(h, wg, wu)


def _ffn_dact(dxo, wd, g, u, *, name):
    t, d = dxo.shape
    f = wd.shape[0]
    tm = _pick(t, (512, 256, 128))
    tn = _pick(f, (1408, 512, 256, 128))

    def body(dx_ref, wd_ref, g_ref, u_ref, dg_ref, du_ref):
        da = 0.5 * _dg(dx_ref[...], wd_ref[...], 1, 1, False)
        gv = g_ref[...]
        sg = _sigmoid(gv)
        dg_ref[...] = (da * u_ref[...] * (sg * (1.0 + gv * (1.0 - sg)))).astype(BF16)
        du_ref[...] = (da * (gv * sg)).astype(BF16)

    xs = pl.BlockSpec((tm, d), lambda j, i: (i, 0))
    ws = pl.BlockSpec((tn, d), lambda j, i: (j, 0))
    os_ = pl.BlockSpec((tm, tn), lambda j, i: (i, j))
    return pl.pallas_call(
        body, out_shape=[jax.ShapeDtypeStruct((t, f), BF16), jax.ShapeDtypeStruct((t, f), BF16)],
        grid=(f // tn, t // tm), in_specs=[xs, ws, os_, os_], out_specs=[os_, os_], name=name,
        compiler_params=_cparams(("parallel", "parallel")))(dxo, wd, g, u)


def _ffn_fwd(x, nw, wg, wu, wd, tag):
    h = _rms_fwd(x, nw, name=f"rms_fwd_{tag}")
    g, u, a = _ffn_up(h, wg, wu, name=f"ffn_up_{tag}")
    y = _mm(a, wd, "nn", res=x, scale=0.5, name=f"ffn_down_{tag}")
    return y, (x, h, g, u, a)


def _ffn_bwd(saved, nw, wg, wu, wd, dxo, tag):
    x, h, g, u, a = saved
    dg, du = _ffn_dact(dxo, wd, g, u, name=f"ffn_dact_{tag}")
    dwd = _mm(a, dxo, "tn", scale=0.5, name=f"ffn_dwd_{tag}")
    dwg = _mm(h, dg, "tn", name=f"ffn_dwg_{tag}")
    dwu = _mm(h, du, "tn", name=f"ffn_dwu_{tag}")
    dh = _mm(dg, wg, "nt", name=f"ffn_dh1_{tag}")
    dh = _mm(du, wu, "nt", res=dh, name=f"ffn_dh2_{tag}")
    dx, dnw = _rms_bwd(x, nw, dh, dxo, name=f"rms_bwd_{tag}")
    return dx, dnw, dwg, dwu, dwd


def _hg_fwd(proj, lbs, nw, *, name):
    bsz, l, _ = proj.shape
    nc = l // CHUNK

    def body(p_ref, lb_ref, nw_ref, y_ref, s_ref, st):
        @pl.when(pl.program_id(1) == 0)
        def _():
            st[...] = jnp.zeros_like(st)

        for b in range(bsz):
            s_ref[b] = st[b]
            p = p_ref[b]
            st_new, y = hg_chunk(st[b], p[:, 0:128], p[:, 128:256], p[:, 256:384], p[:, 384:512], lb_ref[...],
                                 nw_ref[...])
            st[b] = st_new
            y_ref[b] = y

    return pl.pallas_call(
        body,
        out_shape=[jax.ShapeDtypeStruct((bsz, l, HG_W), F32),
                   jax.ShapeDtypeStruct((bsz, HG_HEADS, nc, HEAD, HEAD), F32)],
        grid=(HG_HEADS, nc),
        in_specs=[pl.BlockSpec((bsz, CHUNK, 512), lambda h, c: (0, c, h)),
                  pl.BlockSpec((1, HEAD), lambda h, c: (0, h)),
                  pl.BlockSpec((1, HEAD), lambda h, c: (0, 0))],
        out_specs=[pl.BlockSpec((bsz, CHUNK, HEAD), lambda h, c: (0, c, h)),
                   pl.BlockSpec((bsz, None, None, HEAD, HEAD), lambda h, c: (0, h, c, 0, 0))],
        scratch_shapes=[pltpu.VMEM((bsz, HEAD, HEAD), F32)], name=name,
        compiler_params=_cparams(("parallel", "arbitrary")))(proj, lbs, nw)


def _hg_bwd(proj, lbs, nw, states, dy, *, name):
    bsz, l, _ = proj.shape
    nc = l // CHUNK

    def body(p_ref, lb_ref, nw_ref, s_ref, dy_ref, dp_ref, dlb_ref, dnw_ref, dst):
        @pl.when(pl.program_id(1) == 0)
        def _():
            dst[...] = jnp.zeros_like(dst)
            dlb_ref[...] = jnp.zeros_like(dlb_ref)
            dnw_ref[...] = jnp.zeros_like(dnw_ref)

        for b in range(bsz):
            p = p_ref[b]
            _, vjp = jax.vjp(hg_chunk, s_ref[b], p[:, 0:128], p[:, 128:256], p[:, 256:384], p[:, 384:512],
                             lb_ref[...], nw_ref[...])
            d_st, dq, df, dv, dgl, dlb, dnw = vjp((dst[b], dy_ref[b]))
            dst[b] = d_st
            dp_ref[b] = jnp.concatenate([dq, df, dv, dgl], axis=1)
            dlb_ref[b] += dlb
            dnw_ref[b] += dnw

    rev = lambda c: nc - 1 - c
    acc = pl.BlockSpec((bsz, None, 1, HEAD), lambda h, c: (0, h, 0, 0))
    return pl.pallas_call(
        body,
        out_shape=[jax.ShapeDtypeStruct((bsz, l, 4 * HG_W), F32),
                   jax.ShapeDtypeStruct((bsz, HG_HEADS, 1, HEAD), F32),
                   jax.ShapeDtypeStruct((bsz, HG_HEADS, 1, HEAD), F32)],
        grid=(HG_HEADS, nc),
        in_specs=[pl.BlockSpec((bsz, CHUNK, 512), lambda h, c: (0, rev(c), h)),
                  pl.BlockSpec((1, HEAD), lambda h, c: (0, h)),
                  pl.BlockSpec((1, HEAD), lambda h, c: (0, 0)),
                  pl.BlockSpec((bsz, None, None, HEAD, HEAD), lambda h, c: (0, h, rev(c), 0, 0)),
                  pl.BlockSpec((bsz, CHUNK, HEAD), lambda h, c: (0, rev(c), h))],
        out_specs=[pl.BlockSpec((bsz, CHUNK, 512), lambda h, c: (0, rev(c), h)), acc, acc],
        scratch_shapes=[pltpu.VMEM((bsz, HEAD, HEAD), F32)], name=name,
        compiler_params=_cparams(("parallel", "arbitrary")))(proj, lbs, nw, states, dy)


def _head_masks(h):
    lane = _iota((1, HEAD), 1)
    return jnp.where(lane == h, 1.0, 0.0).astype(F32), jnp.where(lane == GDN_HEADS + h, 1.0, 0.0).astype(F32)


def _gdn_fwd(qkvg, bd, ab, dtb, nw, *, name):
    bsz, l, _ = qkvg.shape
    nc = l // CHUNK

    def body(p_ref, bd_ref, ab_ref, dtb_ref, nw_ref, y_ref, s_ref, st):
        @pl.when(pl.program_id(1) == 0)
        def _():
            st[...] = jnp.zeros_like(st)

        mb, ma = _head_masks(pl.program_id(0))
        for b in range(bsz):
            s_ref[b] = st[b]
            p = p_ref[b]
            st_new, y = gdn_chunk(st[b], p[:, 0:128], p[:, 128:256], p[:, 256:384], p[:, 384:512], bd_ref[b],
                                  ab_ref[...], dtb_ref[...], nw_ref[...], mb, ma)
            st[b] = st_new
            y_ref[b] = y

    vec = pl.BlockSpec((1, HEAD), lambda h, c: (0, 0))
    return pl.pallas_call(
        body,
        out_shape=[jax.ShapeDtypeStruct((bsz, l, GDN_W), F32),
                   jax.ShapeDtypeStruct((bsz, GDN_HEADS, nc, HEAD, HEAD), F32)],
        grid=(GDN_HEADS, nc),
        in_specs=[pl.BlockSpec((bsz, CHUNK, 512), lambda h, c: (0, c, h)),
                  pl.BlockSpec((bsz, CHUNK, HEAD), lambda h, c: (0, c, 0)), vec, vec, vec],
        out_specs=[pl.BlockSpec((bsz, CHUNK, HEAD), lambda h, c: (0, c, h)),
                   pl.BlockSpec((bsz, None, None, HEAD, HEAD), lambda h, c: (0, h, c, 0, 0))],
        scratch_shapes=[pltpu.VMEM((bsz, HEAD, HEAD), F32)], name=name,
        compiler_params=_cparams(("parallel", "arbitrary")))(qkvg, bd, ab, dtb, nw)


def _gdn_bwd(qkvg, bd, ab, dtb, nw, states, dy, *, name):
    bsz, l, _ = qkvg.shape
    nc = l // CHUNK

    def body(p_ref, bd_ref, ab_ref, dtb_ref, nw_ref, s_ref, dy_ref, dp_ref, dbd_ref, dab_ref, ddt_ref, dnw_ref,
             dst):
        @pl.when(pl.program_id(1) == 0)
        def _():
            dst[...] = jnp.zeros_like(dst)
            dab_ref[...] = jnp.zeros_like(dab_ref)
            ddt_ref[...] = jnp.zeros_like(ddt_ref)
            dnw_ref[...] = jnp.zeros_like(dnw_ref)

        mb, ma = _head_masks(pl.program_id(0))
        fn = lambda st, q, k, v, gt, bdv, abv, dtv, nwv: gdn_chunk(st, q, k, v, gt, bdv, abv, dtv, nwv, mb, ma)
        for b in range(bsz):
            p = p_ref[b]
            _, vjp = jax.vjp(fn, s_ref[b], p[:, 0:128], p[:, 128:256], p[:, 256:384], p[:, 384:512], bd_ref[b],
                             ab_ref[...], dtb_ref[...], nw_ref[...])
            d_st, dq, dk, dv, dgt, dbd, dab, ddt, dnw = vjp((dst[b], dy_ref[b]))
            dst[b] = d_st
            dp_ref[b] = jnp.concatenate([dq, dk, dv, dgt], axis=1)
            dbd_ref[b] = dbd
            dab_ref[b] += dab
            ddt_ref[b] += ddt
            dnw_ref[b] += dnw

    rev = lambda c: nc - 1 - c
    vec = pl.BlockSpec((1, HEAD), lambda h, c: (0, 0))
    acc = pl.BlockSpec((bsz, None, 1, HEAD), lambda h, c: (0, h, 0, 0))
    acc_shape = jax.ShapeDtypeStruct((bsz, GDN_HEADS, 1, HEAD), F32)
    return pl.pallas_call(
        body,
        out_shape=[jax.ShapeDtypeStruct((bsz, l, 4 * GDN_W), F32),
                   jax.ShapeDtypeStruct((bsz, GDN_HEADS, l, HEAD), F32), acc_shape, acc_shape, acc_shape],
        grid=(GDN_HEADS, nc),
        in_specs=[pl.BlockSpec((bsz, CHUNK, 512), lambda h, c: (0, rev(c), h)),
                  pl.BlockSpec((bsz, CHUNK, HEAD), lambda h, c: (0, rev(c), 0)), vec, vec, vec,
                  pl.BlockSpec((bsz, None, None, HEAD, HEAD), lambda h, c: (0, h, rev(c), 0, 0)),
                  pl.BlockSpec((bsz, CHUNK, HEAD), lambda h, c: (0, rev(c), h))],
        out_specs=[pl.BlockSpec((bsz, CHUNK, 512), lambda h, c: (0, rev(c), h)),
                   pl.BlockSpec((bsz, None, CHUNK, HEAD), lambda h, c: (0, h, rev(c), 0)), acc, acc, acc],
        scratch_shapes=[pltpu.VMEM((bsz, HEAD, HEAD), F32)], name=name,
        compiler_params=_cparams(("parallel", "arbitrary")))(qkvg, bd, ab, dtb, nw, states, dy)


CONV_TL = 256
QKV_LANES = 3 * HEAD


def _conv_fwd(x, w, *, name):
    bsz, l, wd = x.shape
    tl = min(CONV_TL, l)
    nr = l // tl
    hb = tl // SUBLANE

    def body(x_ref, xp_ref, w_ref, o_ref):
        xv = x_ref[...]
        halo = jnp.where(pl.program_id(2) > 0, xp_ref[...], 0.0)
        xc = jnp.concatenate([halo, xv], axis=0)
        wv = w_ref[...]
        z = wv[CONV_W - 1:CONV_W] * xv
        for j in range(CONV_W - 1):
            z = z + wv[j:j + 1] * pltpu.roll(xc, CONV_W - 1 - j, 0)[SUBLANE:]
        o_ref[...] = jnp.where(_iota((tl, 512), 1) < QKV_LANES, _silu(z), xv)

    return pl.pallas_call(
        body, out_shape=jax.ShapeDtypeStruct(x.shape, F32), grid=(bsz, wd // 512, nr),
        in_specs=[pl.BlockSpec((None, tl, 512), lambda b, h, r: (b, r, h)),
                  pl.BlockSpec((None, SUBLANE, 512), lambda b, h, r: (b, jnp.maximum(r * hb - 1, 0), h)),
                  pl.BlockSpec((SUBLANE, 512), lambda b, h, r: (0, h))],
        out_specs=pl.BlockSpec((None, tl, 512), lambda b, h, r: (b, r, h)), name=name,
        compiler_params=_cparams(("parallel", "parallel", "arbitrary")))(x, x, w)


def _conv_bwd(x, w, dy, *, name):
    bsz, l, wd = x.shape
    tl = min(CONV_TL, l)
    nr = l // tl
    hb = tl // SUBLANE

    def body(x_ref, xp_ref, xn_ref, w_ref, dy_ref, dyn_ref, dx_ref, dw_ref):
        r = pl.program_id(2)
        xv = x_ref[...]
        prev = jnp.where(r > 0, xp_ref[...], 0.0)
        last = r == nr - 1
        nxt = jnp.where(last, 0.0, xn_ref[...])
        dyn = jnp.where(last, 0.0, dyn_ref[...])
        xc = jnp.concatenate([prev, xv, nxt], axis=0)
        wv = w_ref[...]
        shifted = [pltpu.roll(xc, CONV_W - 1 - j, 0) for j in range(CONV_W - 1)] + [xc]
        z = wv[0:1] * shifted[0]
        for j in range(1, CONV_W):
            z = z + wv[j:j + 1] * shifted[j]
        z = z[SUBLANE:]
        sg = _sigmoid(z)
        dyc = jnp.concatenate([dy_ref[...], dyn], axis=0)
        dz = dyc * (sg * (1.0 + z * (1.0 - sg)))
        dx = wv[CONV_W - 1:CONV_W] * dz[:tl]
        for j in range(CONV_W - 1):
            s = CONV_W - 1 - j
            dx = dx + wv[j:j + 1] * pltpu.roll(dz, tl + SUBLANE - s, 0)[:tl]
        conv_lane = _iota((tl, 512), 1) < QKV_LANES
        dx_ref[...] = jnp.where(conv_lane, dx, dy_ref[...])

        @pl.when(r == 0)
        def _():
            dw_ref[...] = jnp.zeros_like(dw_ref)

        dzt = dz[:tl]
        rows = [jnp.sum(dzt * shifted[j][SUBLANE:SUBLANE + tl], axis=0, keepdims=True) for j in range(CONV_W)]
        rows.append(jnp.zeros((SUBLANE - CONV_W, 512), F32))
        dw_ref[...] += jnp.concatenate(rows, axis=0)

    blk = pl.BlockSpec((None, tl, 512), lambda b, h, r: (b, r, h))
    prev = pl.BlockSpec((None, SUBLANE, 512), lambda b, h, r: (b, jnp.maximum(r * hb - 1, 0), h))
    nxt = pl.BlockSpec((None, SUBLANE, 512), lambda b, h, r: (b, jnp.minimum((r + 1) * hb, l // SUBLANE - 1), h))
    return pl.pallas_call(
        body, out_shape=[jax.ShapeDtypeStruct(x.shape, F32), jax.ShapeDtypeStruct((bsz, SUBLANE, wd), F32)],
        grid=(bsz, wd // 512, nr),
        in_specs=[blk, prev, nxt, pl.BlockSpec((SUBLANE, 512), lambda b, h, r: (0, h)), blk, nxt],
        out_specs=[blk, pl.BlockSpec((None, SUBLANE, 512), lambda b, h, r: (b, 0, h))], name=name,
        compiler_params=_cparams(("parallel", "parallel", "arbitrary")))(x, x, x, w, dy, dy)


S5_LC = 256
S5_TN = 512


def _s5_scan(xr, xi, pw, *, reverse, name):
    bsz, l, n = xr.shape
    lc = min(S5_LC, l)
    nc = l // lc
    steps = int(math.log2(lc))

    def shift(v, s):
        if reverse:
            if s >= SUBLANE:
                return jnp.concatenate([v[s:], jnp.zeros((s, v.shape[1]), F32)], axis=0)
            return jnp.where(_iota(v.shape, 0) < lc - s, pltpu.roll(v, lc - s, 0), 0.0)
        if s >= SUBLANE:
            return jnp.concatenate([jnp.zeros((s, v.shape[1]), F32), v[:lc - s]], axis=0)
        return jnp.where(_iota(v.shape, 0) >= s, pltpu.roll(v, s, 0), 0.0)

    def body(xr_ref, xi_ref, pw_ref, hr_ref, hi_ref, cr, ci):
        @pl.when(pl.program_id(2) == 0)
        def _():
            cr[...] = jnp.zeros_like(cr)
            ci[...] = jnp.zeros_like(ci)

        vr, vi = xr_ref[...], xi_ref[...]
        for k in range(steps):
            s = 1 << k
            ar, ai = pw_ref[0, s - 1:s, :], pw_ref[1, s - 1:s, :]
            sr, si = shift(vr, s), shift(vi, s)
            vr, vi = vr + ar * sr - ai * si, vi + ar * si + ai * sr
        pr, pi = pw_ref[2], pw_ref[3]
        c_r, c_i = cr[...], ci[...]
        vr, vi = vr + pr * c_r - pi * c_i, vi + pr * c_i + pi * c_r
        hr_ref[...] = vr
        hi_ref[...] = vi
        edge = 0 if reverse else lc - 1
        cr[...] = vr[edge:edge + 1]
        ci[...] = vi[edge:edge + 1]

    tmap = (lambda c: nc - 1 - c) if reverse else (lambda c: c)
    blk = pl.BlockSpec((None, lc, S5_TN), lambda b, j, c: (b, tmap(c), j))
    return pl.pallas_call(
        body, out_shape=[jax.ShapeDtypeStruct(xr.shape, F32), jax.ShapeDtypeStruct(xr.shape, F32)],
        grid=(bsz, n // S5_TN, nc),
        in_specs=[blk, blk, pl.BlockSpec((4, lc, S5_TN), lambda b, j, c: (0, 0, j))], out_specs=[blk, blk],
        scratch_shapes=[pltpu.VMEM((1, S5_TN), F32), pltpu.VMEM((1, S5_TN), F32)], name=name,
        compiler_params=_cparams(("parallel", "parallel", "arbitrary")))(xr, xi, pw)


def _s5_dabar(gr, gi, hr, hi, *, name):
    bsz, l, n = gr.shape
    lc = min(S5_LC, l)
    nc = l // lc
    hb = lc // SUBLANE

    def body(gr_ref, gi_ref, hr_ref, hi_ref, hrp_ref, hip_ref, o_ref):
        c = pl.program_id(2)

        @pl.when(c == 0)
        def _():
            o_ref[...] = jnp.zeros_like(o_ref)

        def prev(h_ref, hp_ref):
            first = jnp.where(c > 0, hp_ref[SUBLANE - 1:SUBLANE, :], 0.0)
            return jnp.where(_iota((lc, S5_TN), 0) == 0, first, pltpu.roll(h_ref[...], 1, 0))

        pr, pi = prev(hr_ref, hrp_ref), prev(hi_ref, hip_ref)
        g_r, g_i = gr_ref[...], gi_ref[...]
        d_re = jnp.sum(g_r * pr + g_i * pi, axis=0, keepdims=True)
        d_im = jnp.sum(g_i * pr - g_r * pi, axis=0, keepdims=True)
        o_ref[...] += jnp.concatenate([d_re, d_im, jnp.zeros((SUBLANE - 2, S5_TN), F32)], axis=0)

    blk = pl.BlockSpec((None, lc, S5_TN), lambda b, j, c: (b, c, j))
    prv = pl.BlockSpec((None, SUBLANE, S5_TN), lambda b, j, c: (b, jnp.maximum(c * hb - 1, 0), j))
    return pl.pallas_call(
        body, out_shape=jax.ShapeDtypeStruct((bsz, SUBLANE, n), F32), grid=(bsz, n // S5_TN, nc),
        in_specs=[blk, blk, blk, blk, prv, prv],
        out_specs=pl.BlockSpec((None, SUBLANE, S5_TN), lambda b, j, c: (b, 0, j)), name=name,
        compiler_params=_cparams(("parallel", "parallel", "arbitrary")))(gr, gi, hr, hi, hr, hi)


def _s5_post_fwd(ych, u, d, wglu, *, name):
    t, w = ych.shape
    tm = _pick(t, (512, 256, 128))

    def body(y_ref, u_ref, d_ref, w_ref, o_ref):
        o_ref[...] = s5_post(y_ref[...], u_ref[...], d_ref[...], w_ref[...])

    row = pl.BlockSpec((tm, w), lambda i: (i, 0))
    return pl.pallas_call(
        body, out_shape=jax.ShapeDtypeStruct((t, w), F32), grid=(t // tm,),
        in_specs=[row, row, pl.BlockSpec((1, w), lambda i: (0, 0)), pl.BlockSpec((w, w), lambda i: (0, 0))],
        out_specs=row, name=name, compiler_params=_cparams(("parallel",)))(ych, u, d, wglu)


def _s5_post_bwd(ych, u, d, wglu, dout, *, name):
    t, w = ych.shape
    tm = _pick(t, (512, 256, 128))

    def body(y_ref, u_ref, d_ref, w_ref, do_ref, dy_ref, du_ref, dd_ref, dw_ref):
        _, vjp = jax.vjp(s5_post, y_ref[...], u_ref[...], d_ref[...], w_ref[...])
        dy, du, dd, dw = vjp(do_ref[...])
        dy_ref[...] = dy
        du_ref[...] = du

        @pl.when(pl.program_id(0) == 0)
        def _():
            dd_ref[...] = jnp.zeros_like(dd_ref)
            dw_ref[...] = jnp.zeros_like(dw_ref)

        dd_ref[...] += dd
        dw_ref[...] += dw

    row = pl.BlockSpec((tm, w), lambda i: (i, 0))
    vec = pl.BlockSpec((1, w), lambda i: (0, 0))
    mat = pl.BlockSpec((w, w), lambda i: (0, 0))
    return pl.pallas_call(
        body, out_shape=[jax.ShapeDtypeStruct((t, w), F32), jax.ShapeDtypeStruct((t, w), F32),
                         jax.ShapeDtypeStruct((1, w), F32), jax.ShapeDtypeStruct((w, w), F32)], grid=(t // tm,),
        in_specs=[row, row, vec, mat, row], out_specs=[row, row, vec, mat], name=name,
        compiler_params=_cparams(("arbitrary",)))(ych, u, d, wglu, dout)


def _s5_disc(a_re, a_im, b_re, b_im, log_dt):
    dt = jnp.exp(log_dt)[:, None]
    mag = jnp.exp(dt * a_re)
    ang = dt * a_im
    abar_re = mag * jnp.cos(ang)
    abar_im = mag * jnp.sin(ang)
    den = a_re * a_re + a_im * a_im
    zr = abar_re - 1.0
    zi = abar_im
    coef_re = ((zr * a_re + zi * a_im) / den)[..., None]
    coef_im = ((zi * a_re - zr * a_im) / den)[..., None]
    return abar_re, abar_im, coef_re * b_re - coef_im * b_im, coef_re * b_im + coef_im * b_re


def _s5_powers(a_re, a_im, log_dt, lc):
    dt = jnp.exp(log_dt)[:, None]
    n = jnp.arange(1, lc + 1, dtype=F32)[:, None, None]
    mag = jnp.exp(n * (dt * a_re)[None])
    ang = n * (dt * a_im)[None]
    pr = (mag * jnp.cos(ang)).reshape(lc, S5_N)
    pi = (mag * jnp.sin(ang)).reshape(lc, S5_N)
    fwd = jnp.stack([pr, pi, pr, pi])
    bwd = jnp.stack([pr, -pi, pr[::-1], -pi[::-1]])
    return fwd, bwd


def _block_diag_in(bb):
    eye = jnp.eye(S5_GROUPS, dtype=bb.dtype)
    return jnp.einsum("gnp,gh->gphn", bb, eye).reshape(S5_W, S5_N)


def _block_diag_out(cc):
    eye = jnp.eye(S5_GROUPS, dtype=cc.dtype)
    return jnp.einsum("gpn,gh->gnhp", cc, eye).reshape(S5_N, S5_W)


def _adamw(w, g, m, v, *, name):
    shape = w.shape
    cols = shape[-1] if w.ndim > 1 else shape[0]
    rows = w.size // cols
    tr = _pick(rows, (512, 352, 256, 128, 64, 32, 16, 8)) if rows % SUBLANE == 0 else rows
    flat = lambda t: t.reshape(rows, cols)

    def body(w_ref, g_ref, m_ref, v_ref, d_ref, nm_ref, nv_ref):
        gv = g_ref[...]
        mn = ADAM_B1 * m_ref[...] + (1.0 - ADAM_B1) * gv
        vn = ADAM_B2 * v_ref[...] + (1.0 - ADAM_B2) * jnp.square(gv)
        m_hat = mn / (1.0 - ADAM_B1 ** ADAM_STEP)
        v_hat = vn / (1.0 - ADAM_B2 ** ADAM_STEP)
        d_ref[...] = -ADAM_LR * (m_hat / (jnp.sqrt(v_hat) + ADAM_EPS) + ADAM_WD * w_ref[...])
        nm_ref[...] = mn
        nv_ref[...] = vn

    blk = pl.BlockSpec((tr, cols), lambda i: (i, 0))
    sds = jax.ShapeDtypeStruct((rows, cols), F32)
    outs = pl.pallas_call(body, out_shape=[sds, sds, sds], grid=(rows // tr,), in_specs=[blk] * 4,
                          out_specs=[blk] * 3, name=name,
                          compiler_params=_cparams(("parallel",)))(flat(w), flat(g), flat(m), flat(v))
    return tuple(o.reshape(shape) for o in outs)


ANY = pl.BlockSpec(memory_space=pl.ANY)


def _coords():
    x, y, c = lax.axis_index("x"), lax.axis_index("y"), lax.axis_index("c")
    chips = [(1 - x, y), (x, 1 - y), (1 - x, 1 - y)]
    return x, y, c, chips


def _all_gather_chips(packs, *, name):
    n = len(packs)

    def body(*refs):
        p_refs, o_refs, (send, recv) = refs[:n], refs[n:2 * n], refs[2 * n:]
        x, y, c, chips = _coords()
        sibling = (x, y, 1 - c)

        def copy(i, k, src, dst, to):
            return pltpu.make_async_remote_copy(src_ref=src, dst_ref=dst, send_sem=send.at[6 * i + k],
                                                recv_sem=recv.at[6 * i + k], device_id=to, device_id_type=MESH_ID)

        def part(i, chip, half):
            return o_refs[i].at[2 * chip[0] + chip[1], half]

        first = [copy(i, k, p_refs[i].at[c], part(i, (x, y), c), (*chip, c))
                 for i in range(n) for k, chip in enumerate(chips)]
        for cp in first:
            cp.start()
        passed = []
        for i in range(n):
            for k, chip in enumerate(chips):
                copy(i, k, part(i, chip, c), part(i, chip, c), (x, y, c)).wait_recv()
                passed.append(copy(i, 3 + k, part(i, chip, c), part(i, chip, c), sibling))
                passed[-1].start()
        for i in range(n):
            for k, chip in enumerate(chips):
                copy(i, 3 + k, part(i, chip, 1 - c), part(i, chip, 1 - c), (x, y, c)).wait_recv()
        for cp in first + passed:
            cp.wait_send()

    return pl.pallas_call(
        body, out_shape=[jax.ShapeDtypeStruct((N_CHIPS,) + p.shape, p.dtype) for p in packs], in_specs=[ANY] * n,
        out_specs=[ANY] * n,
        scratch_shapes=[pltpu.SemaphoreType.DMA((6 * n,)), pltpu.SemaphoreType.DMA((6 * n,))], name=name)(*packs)


def _swap_halves(gs, *, name):
    n = len(gs)

    def body(*refs):
        g_refs, o_refs, (send, recv) = refs[:n], refs[n:2 * n], refs[2 * n:]
        x, y, c, _ = _coords()
        cps = [pltpu.make_async_remote_copy(src_ref=g_refs[i].at[:, 1 - c], dst_ref=o_refs[i], send_sem=send.at[i],
                                            recv_sem=recv.at[i], device_id=(x, y, 1 - c), device_id_type=MESH_ID)
               for i in range(n)]
        for cp in cps:
            cp.start()
        for cp in cps:
            cp.wait()

    return pl.pallas_call(
        body, out_shape=[jax.ShapeDtypeStruct(g.shape[:1] + g.shape[2:], g.dtype) for g in gs], in_specs=[ANY] * n,
        out_specs=[ANY] * n, scratch_shapes=[pltpu.SemaphoreType.DMA((n,)), pltpu.SemaphoreType.DMA((n,))],
        name=name)(*gs)


def _add_sibling(g, got, *, name):
    n, _, m, r, w = g.shape

    def body(g0_ref, g1_ref, r_ref, a_ref, own_ref):
        c = lax.axis_index("c")
        mine = 2 * lax.axis_index("x") + lax.axis_index("y")
        j = pl.program_id(1)
        s = jnp.where(c == 0, g0_ref[...], g1_ref[...]) + r_ref[...]
        a_ref[...] = s.astype(BF16)

        @pl.when(j == 0)
        def _():
            own_ref[...] = jnp.zeros_like(own_ref)

        own_ref[...] = jnp.where(j == mine, s, own_ref[...])

    return pl.pallas_call(
        body, out_shape=[jax.ShapeDtypeStruct((n, m, r, w), BF16), jax.ShapeDtypeStruct((m, r, w), F32)],
        grid=(m, n),
        in_specs=[pl.BlockSpec((None, None, None, r, w), lambda e, j: (j, 0, e, 0, 0)),
                  pl.BlockSpec((None, None, None, r, w), lambda e, j: (j, 1, e, 0, 0)),
                  pl.BlockSpec((None, None, r, w), lambda e, j: (j, e, 0, 0))],
        out_specs=[pl.BlockSpec((None, None, r, w), lambda e, j: (j, e, 0, 0)),
                   pl.BlockSpec((None, r, w), lambda e, j: (e, 0, 0))],
        name=name, compiler_params=_cparams(("parallel", "arbitrary")))(g, g, got)


def _exchange_chips(a16s, *, name):
    n = len(a16s)

    def body(*refs):
        a_refs, o_refs, (send, recv) = refs[:n], refs[n:2 * n], refs[2 * n:]
        x, y, c, chips = _coords()
        cps = [pltpu.make_async_remote_copy(src_ref=a_refs[i].at[2 * chip[0] + chip[1]], dst_ref=o_refs[i].at[k],
                                            send_sem=send.at[3 * i + k], recv_sem=recv.at[3 * i + k],
                                            device_id=(*chip, c), device_id_type=MESH_ID)
               for i in range(n) for k, chip in enumerate(chips)]
        for cp in cps:
            cp.start()
        for cp in cps:
            cp.wait()

    return pl.pallas_call(
        body, out_shape=[jax.ShapeDtypeStruct((N_CHIPS - 1,) + a.shape[1:], a.dtype) for a in a16s],
        in_specs=[ANY] * n, out_specs=[ANY] * n,
        scratch_shapes=[pltpu.SemaphoreType.DMA((3 * n,)), pltpu.SemaphoreType.DMA((3 * n,))], name=name)(*a16s)


def _add_chips(own, got, *, name):
    m, r, w = own.shape

    def body(o_ref, g_ref, s_ref):
        s = o_ref[...]
        for k in range(N_CHIPS - 1):
            s = s + g_ref[k].astype(F32)
        s_ref[...] = s

    return pl.pallas_call(
        body, out_shape=jax.ShapeDtypeStruct((m, r, w), F32), grid=(m,),
        in_specs=[pl.BlockSpec((None, r, w), lambda i: (i, 0, 0)),
                  pl.BlockSpec((N_CHIPS - 1, None, r, w), lambda i: (0, i, 0, 0))],
        out_specs=pl.BlockSpec((None, r, w), lambda i: (i, 0, 0)), name=name,
        compiler_params=_cparams(("parallel",)))(own, got)


def _swap_reduced(halves, *, name):
    n = len(halves)

    def body(*refs):
        h_refs, o_refs, (send, recv) = refs[:n], refs[n:2 * n], refs[2 * n:]
        x, y, c, _ = _coords()
        cps = [pltpu.make_async_remote_copy(src_ref=h_refs[i], dst_ref=o_refs[i], send_sem=send.at[i],
                                            recv_sem=recv.at[i], device_id=(x, y, 1 - c), device_id_type=MESH_ID)
               for i in range(n)]
        for cp in cps:
            cp.start()
        for cp in cps:
            cp.wait()

    return pl.pallas_call(
        body, out_shape=[jax.ShapeDtypeStruct(h.shape, h.dtype) for h in halves], in_specs=[ANY] * n,
        out_specs=[ANY] * n, scratch_shapes=[pltpu.SemaphoreType.DMA((n,)), pltpu.SemaphoreType.DMA((n,))],
        name=name)(*halves)


def _gather_all(vec, *, name):
    s, w = vec.shape

    def body(v_ref, o_ref, send, recv, local):
        x, y, c, _ = _coords()
        me = 4 * x + 2 * y + c
        mine = pltpu.make_async_copy(v_ref, o_ref.at[me], local)
        mine.start()
        cps = []
        for k in range(1, N_DEV):
            px = 1 - x if k & 4 else x
            py = 1 - y if k & 2 else y
            pc = 1 - c if k & 1 else c
            peer = 4 * px + 2 * py + pc
            cps.append((pltpu.make_async_remote_copy(src_ref=v_ref, dst_ref=o_ref.at[me], send_sem=send.at[k - 1],
                                                     recv_sem=recv.at[k - 1], device_id=(px, py, pc),
                                                     device_id_type=MESH_ID),
                        pltpu.make_async_remote_copy(src_ref=v_ref, dst_ref=o_ref.at[peer], send_sem=send.at[k - 1],
                                                     recv_sem=recv.at[k - 1], device_id=(px, py, pc),
                                                     device_id_type=MESH_ID)))
        for snd, _ in cps:
            snd.start()
        for snd, rcv in cps:
            rcv.wait_recv()
            snd.wait_send()
        mine.wait()

    return pl.pallas_call(
        body, out_shape=jax.ShapeDtypeStruct((N_DEV, s, w), vec.dtype), in_specs=[ANY], out_specs=ANY,
        scratch_shapes=[pltpu.SemaphoreType.DMA((N_DEV - 1,)), pltpu.SemaphoreType.DMA((N_DEV - 1,)),
                        pltpu.SemaphoreType.DMA], name=name)(vec)


def _sum_all(parts, *, name):
    n, s, w = parts.shape
    tr = _pick(s, (96, 72, 48, 32, 24, 16, 8))

    def body(p_ref, o_ref):
        acc = p_ref[0]
        for k in range(1, n):
            acc = acc + p_ref[k]
        o_ref[...] = acc

    return pl.pallas_call(
        body, out_shape=jax.ShapeDtypeStruct((s, w), F32), grid=(s // tr,),
        in_specs=[pl.BlockSpec((n, tr, w), lambda i: (0, i, 0))], out_specs=pl.BlockSpec((tr, w), lambda i: (i, 0)),
        name=name, compiler_params=_cparams(("parallel",)))(parts)


def _rows_pack(pieces, lead):
    rows = [p.reshape(p.shape[:lead] + (-1, PACK_COLS)) for p in pieces]
    spare = 2 * ROW_BLOCKS * ROW_BLOCK - sum(r.shape[lead] for r in rows)
    rows.append(jnp.zeros(rows[0].shape[:lead] + (spare, PACK_COLS), rows[0].dtype))
    return jnp.concatenate(rows, axis=lead).reshape(rows[0].shape[:lead] + (2, ROW_BLOCKS, ROW_BLOCK, PACK_COLS))


def _rows_unpack(packed, shapes, lead):
    flat = packed.reshape(packed.shape[:lead] + (-1, PACK_COLS))
    out, off = [], 0
    for shp in shapes:
        rows = int(np.prod(shp)) // PACK_COLS
        out.append(lax.slice_in_dim(flat, off, off + rows, axis=lead).reshape(packed.shape[:lead] + tuple(shp)))
        off += rows
    return out


def _unshard(piece, axis):
    return jnp.concatenate([piece[j] for j in range(N_CHIPS)], axis=axis)


def _to_shards(full, axis):
    return jnp.stack(jnp.split(full, N_CHIPS, axis=axis))


def _by_head(w, parts, heads):
    r = w.shape[0]
    return w.reshape(r, parts, heads, HEAD).transpose(0, 2, 1, 3).reshape(r, parts * heads * HEAD)


def _by_part(w, parts, heads):
    r = w.shape[0]
    return w.reshape(r, heads, parts, HEAD).transpose(0, 2, 1, 3).reshape(r, parts * heads * HEAD)


def kernel(x, ffn1_norm, ffn1_w_gate, ffn1_w_up, ffn1_w_down, mix_norm, ffn2_norm, ffn2_w_gate, ffn2_w_up, ffn2_w_down, ev_w_in, hg_lb_logits, hg_norm_w, s5_a_re, s5_a_im, s5_b_re, s5_b_im, s5_c_re, s5_c_im, s5_d, s5_log_dt, s5_w_glu, ev_w_out, od_w_in, gdn_conv_w, gdn_a_log, gdn_dt_bias, gdn_norm_w, od_w_out, final_norm, loss_target, m_ffn1_norm, m_ffn1_w_gate, m_ffn1_w_up, m_ffn1_w_down, m_mix_norm, m_ffn2_norm, m_ffn2_w_gate, m_ffn2_w_up, m_ffn2_w_down, m_ev_w_in, m_hg_lb_logits, m_hg_norm_w, m_s5_a_re, m_s5_a_im, m_s5_b_re, m_s5_b_im, m_s5_c_re, m_s5_c_im, m_s5_d, m_s5_log_dt, m_s5_w_glu, m_ev_w_out, m_od_w_in, m_gdn_conv_w, m_gdn_a_log, m_gdn_dt_bias, m_gdn_norm_w, m_od_w_out, m_final_norm, v_ffn1_norm, v_ffn1_w_gate, v_ffn1_w_up, v_ffn1_w_down, v_mix_norm, v_ffn2_norm, v_ffn2_w_gate, v_ffn2_w_up, v_ffn2_w_down, v_ev_w_in, v_hg_lb_logits, v_hg_norm_w, v_s5_a_re, v_s5_a_im, v_s5_b_re, v_s5_b_im, v_s5_c_re, v_s5_c_im, v_s5_d, v_s5_log_dt, v_s5_w_glu, v_ev_w_out, v_od_w_in, v_gdn_conv_w, v_gdn_a_log, v_gdn_dt_bias, v_gdn_norm_w, v_od_w_out, v_final_norm):
    args = locals()
    wts = {n: args[n] for n in WEIGHTS}
    mom = {n: args["m_" + n] for n in WEIGHTS}
    var = {n: args["v_" + n] for n in WEIGHTS}
    bsz, seq, _ = x.shape
    t = bsz * seq

    def wire(n):
        if n == "gdn_conv_w":
            return lax.bitcast_convert_type(wts[n], BF16)
        return wts[n].astype(BF16)

    chip = 2 * lax.axis_index("x") + lax.axis_index("y")
    is_own = (jnp.arange(N_CHIPS) == chip).reshape(N_CHIPS, 1, 1, 1, 1)
    n_wide = wts[WIDE[0]].shape[0]
    pack_a = jnp.concatenate([wts[n].astype(BF16) for n in WIDE], axis=0)
    pack_a = pack_a.reshape((2, len(WIDE) * n_wide // 2) + pack_a.shape[1:])
    pack_b = _rows_pack([wire(n) for n, _ in ROWS], 0)
    got_a, got_b = _all_gather_chips([pack_a, pack_b], name="gather_weights")
    got_a = jnp.where(is_own, pack_a[None], got_a).reshape((N_CHIPS, len(WIDE) * n_wide) + pack_a.shape[2:])
    got_b = jnp.where(is_own, pack_b[None], got_b)
    full = {}
    for k, n in enumerate(WIDE):
        full[n] = jnp.concatenate([got_a[j, k * n_wide:(k + 1) * n_wide] for j in range(N_CHIPS)], axis=2)
    for (n, axis), piece in zip(ROWS, _rows_unpack(got_b, [wire(n).shape for n, _ in ROWS], 1)):
        if n == "gdn_conv_w":
            piece = lax.bitcast_convert_type(piece, F32)
        full[n] = _unshard(piece, axis)

    ev_cols = lambda w: jnp.concatenate([_by_head(w[:, :4 * HG_W], 4, HG_HEADS), w[:, 4 * HG_W:]], axis=1)
    ev_cols_back = lambda w: jnp.concatenate([_by_part(w[:, :4 * HG_W], 4, HG_HEADS), w[:, 4 * HG_W:]], axis=1)

    lbs_fn = lambda lg: (lambda p: jnp.cumsum(p, axis=0) - p[0])(jax.nn.softmax(lg, axis=0))
    lbs, lbs_vjp = jax.vjp(lbs_fn, hg_lb_logits)
    lc = min(S5_LC, seq)

    xs = x.reshape(t, D_MODEL)
    saved = []
    for layer in range(DEPTH):
        j = layer // 2
        rec = {}
        xs, rec["ffn1"] = _ffn_fwd(xs, ffn1_norm[layer][None], full["ffn1_w_gate"][layer], full["ffn1_w_up"][layer],
                                   full["ffn1_w_down"][layer], f"a{layer}")
        rec["x_mix"] = xs
        h = _rms_fwd(xs, mix_norm[layer][None], name=f"rms_fwd_mix{layer}")
        rec["h"] = h
        if layer % 2 == 0:
            w_in = ev_cols(full["ev_w_in"][j])
            proj = _mm(h, w_in, "nn", name=f"ev_proj{layer}").reshape(bsz, seq, -1)
            rec["proj"] = proj
            y_a, rec["hg_states"] = _hg_fwd(proj, lbs[j][None], hg_norm_w[j][None], name=f"hgrn2_fwd{layer}")
            disc, rec["disc_vjp"] = jax.vjp(_s5_disc, s5_a_re[j], s5_a_im[j], s5_b_re[j], s5_b_im[j], s5_log_dt[j])
            wb_re, wb_im = _block_diag_in(disc[2]).astype(BF16), _block_diag_in(disc[3]).astype(BF16)
            wc_re = _block_diag_out(s5_c_re[j]).astype(BF16)
            wc_im = _block_diag_out(-s5_c_im[j]).astype(BF16)
            pw_f, pw_b = _s5_powers(s5_a_re[j], s5_a_im[j], s5_log_dt[j], lc)
            u = proj[:, :, 4 * HG_W:].reshape(t, S5_W)
            bu_re = _mm(u, wb_re, "nn", name=f"s5_bu_re{layer}").reshape(bsz, seq, S5_N)
            bu_im = _mm(u, wb_im, "nn", name=f"s5_bu_im{layer}").reshape(bsz, seq, S5_N)
            h_re, h_im = _s5_scan(bu_re, bu_im, pw_f, reverse=False, name=f"s5_scan_fwd{layer}")
            h_re2, h_im2 = h_re.reshape(t, S5_N), h_im.reshape(t, S5_N)
            ych = _mm(h_re2, wc_re, "nn", name=f"s5_out_re{layer}")
            ych = _mm(h_im2, wc_im, "nn", res=ych, name=f"s5_out_im{layer}")
            w_glu = full["s5_w_glu"][j]
            y_b = _s5_post_fwd(ych, u, s5_d[j][None], w_glu, name=f"s5_post_fwd{layer}")
            rec.update(u=u, h_re=h_re, h_im=h_im, ych=ych, wb=(wb_re, wb_im), wc=(wc_re, wc_im), pw_b=pw_b,
                       y_a=y_a.reshape(t, HG_W), y_b=y_b)
            w_out = full["ev_w_out"][j]
            xs = _mm(rec["y_a"], w_out[:HG_W], "nn", res=xs, name=f"ev_out_a{layer}")
            xs = _mm(y_b, w_out[HG_W:], "nn", res=xs, name=f"ev_out_b{layer}")
        else:
            w_in = full["od_w_in"][j]
            w_main = _by_head(w_in[:, :4 * GDN_W], 4, GDN_HEADS)
            w_bd = jnp.pad(w_in[:, 4 * GDN_W:], ((0, 0), (0, HEAD - 2 * GDN_HEADS)))
            proj = _mm(h, w_main, "nn", name=f"od_proj{layer}").reshape(bsz, seq, -1)
            bd = _mm(h, w_bd, "nn", name=f"od_proj_bd{layer}").reshape(bsz, seq, HEAD)
            conv_w = full["gdn_conv_w"][j].reshape(CONV_W, 3, GDN_HEADS, HEAD).transpose(0, 2, 1, 3)
            conv_w = jnp.pad(conv_w, ((0, SUBLANE - CONV_W), (0, 0), (0, 1), (0, 0))).reshape(SUBLANE, 4 * GDN_W)
            qkvg = _conv_fwd(proj, conv_w, name=f"gdn_conv_fwd{layer}")
            ab = jnp.pad(gdn_a_log[j][None], ((0, 0), (GDN_HEADS, HEAD - 2 * GDN_HEADS)))
            dtb = jnp.pad(gdn_dt_bias[j][None], ((0, 0), (GDN_HEADS, HEAD - 2 * GDN_HEADS)))
            y, states = _gdn_fwd(qkvg, bd, ab, dtb, gdn_norm_w[j][None], name=f"gdn_fwd{layer}")
            rec.update(proj=proj, bd=bd, conv_w=conv_w, qkvg=qkvg, ab=ab, dtb=dtb, states=states,
                       y=y.reshape(t, GDN_W), w_main=w_main, w_bd=w_bd)
            xs = _mm(rec["y"], full["od_w_out"][j], "nn", res=xs, name=f"od_out{layer}")
        xs, rec["ffn2"] = _ffn_fwd(xs, ffn2_norm[layer][None], full["ffn2_w_gate"][layer], full["ffn2_w_up"][layer],
                                   full["ffn2_w_down"][layer], f"b{layer}")
        saved.append(rec)

    loss_part, dx, d_final = _final_loss(xs, final_norm[None], loss_target.reshape(t, D_MODEL), name="loss_head")
    loss = lax.psum(loss_part[0, 0], ("x", "y", "c"))

    grads = {n: [None] * wts[n].shape[0] for n in WEIGHTS if n != "final_norm"}
    grads["final_norm"] = d_final[0]
    d_lbs = [None] * 2
    for layer in reversed(range(DEPTH)):
        j = layer // 2
        rec = saved[layer]
        dx, dn, dwg, dwu, dwd = _ffn_bwd(rec["ffn2"], ffn2_norm[layer][None], full["ffn2_w_gate"][layer],
                                         full["ffn2_w_up"][layer], full["ffn2_w_down"][layer], dx, f"b{layer}")
        grads["ffn2_norm"][layer], grads["ffn2_w_gate"][layer] = dn[0], dwg
        grads["ffn2_w_up"][layer], grads["ffn2_w_down"][layer] = dwu, dwd
        h = rec["h"]
        if layer % 2 == 0:
            w_out = full["ev_w_out"][j]
            dy_a = _mm(dx, w_out[:HG_W], "nt", name=f"ev_dya{layer}")
            dy_b = _mm(dx, w_out[HG_W:], "nt", name=f"ev_dyb{layer}")
            dw_out = jnp.concatenate([_mm(rec["y_a"], dx, "tn", name=f"ev_dwo_a{layer}"),
                                      _mm(rec["y_b"], dx, "tn", name=f"ev_dwo_b{layer}")], axis=0)
            grads["ev_w_out"][j] = dw_out
            wb_re, wb_im = rec["wb"]
            wc_re, wc_im = rec["wc"]
            dych, du, dd, dwglu = _s5_post_bwd(rec["ych"], rec["u"], s5_d[j][None], full["s5_w_glu"][j], dy_b,
                                               name=f"s5_post_bwd{layer}")
            grads["s5_d"][j], grads["s5_w_glu"][j] = dd[0], dwglu
            h_re2, h_im2 = rec["h_re"].reshape(t, S5_N), rec["h_im"].reshape(t, S5_N)
            dwc_re = _mm(h_re2, dych, "tn", name=f"s5_dwc_re{layer}")
            dwc_im = _mm(h_im2, dych, "tn", name=f"s5_dwc_im{layer}")
            dh_re = _mm(dych, wc_re, "nt", name=f"s5_dh_re{layer}").reshape(bsz, seq, S5_N)
            dh_im = _mm(dych, wc_im, "nt", name=f"s5_dh_im{layer}").reshape(bsz, seq, S5_N)
            g_re, g_im = _s5_scan(dh_re, dh_im, rec["pw_b"], reverse=True, name=f"s5_scan_bwd{layer}")
            da = _s5_dabar(g_re, g_im, rec["h_re"], rec["h_im"], name=f"s5_dabar{layer}").sum(axis=0)
            g_re2, g_im2 = g_re.reshape(t, S5_N), g_im.reshape(t, S5_N)
            dwb_re = _mm(rec["u"], g_re2, "tn", name=f"s5_dwb_re{layer}")
            dwb_im = _mm(rec["u"], g_im2, "tn", name=f"s5_dwb_im{layer}")
            du = _mm(g_re2, wb_re, "nt", res=du, name=f"s5_du_re{layer}")
            du = _mm(g_im2, wb_im, "nt", res=du, name=f"s5_du_im{layer}")
            diag_in = lambda m: jnp.einsum("gpgn->gnp", m.reshape(S5_GROUPS, S5_GROUP, S5_GROUPS, S5_STATE))
            diag_out = lambda m: jnp.einsum("gngp->gpn", m.reshape(S5_GROUPS, S5_STATE, S5_GROUPS, S5_GROUP))
            d_disc = (da[0].reshape(S5_GROUPS, S5_STATE), da[1].reshape(S5_GROUPS, S5_STATE), diag_in(dwb_re),
                      diag_in(dwb_im))
            ga_re, ga_im, gb_re, gb_im, g_dt = rec["disc_vjp"](d_disc)
            grads["s5_a_re"][j], grads["s5_a_im"][j], grads["s5_b_re"][j] = ga_re, ga_im, gb_re
            grads["s5_b_im"][j], grads["s5_log_dt"][j] = gb_im, g_dt
            grads["s5_c_re"][j], grads["s5_c_im"][j] = diag_out(dwc_re), -diag_out(dwc_im)
            dproj_hg, dlb, dnw = _hg_bwd(rec["proj"], lbs[j][None], hg_norm_w[j][None], rec["hg_states"],
                                         dy_a.reshape(bsz, seq, HG_W), name=f"hgrn2_bwd{layer}")
            d_lbs[j] = dlb.sum(axis=0).reshape(HG_W)
            grads["hg_norm_w"][j] = dnw.sum(axis=(0, 1, 2))
            dproj = jnp.concatenate([dproj_hg.reshape(t, 4 * HG_W), du], axis=1)
            w_in = ev_cols(full["ev_w_in"][j])
            dw_in = _mm(h, dproj, "tn", name=f"ev_dwin{layer}")
            grads["ev_w_in"][j] = ev_cols_back(dw_in)
            dh = _mm(dproj, w_in, "nt", name=f"ev_dh{layer}")
        else:
            dy = _mm(dx, full["od_w_out"][j], "nt", name=f"od_dy{layer}")
            grads["od_w_out"][j] = _mm(rec["y"], dx, "tn", name=f"od_dwo{layer}")
            dqkvg, dbd_h, dab, ddt, dnw = _gdn_bwd(rec["qkvg"], rec["bd"], rec["ab"], rec["dtb"],
                                                   gdn_norm_w[j][None], rec["states"],
                                                   dy.reshape(bsz, seq, GDN_W), name=f"gdn_bwd{layer}")
            grads["gdn_norm_w"][j] = dnw.sum(axis=(0, 1, 2))
            grads["gdn_a_log"][j] = dab.sum(axis=(0, 1, 2))[GDN_HEADS:2 * GDN_HEADS]
            grads["gdn_dt_bias"][j] = ddt.sum(axis=(0, 1, 2))[GDN_HEADS:2 * GDN_HEADS]
            dbd = dbd_h.sum(axis=1).reshape(t, HEAD)
            dproj, dconv = _conv_bwd(rec["proj"], rec["conv_w"], dqkvg, name=f"gdn_conv_bwd{layer}")
            dconv = dconv.sum(axis=0)[:CONV_W].reshape(CONV_W, GDN_HEADS, 4, HEAD)[:, :, :3]
            grads["gdn_conv_w"][j] = dconv.transpose(0, 2, 1, 3).reshape(CONV_W, 3 * GDN_W)
            dproj = dproj.reshape(t, 4 * GDN_W)
            dw_main = _by_part(_mm(h, dproj, "tn", name=f"od_dwin{layer}"), 4, GDN_HEADS)
            dw_bd = _mm(h, dbd, "tn", name=f"od_dwbd{layer}")[:, :2 * GDN_HEADS]
            grads["od_w_in"][j] = jnp.concatenate([dw_main, dw_bd], axis=1)
            dh = _mm(dproj, rec["w_main"], "nt", name=f"od_dh{layer}")
            dh = _mm(dbd, rec["w_bd"], "nt", res=dh, name=f"od_dh_bd{layer}")
        dx, dn = _rms_bwd(rec["x_mix"], mix_norm[layer][None], dh, dx, name=f"rms_bwd_mix{layer}")
        grads["mix_norm"][layer] = dn[0]
        dx, dn, dwg, dwu, dwd = _ffn_bwd(rec["ffn1"], ffn1_norm[layer][None], full["ffn1_w_gate"][layer],
                                         full["ffn1_w_up"][layer], full["ffn1_w_down"][layer], dx, f"a{layer}")
        grads["ffn1_norm"][layer], grads["ffn1_w_gate"][layer] = dn[0], dwg
        grads["ffn1_w_up"][layer], grads["ffn1_w_down"][layer] = dwu, dwd
    grad_x = dx.reshape(x.shape)
    (grads["hg_lb_logits"],) = lbs_vjp(jnp.stack(d_lbs))
    for n in WEIGHTS:
        if isinstance(grads[n], list):
            grads[n] = jnp.stack(grads[n])

    g_a = jnp.concatenate([_to_shards(grads[n], 2) for n in WIDE], axis=1)
    g_a = g_a.reshape((N_CHIPS, 2, g_a.shape[1] // 2) + g_a.shape[2:])
    g_b = _rows_pack([_to_shards(grads[n], axis) for n, axis in ROWS], 1)
    sib_a, sib_b = _swap_halves([g_a, g_b], name="reduce_swap_halves")
    wire_a, own_a = _add_sibling(g_a, sib_a, name="reduce_add_sibling_wide")
    wire_b, own_b = _add_sibling(g_b, sib_b, name="reduce_add_sibling_rows")
    oth_a, oth_b = _exchange_chips([wire_a, wire_b], name="reduce_exchange_chips")
    half_a = _add_chips(own_a, oth_a, name="reduce_add_chips_wide")
    half_b = _add_chips(own_b, oth_b, name="reduce_add_chips_rows")
    peer_a, peer_b = _swap_reduced([half_a, half_b], name="reduce_swap_reduced")
    south = lax.axis_index("c") == 0
    both = lambda mine, peer: jnp.where(south, jnp.stack([mine, peer]), jnp.stack([peer, mine]))
    red_a = both(half_a, peer_a).reshape((len(WIDE) * n_wide,) + half_a.shape[1:])
    for k, n in enumerate(WIDE):
        grads[n] = red_a[k * n_wide:(k + 1) * n_wide]
    for (n, _), piece in zip(ROWS, _rows_unpack(both(half_b, peer_b), [wts[n].shape for n, _ in ROWS], 0)):
        grads[n] = piece

    small = jnp.concatenate([grads[n].reshape(-1) for n in SMALL])
    small = jnp.pad(small, (0, SMALL_ROWS * PACK_COLS - small.shape[0])).reshape(SMALL_ROWS, PACK_COLS)
    small = _sum_all(_gather_all(small, name="gather_small_grads"), name="sum_small_grads").reshape(-1)
    off = 0
    for n in SMALL:
        size = int(np.prod(wts[n].shape))
        grads[n] = small[off:off + size].reshape(wts[n].shape)
        off += size

    small_w = jnp.concatenate([wts[n].reshape(-1) for n in SMALL])
    small_m = jnp.concatenate([mom[n].reshape(-1) for n in SMALL])
    small_v = jnp.concatenate([var[n].reshape(-1) for n in SMALL])
    n_small = small_w.shape[0]
    padv = lambda a, fill: jnp.pad(a, (0, SMALL_ROWS * PACK_COLS - n_small), constant_values=fill).reshape(
        SMALL_ROWS, PACK_COLS)
    sd, sm, sv = _adamw(padv(small_w, 0.0), small.reshape(SMALL_ROWS, PACK_COLS), padv(small_m, 0.0),
                        padv(small_v, 1.0), name="adamw_small")
    delta, new_m, new_v = {}, {}, {}
    off = 0
    for n in SMALL:
        size = int(np.prod(wts[n].shape))
        delta[n] = sd.reshape(-1)[off:off + size].reshape(wts[n].shape)
        new_m[n] = sm.reshape(-1)[off:off + size].reshape(wts[n].shape)
        new_v[n] = sv.reshape(-1)[off:off + size].reshape(wts[n].shape)
        off += size
    for n, _ in SHARDED:
        delta[n], new_m[n], new_v[n] = _adamw(wts[n], grads[n], mom[n], var[n], name=f"adamw_{n}")

    return (loss, grad_x, *[grads[n] for n in WEIGHTS], *[delta[n] for n in WEIGHTS],
            *[new_m[n] for n in WEIGHTS], *[new_v[n] for n in WEIGHTS])
```

```python
import functools
import math

import jax
import jax.numpy as jnp
import numpy as np
from jax import lax
from jax.experimental import pallas as pl
from jax.experimental.pallas import tpu as pltpu

F32 = jnp.float32
BF16 = jnp.bfloat16
HI = lax.Precision.HIGHEST
MESH_ID = pl.DeviceIdType.MESH

D_MODEL = 1024
D_FF = 2816
DEPTH = 4
NORM_EPS = 1e-6
F_MIN = 1e-6
CHUNK = 64
HEAD = 128
HG_HEADS = 4
HG_W = 512
S5_W = 512
S5_GROUP = 16
S5_GROUPS = 32
S5_STATE = 64
S5_N = S5_GROUPS * S5_STATE
GDN_HEADS = 8
GDN_W = 1024
CONV_W = 4
N_CHIPS = 4
N_DEV = 8
LANE = 128
SUBLANE = 8
VMEM_LIMIT = 56 * 1024 * 1024
PACK_COLS = 1024
ROW_BLOCK = 512
ROW_BLOCKS = 10
SMALL_ROWS = 288

ADAM_LR = 0.001
ADAM_B1 = 0.9
ADAM_B2 = 0.999
ADAM_EPS = 1e-08
ADAM_WD = 0.01
ADAM_STEP = 10

SHARDED = (("ffn1_w_gate", 2), ("ffn1_w_up", 2), ("ffn1_w_down", 1), ("ffn2_w_gate", 2), ("ffn2_w_up", 2),
           ("ffn2_w_down", 1), ("ev_w_in", 2), ("s5_w_glu", 1), ("ev_w_out", 1), ("od_w_in", 2),
           ("gdn_conv_w", 2), ("od_w_out", 1))
WIDE = ("ffn1_w_gate", "ffn1_w_up", "ffn2_w_gate", "ffn2_w_up")
ROWS = (("ffn1_w_down", 1), ("ffn2_w_down", 1), ("ev_w_out", 1), ("od_w_out", 1), ("s5_w_glu", 1), ("ev_w_in", 2),
        ("od_w_in", 2), ("gdn_conv_w", 2))
SMALL = ("ffn1_norm", "mix_norm", "ffn2_norm", "hg_lb_logits", "hg_norm_w", "s5_a_re", "s5_a_im", "s5_b_re",
         "s5_b_im", "s5_c_re", "s5_c_im", "s5_d", "s5_log_dt", "gdn_a_log", "gdn_dt_bias", "gdn_norm_w",
         "final_norm")
WEIGHTS = ("ffn1_norm", "ffn1_w_gate", "ffn1_w_up", "ffn1_w_down", "mix_norm", "ffn2_norm", "ffn2_w_gate",
           "ffn2_w_up", "ffn2_w_down", "ev_w_in", "hg_lb_logits", "hg_norm_w", "s5_a_re", "s5_a_im", "s5_b_re",
           "s5_b_im", "s5_c_re", "s5_c_im", "s5_d", "s5_log_dt", "s5_w_glu", "ev_w_out", "od_w_in", "gdn_conv_w",
           "gdn_a_log", "gdn_dt_bias", "gdn_norm_w", "od_w_out", "final_norm")


def _cparams(sem=None):
    return pltpu.CompilerParams(dimension_semantics=sem, vmem_limit_bytes=VMEM_LIMIT)


def _dg(a, b, ca, cb, hi):
    if a.ndim == 3 or b.ndim == 3:
        n = a.shape[0] if a.ndim == 3 else b.shape[0]
        a = a if a.ndim == 3 else jnp.broadcast_to(a, (n,) + a.shape)
        b = b if b.ndim == 3 else jnp.broadcast_to(b, (n,) + b.shape)
        dims = (((ca + 1,), (cb + 1,)), ((0,), (0,)))
    else:
        dims = (((ca,), (cb,)), ((), ()))
    if hi:
        return lax.dot_general(a.astype(F32), b.astype(F32), dims, precision=HI, preferred_element_type=F32)
    return lax.dot_general(a.astype(BF16), b.astype(BF16), dims, preferred_element_type=F32)


@functools.partial(jax.custom_vjp, nondiff_argnums=(2,))
def mm_nn(a, b, hi=False):
    return _dg(a, b, 1, 0, hi)


@functools.partial(jax.custom_vjp, nondiff_argnums=(2,))
def mm_nt(a, b, hi=False):
    return _dg(a, b, 1, 1, hi)


@functools.partial(jax.custom_vjp, nondiff_argnums=(2,))
def mm_tn(a, b, hi=False):
    return _dg(a, b, 0, 0, hi)


mm_nn.defvjp(lambda a, b, hi: (_dg(a, b, 1, 0, hi), (a, b)),
             lambda hi, r, g: (mm_nt(g, r[1], hi), mm_tn(r[0], g, hi)))
mm_nt.defvjp(lambda a, b, hi: (_dg(a, b, 1, 1, hi), (a, b)),
             lambda hi, r, g: (mm_nn(g, r[1], hi), mm_tn(g, r[0], hi)))
mm_tn.defvjp(lambda a, b, hi: (_dg(a, b, 0, 0, hi), (a, b)),
             lambda hi, r, g: (mm_nt(r[1], g, hi), mm_nn(r[0], g, hi)))


def _sigmoid(x):
    return 1.0 / (1.0 + jnp.exp(-x))


def _silu(x):
    return x * _sigmoid(x)


def _softplus(x):
    return jnp.maximum(x, 0.0) + jnp.log(1.0 + jnp.exp(-jnp.abs(x)))


def _gelu(x):
    return 0.5 * x * (1.0 + jnp.tanh(math.sqrt(2.0 / math.pi) * (x + 0.044715 * (x * x * x))))


def _iota(shape, dim):
    return lax.broadcasted_iota(jnp.int32, shape, dim)


def _l2norm(t):
    return t * lax.rsqrt(jnp.sum(t * t, axis=-1, keepdims=True) + NORM_EPS)


def gated_head_norm(o, gate, nw):
    o = o * lax.rsqrt(jnp.mean(o * o, axis=-1, keepdims=True) + NORM_EPS) * nw
    return o * _silu(gate)


HG_SUB = 8
HG_MID = 3


def hg_chunk(st, ql, fl, v, gl, lb, nw):
    c = CHUNK
    n = st.shape[0]
    q = _silu(ql)
    f = lb + (1.0 - lb) * _sigmoid(fl)
    lf = jnp.log(jnp.maximum(f, F_MIN))
    k = 1.0 - f
    t_i = _iota((2 * c, c), 0)
    s_i = _iota((2 * c, c), 1)
    lim = jnp.where(t_i < c, t_i, ((t_i - c) // HG_SUB) * HG_SUB + HG_MID)
    sums = jnp.broadcast_to(jnp.where(s_i <= lim, 1.0, 0.0).astype(F32), (n, 2 * c, c))
    cum = mm_nn(sums, lf, True)
    b, rb = cum[:, :c], cum[:, c:]
    qp = q * jnp.exp(b - rb)
    row = _iota((1, c, HEAD), 1)
    blocks = []
    for i in range(c // HG_SUB):
        r_i = jnp.tile(rb[:, i * HG_SUB:(i + 1) * HG_SUB], (1, c // HG_SUB, 1))
        seen = row < (i + 1) * HG_SUB
        kp = k * jnp.exp(jnp.where(seen, r_i - b, 0.0))
        blocks.append(mm_nt(qp[:, i * HG_SUB:(i + 1) * HG_SUB], kp))
    a = jnp.concatenate(blocks, axis=1)
    a = jnp.where(_iota((1, c, c), 2) <= _iota((1, c, c), 1), a, 0.0)
    o = mm_nn(a, v) + mm_nt(q * jnp.exp(b), st)
    bl = jnp.sum(lf, axis=1, keepdims=True)
    st_new = st * jnp.exp(bl) + mm_tn(v, k * jnp.exp(bl - b))
    return st_new, gated_head_norm(o, gl, nw)


def gdn_chunk(st, qc, kc, vc, gate, bd, ab, dtb, nw, mb, ma):
    c = CHUNK
    n = st.shape[0]
    beta = _sigmoid(jnp.sum(bd * mb, axis=-1, keepdims=True))
    a_lin = jnp.sum(bd * ma, axis=-1, keepdims=True)
    a_log = jnp.sum(ab * ma, axis=-1, keepdims=True)
    dt_b = jnp.sum(dtb * ma, axis=-1, keepdims=True)
    la = -jnp.exp(a_log) * _softplus(a_lin + dt_b)
    la_b = jnp.broadcast_to(la, (n, c, HEAD))
    t_i = _iota((1, c, c), 1)
    s_i = _iota((1, c, c), 2)
    lower = s_i <= t_i
    strict = s_i < t_i
    tri = jnp.broadcast_to(jnp.where(lower, 1.0, 0.0).astype(F32), (n, c, c))
    g = mm_nn(tri, la_b, True)
    d = mm_nn(tri, jnp.where(strict, la_b[:, :, :c], 0.0), True)
    lm = jnp.where(lower, jnp.exp(jnp.where(lower, d, 0.0)), 0.0)
    q = _l2norm(qc) * (HEAD ** -0.5)
    k = _l2norm(kc)
    kb = k * beta
    vb = vc * beta
    m = jnp.where(strict, mm_nt(kb, k) * lm, 0.0)
    eye = jnp.where(s_i == t_i, 1.0, 0.0).astype(F32)
    t_inv = eye - m
    p = m
    for _ in range(int(math.log2(c)) - 1):
        p = mm_nn(p, p, True)
        t_inv = t_inv + mm_nn(t_inv, p, True)
    u = mm_nn(t_inv, vb)
    w = mm_nn(t_inv, kb * jnp.exp(g))
    attn = mm_nt(q, k) * lm
    v_new = u - mm_nt(w, st)
    o = mm_nt(q * jnp.exp(g), st) + mm_nn(attn, v_new)
    g_last = jnp.sum(la_b, axis=1, keepdims=True)
    st_new = st * jnp.exp(g_last) + mm_tn(v_new, k * jnp.exp(g_last - g))
    return st_new, gated_head_norm(o, gate, nw)


def s5_post(ych, u, d, wglu):
    y = _gelu(ych + d * u)
    return y * _sigmoid(mm_nn(y, wglu))


def _pick(n, cands):
    for c in cands:
        if n % c == 0:
            return c
    return n


def _mm(a, b, mode, *, name, out_dtype=F32, res=None, scale=1.0, tm=None, tn=None, tk=None):
    if mode == "nn":
        (m, k), (k2, n) = a.shape, b.shape
    elif mode == "nt":
        (m, k), (n, k2) = a.shape, b.shape
    else:
        (k, m), (k2, n) = a.shape, b.shape
    assert k == k2, (a.shape, b.shape, mode)
    tm = tm or _pick(m, (512, 1408, 256, 128) if mode == "tn" else (512, 704, 256, 128))
    tn = tn or _pick(n, (1024, 1408, 512, 256, 128))
    tk = tk or _pick(k, (1024, 1408, 512, 256, 128))
    nk = k // tk
    grid = (n // tn, m // tm, nk)
    if mode == "tn":
        a_spec = pl.BlockSpec((tk, tm), lambda j, i, kk: (kk, i))
    else:
        a_spec = pl.BlockSpec((tm, tk), lambda j, i, kk: (i, kk))
    if mode == "nt":
        b_spec = pl.BlockSpec((tn, tk), lambda j, i, kk: (j, kk))
    else:
        b_spec = pl.BlockSpec((tk, tn), lambda j, i, kk: (kk, j))
    o_spec = pl.BlockSpec((tm, tn), lambda j, i, kk: (i, j))
    ca = 0 if mode == "tn" else 1
    cb = 1 if mode == "nt" else 0
    has_res = res is not None

    def body(*refs):
        if has_res:
            a_ref, b_ref, r_ref, o_ref, acc = refs
        else:
            a_ref, b_ref, o_ref, acc = refs
        kk = pl.program_id(2)

        @pl.when(kk == 0)
        def _():
            acc[...] = jnp.zeros_like(acc)

        acc[...] += _dg(a_ref[...], b_ref[...], ca, cb, False)

        @pl.when(kk == nk - 1)
        def _():
            r = acc[...] * scale if scale != 1.0 else acc[...]
            if has_res:
                r = r_ref[...] + r
            o_ref[...] = r.astype(out_dtype)

    in_specs = [a_spec, b_spec] + ([o_spec] if has_res else [])
    args = (a, b) + ((res,) if has_res else ())
    return pl.pallas_call(
        body, out_shape=jax.ShapeDtypeStruct((m, n), out_dtype), grid=grid, in_specs=in_specs, out_specs=o_spec,
        scratch_shapes=[pltpu.VMEM((tm, tn), F32)], name=name,
        compiler_params=_cparams(("parallel", "parallel", "arbitrary")))(*args)


def _rms_fwd(x, w, *, name):
    t, d = x.shape
    tm = _pick(t, (512, 256, 128))

    def body(x_ref, w_ref, o_ref):
        xv = x_ref[...]
        r = lax.rsqrt(jnp.mean(xv * xv, axis=-1, keepdims=True) + NORM_EPS)
        o_ref[...] = (xv * r * w_ref[...]).astype(BF16)

    return pl.pallas_call(
        body, out_shape=jax.ShapeDtypeStruct((t, d), BF16), grid=(t // tm,),
        in_specs=[pl.BlockSpec((tm, d), lambda i: (i, 0)), pl.BlockSpec((1, d), lambda i: (0, 0))],
        out_specs=pl.BlockSpec((tm, d), lambda i: (i, 0)), name=name, compiler_params=_cparams(("parallel",)))(x, w)


def _rms_bwd(x, w, dh, dres, *, name):
    t, d = x.shape
    tm = _pick(t, (512, 256, 128))

    def body(x_ref, w_ref, dh_ref, dr_ref, dx_ref, dw_ref):
        xv = x_ref[...]
        r = lax.rsqrt(jnp.mean(xv * xv, axis=-1, keepdims=True) + NORM_EPS)
        xh = xv * r
        dhv = dh_ref[...]
        dxh = dhv * w_ref[...]
        dx_ref[...] = dr_ref[...] + r * (dxh - xh * jnp.mean(dxh * xh, axis=-1, keepdims=True))

        @pl.when(pl.program_id(0) == 0)
        def _():
            dw_ref[...] = jnp.zeros_like(dw_ref)

        dw_ref[...] += jnp.sum(dhv * xh, axis=0, keepdims=True)

    row = pl.BlockSpec((tm, d), lambda i: (i, 0))
    vec = pl.BlockSpec((1, d), lambda i: (0, 0))
    return pl.pallas_call(
        body, out_shape=[jax.ShapeDtypeStruct((t, d), F32), jax.ShapeDtypeStruct((1, d), F32)], grid=(t // tm,),
        in_specs=[row, vec, row, row], out_specs=[row, vec], name=name,
        compiler_params=_cparams(("arbitrary",)))(x, w, dh, dres)


def _final_loss(x, w, target, *, name):
    t, d = x.shape
    tm = _pick(t, (512, 256, 128))

    def body(x_ref, w_ref, t_ref, l_ref, dx_ref, dw_ref):
        xv = x_ref[...]
        wv = w_ref[...]
        r = lax.rsqrt(jnp.mean(xv * xv, axis=-1, keepdims=True) + NORM_EPS)
        xh = xv * r
        err = xh * wv - t_ref[...]
        dy = err * (1.0 / d)
        dxh = dy * wv
        dx_ref[...] = r * (dxh - xh * jnp.mean(dxh * xh, axis=-1, keepdims=True))

        @pl.when(pl.program_id(0) == 0)
        def _():
            dw_ref[...] = jnp.zeros_like(dw_ref)
            l_ref[...] = jnp.zeros_like(l_ref)

        dw_ref[...] += jnp.sum(dy * xh, axis=0, keepdims=True)
        l_ref[...] += 0.5 * jnp.sum(jnp.mean(err * err, axis=-1, keepdims=True), axis=0, keepdims=True)

    row = pl.BlockSpec((tm, d), lambda i: (i, 0))
    vec = pl.BlockSpec((1, d), lambda i: (0, 0))
    return pl.pallas_call(
        body, out_shape=[jax.ShapeDtypeStruct((SUBLANE, LANE), F32), jax.ShapeDtypeStruct((t, d), F32),
                         jax.ShapeDtypeStruct((1, d), F32)], grid=(t // tm,),
        in_specs=[row, vec, row], out_specs=[pl.BlockSpec((SUBLANE, LANE), lambda i: (0, 0)), row, vec], name=name,
        compiler_params=_cparams(("arbitrary",)))(x, w, target)


def _ffn_up(h, wg, wu, *, name):
    t, d = h.shape
    f = wg.shape[1]
    tm = _pick(t, (512, 256, 128))
    tn = _pick(f, (1408, 512, 256, 128))

    def body(h_ref, wg_ref, wu_ref, g_ref, u_ref, a_ref):
        hv = h_ref[...]
        g = _dg(hv, wg_ref[...], 1, 0, False)
        u = _dg(hv, wu_ref[...], 1, 0, False)
        g_ref[...] = g
        u_ref[...] = u
        a_ref[...] = (_silu(g) * u).astype(BF16)

    hs = pl.BlockSpec((tm, d), lambda j, i: (i, 0))
    ws = pl.BlockSpec((d, tn), lambda j, i: (0, j))
    os_ = pl.BlockSpec((tm, tn), lambda j, i: (i, j))
    return pl.pallas_call(
        body, out_shape=[jax.ShapeDtypeStruct((t, f), F32), jax.ShapeDtypeStruct((t, f), F32),
                         jax.ShapeDtypeStruct((t, f), BF16)], grid=(f // tn, t // tm),
        in_specs=[hs, ws, ws], out_specs=[os_, os_, os_], name=name,
        compiler_params=_cparams(("parallel", "parallel")))(h, wg, wu)


def _ffn_dact(dxo, wd, g, u, *, name):
    t, d = dxo.shape
    f = wd.shape[0]
    tm = _pick(t, (512, 256, 128))
    tn = _pick(f, (1408, 512, 256, 128))

    def body(dx_ref, wd_ref, g_ref, u_ref, dg_ref, du_ref):
        da = 0.5 * _dg(dx_ref[...], wd_ref[...], 1, 1, False)
        gv = g_ref[...]
        sg = _sigmoid(gv)
        dg_ref[...] = (da * u_ref[...] * (sg * (1.0 + gv * (1.0 - sg)))).astype(BF16)
        du_ref[...] = (da * (gv * sg)).astype(BF16)

    xs = pl.BlockSpec((tm, d), lambda j, i: (i, 0))
    ws = pl.BlockSpec((tn, d), lambda j, i: (j, 0))
    os_ = pl.BlockSpec((tm, tn), lambda j, i: (i, j))
    return pl.pallas_call(
        body, out_shape=[jax.ShapeDtypeStruct((t, f), BF16), jax.ShapeDtypeStruct((t, f), BF16)],
        grid=(f // tn, t // tm), in_specs=[xs, ws, os_, os_], out_specs=[os_, os_], name=name,
        compiler_params=_cparams(("parallel", "parallel")))(dxo, wd, g, u)


def _ffn_fwd(x, nw, wg, wu, wd, tag):
    h = _rms_fwd(x, nw, name=f"rms_fwd_{tag}")
    g, u, a = _ffn_up(h, wg, wu, name=f"ffn_up_{tag}")
    y = _mm(a, wd, "nn", res=x, scale=0.5, name=f"ffn_down_{tag}")
    return y, (x, h, g, u, a)


def _ffn_bwd(saved, nw, wg, wu, wd, dxo, tag):
    x, h, g, u, a = saved
    dg, du = _ffn_dact(dxo, wd, g, u, name=f"ffn_dact_{tag}")
    dwd = _mm(a, dxo, "tn", scale=0.5, name=f"ffn_dwd_{tag}")
    dwg = _mm(h, dg, "tn", name=f"ffn_dwg_{tag}")
    dwu = _mm(h, du, "tn", name=f"ffn_dwu_{tag}")
    dh = _mm(dg, wg, "nt", name=f"ffn_dh1_{tag}")
    dh = _mm(du, wu, "nt", res=dh, name=f"ffn_dh2_{tag}")
    dx, dnw = _rms_bwd(x, nw, dh, dxo, name=f"rms_bwd_{tag}")
    return dx, dnw, dwg, dwu, dwd


MIX_HP = 4


def _pairs(ref, bsz, part):
    return jnp.stack([ref[b, :, 512 * hh + HEAD * part:512 * hh + HEAD * (part + 1)]
                      for b in range(bsz) for hh in range(MIX_HP)])


def _hg_fwd(proj, lbs, nw, *, name):
    bsz, l, _ = proj.shape
    nc = l // CHUNK
    groups = HG_HEADS // MIX_HP

    def body(p_ref, lb_ref, nw_ref, y_ref, s_ref, st):
        @pl.when(pl.program_id(1) == 0)
        def _():
            st[...] = jnp.zeros_like(st)

        prev = st[...]
        lb = jnp.stack([lb_ref[:, HEAD * hh:HEAD * (hh + 1)] for _ in range(bsz) for hh in range(MIX_HP)])
        st_new, y = hg_chunk(prev, *[_pairs(p_ref, bsz, part) for part in range(4)], lb, nw_ref[...])
        s_ref[...] = prev.reshape(bsz, MIX_HP, HEAD, HEAD)
        st[...] = st_new
        for b in range(bsz):
            for hh in range(MIX_HP):
                y_ref[b, :, HEAD * hh:HEAD * (hh + 1)] = y[b * MIX_HP + hh]

    return pl.pallas_call(
        body,
        out_shape=[jax.ShapeDtypeStruct((bsz, l, HG_W), F32),
                   jax.ShapeDtypeStruct((bsz, HG_HEADS, nc, HEAD, HEAD), F32)],
        grid=(groups, nc),
        in_specs=[pl.BlockSpec((bsz, CHUNK, MIX_HP * 512), lambda g, c: (0, c, g)),
                  pl.BlockSpec((1, MIX_HP * HEAD), lambda g, c: (0, g)),
                  pl.BlockSpec((1, HEAD), lambda g, c: (0, 0))],
        out_specs=[pl.BlockSpec((bsz, CHUNK, MIX_HP * HEAD), lambda g, c: (0, c, g)),
                   pl.BlockSpec((bsz, MIX_HP, None, HEAD, HEAD), lambda g, c: (0, g, c, 0, 0))],
        scratch_shapes=[pltpu.VMEM((bsz * MIX_HP, HEAD, HEAD), F32)], name=name,
        compiler_params=_cparams(("parallel", "arbitrary")))(proj, lbs, nw)


def _hg_bwd(proj, lbs, nw, states, dy, *, name):
    bsz, l, _ = proj.shape
    nc = l // CHUNK
    groups = HG_HEADS // MIX_HP
    n = bsz * MIX_HP

    def body(p_ref, lb_ref, nw_ref, s_ref, dy_ref, dp_ref, dlb_ref, dnw_ref, dst):
        @pl.when(pl.program_id(1) == 0)
        def _():
            dst[...] = jnp.zeros_like(dst)
            dlb_ref[...] = jnp.zeros_like(dlb_ref)
            dnw_ref[...] = jnp.zeros_like(dnw_ref)

        lb = jnp.stack([lb_ref[:, HEAD * hh:HEAD * (hh + 1)] for _ in range(bsz) for hh in range(MIX_HP)])
        dy3 = jnp.stack([dy_ref[b, :, HEAD * hh:HEAD * (hh + 1)] for b in range(bsz) for hh in range(MIX_HP)])
        _, vjp = jax.vjp(hg_chunk, s_ref[...].reshape(n, HEAD, HEAD), *[_pairs(p_ref, bsz, part) for part in range(4)],
                         lb, nw_ref[...])
        d_st, dq, df, dv, dgl, dlb, dnw = vjp((dst[...], dy3))
        dst[...] = d_st
        for b in range(bsz):
            for hh in range(MIX_HP):
                for part, d in enumerate((dq, df, dv, dgl)):
                    dp_ref[b, :, 512 * hh + HEAD * part:512 * hh + HEAD * (part + 1)] = d[b * MIX_HP + hh]
        dlb_ref[...] += dlb.reshape(bsz, MIX_HP, 1, HEAD)
        dnw_ref[...] += dnw

    rev = lambda c: nc - 1 - c
    return pl.pallas_call(
        body,
        out_shape=[jax.ShapeDtypeStruct((bsz, l, 4 * HG_W), F32),
                   jax.ShapeDtypeStruct((bsz, HG_HEADS, 1, HEAD), F32),
                   jax.ShapeDtypeStruct((groups, 1, HEAD), F32)],
        grid=(groups, nc),
        in_specs=[pl.BlockSpec((bsz, CHUNK, MIX_HP * 512), lambda g, c: (0, rev(c), g)),
                  pl.BlockSpec((1, MIX_HP * HEAD), lambda g, c: (0, g)),
                  pl.BlockSpec((1, HEAD), lambda g, c: (0, 0)),
                  pl.BlockSpec((bsz, MIX_HP, None, HEAD, HEAD), lambda g, c: (0, g, rev(c), 0, 0)),
                  pl.BlockSpec((bsz, CHUNK, MIX_HP * HEAD), lambda g, c: (0, rev(c), g))],
        out_specs=[pl.BlockSpec((bsz, CHUNK, MIX_HP * 512), lambda g, c: (0, rev(c), g)),
                   pl.BlockSpec((bsz, MIX_HP, 1, HEAD), lambda g, c: (0, g, 0, 0)),
                   pl.BlockSpec((None, 1, HEAD), lambda g, c: (g, 0, 0))],
        scratch_shapes=[pltpu.VMEM((n, HEAD, HEAD), F32)], name=name,
        compiler_params=_cparams(("parallel", "arbitrary")))(proj, lbs, nw, states, dy)


def _head_masks(group, bsz):
    n = bsz * MIX_HP
    head = group * MIX_HP + _iota((n, 1, HEAD), 0) % MIX_HP
    lane = _iota((n, 1, HEAD), 2)
    return jnp.where(lane == head, 1.0, 0.0).astype(F32), jnp.where(lane == GDN_HEADS + head, 1.0, 0.0).astype(F32)


def _gdn_fwd(qkvg, bd, ab, dtb, nw, *, name):
    bsz, l, _ = qkvg.shape
    nc = l // CHUNK
    groups = GDN_HEADS // MIX_HP

    def body(p_ref, bd_ref, ab_ref, dtb_ref, nw_ref, y_ref, s_ref, st):
        @pl.when(pl.program_id(1) == 0)
        def _():
            st[...] = jnp.zeros_like(st)

        mb, ma = _head_masks(pl.program_id(0), bsz)
        prev = st[...]
        bd3 = jnp.stack([bd_ref[b] for b in range(bsz) for _ in range(MIX_HP)])
        st_new, y = gdn_chunk(prev, *[_pairs(p_ref, bsz, part) for part in range(4)], bd3, ab_ref[...],
                              dtb_ref[...], nw_ref[...], mb, ma)
        s_ref[...] = prev.reshape(bsz, MIX_HP, HEAD, HEAD)
        st[...] = st_new
        for b in range(bsz):
            for hh in range(MIX_HP):
                y_ref[b, :, HEAD * hh:HEAD * (hh + 1)] = y[b * MIX_HP + hh]

    vec = pl.BlockSpec((1, HEAD), lambda g, c: (0, 0))
    return pl.pallas_call(
        body,
        out_shape=[jax.ShapeDtypeStruct((bsz, l, GDN_W), F32),
                   jax.ShapeDtypeStruct((bsz, GDN_HEADS, nc, HEAD, HEAD), F32)],
        grid=(groups, nc),
        in_specs=[pl.BlockSpec((bsz, CHUNK, MIX_HP * 512), lambda g, c: (0, c, g)),
                  pl.BlockSpec((bsz, CHUNK, HEAD), lambda g, c: (0, c, 0)), vec, vec, vec],
        out_specs=[pl.BlockSpec((bsz, CHUNK, MIX_HP * HEAD), lambda g, c: (0, c, g)),
                   pl.BlockSpec((bsz, MIX_HP, None, HEAD, HEAD), lambda g, c: (0, g, c, 0, 0))],
        scratch_shapes=[pltpu.VMEM((bsz * MIX_HP, HEAD, HEAD), F32)], name=name,
        compiler_params=_cparams(("parallel", "arbitrary")))(qkvg, bd, ab, dtb, nw)


def _gdn_bwd(qkvg, bd, ab, dtb, nw, states, dy, *, name):
    bsz, l, _ = qkvg.shape
    nc = l // CHUNK
    groups = GDN_HEADS // MIX_HP
    n = bsz * MIX_HP

    def body(p_ref, bd_ref, ab_ref, dtb_ref, nw_ref, s_ref, dy_ref, dp_ref, dbd_ref, dab_ref, ddt_ref, dnw_ref,
             dst):
        @pl.when(pl.program_id(1) == 0)
        def _():
            dst[...] = jnp.zeros_like(dst)
            dab_ref[...] = jnp.zeros_like(dab_ref)
            ddt_ref[...] = jnp.zeros_like(ddt_ref)
            dnw_ref[...] = jnp.zeros_like(dnw_ref)

        mb, ma = _head_masks(pl.program_id(0), bsz)
        fn = lambda st, q, k, v, gt, bdv, abv, dtv, nwv: gdn_chunk(st, q, k, v, gt, bdv, abv, dtv, nwv, mb, ma)
        bd3 = jnp.stack([bd_ref[b] for b in range(bsz) for _ in range(MIX_HP)])
        dy3 = jnp.stack([dy_ref[b, :, HEAD * hh:HEAD * (hh + 1)] for b in range(bsz) for hh in range(MIX_HP)])
        _, vjp = jax.vjp(fn, s_ref[...].reshape(n, HEAD, HEAD), *[_pairs(p_ref, bsz, part) for part in range(4)], bd3,
                         ab_ref[...], dtb_ref[...], nw_ref[...])
        d_st, dq, dk, dv, dgt, dbd, dab, ddt, dnw = vjp((dst[...], dy3))
        dst[...] = d_st
        for b in range(bsz):
            for hh in range(MIX_HP):
                for part, d in enumerate((dq, dk, dv, dgt)):
                    dp_ref[b, :, 512 * hh + HEAD * part:512 * hh + HEAD * (part + 1)] = d[b * MIX_HP + hh]
        dbd_ref[...] = jnp.sum(dbd.reshape(bsz, MIX_HP, CHUNK, HEAD), axis=1)
        dab_ref[...] += dab
        ddt_ref[...] += ddt
        dnw_ref[...] += dnw

    rev = lambda c: nc - 1 - c
    vec = pl.BlockSpec((1, HEAD), lambda g, c: (0, 0))
    acc = pl.BlockSpec((None, 1, HEAD), lambda g, c: (g, 0, 0))
    acc_shape = jax.ShapeDtypeStruct((groups, 1, HEAD), F32)
    return pl.pallas_call(
        body,
        out_shape=[jax.ShapeDtypeStruct((bsz, l, 4 * GDN_W), F32),
                   jax.ShapeDtypeStruct((bsz, groups, l, HEAD), F32), acc_shape, acc_shape, acc_shape],
        grid=(groups, nc),
        in_specs=[pl.BlockSpec((bsz, CHUNK, MIX_HP * 512), lambda g, c: (0, rev(c), g)),
                  pl.BlockSpec((bsz, CHUNK, HEAD), lambda g, c: (0, rev(c), 0)), vec, vec, vec,
                  pl.BlockSpec((bsz, MIX_HP, None, HEAD, HEAD), lambda g, c: (0, g, rev(c), 0, 0)),
                  pl.BlockSpec((bsz, CHUNK, MIX_HP * HEAD), lambda g, c: (0, rev(c), g))],
        out_specs=[pl.BlockSpec((bsz, CHUNK, MIX_HP * 512), lambda g, c: (0, rev(c), g)),
                   pl.BlockSpec((bsz, None, CHUNK, HEAD), lambda g, c: (0, g, rev(c), 0)), acc, acc, acc],
        scratch_shapes=[pltpu.VMEM((n, HEAD, HEAD), F32)], name=name,
        compiler_params=_cparams(("parallel", "arbitrary")))(qkvg, bd, ab, dtb, nw, states, dy)


CONV_TL = 256
QKV_LANES = 3 * HEAD


def _conv_fwd(x, w, *, name):
    bsz, l, wd = x.shape
    tl = min(CONV_TL, l)
    nr = l // tl
    hb = tl // SUBLANE

    def body(x_ref, xp_ref, w_ref, o_ref):
        xv = x_ref[...]
        halo = jnp.where(pl.program_id(2) > 0, xp_ref[...], 0.0)
        xc = jnp.concatenate([halo, xv], axis=0)
        wv = w_ref[...]
        z = wv[CONV_W - 1:CONV_W] * xv
        for j in range(CONV_W - 1):
            z = z + wv[j:j + 1] * pltpu.roll(xc, CONV_W - 1 - j, 0)[SUBLANE:]
        o_ref[...] = jnp.where(_iota((tl, 512), 1) < QKV_LANES, _silu(z), xv)

    return pl.pallas_call(
        body, out_shape=jax.ShapeDtypeStruct(x.shape, F32), grid=(bsz, wd // 512, nr),
        in_specs=[pl.BlockSpec((None, tl, 512), lambda b, h, r: (b, r, h)),
                  pl.BlockSpec((None, SUBLANE, 512), lambda b, h, r: (b, jnp.maximum(r * hb - 1, 0), h)),
                  pl.BlockSpec((SUBLANE, 512), lambda b, h, r: (0, h))],
        out_specs=pl.BlockSpec((None, tl, 512), lambda b, h, r: (b, r, h)), name=name,
        compiler_params=_cparams(("parallel", "parallel", "arbitrary")))(x, x, w)


def _conv_bwd(x, w, dy, *, name):
    bsz, l, wd = x.shape
    tl = min(CONV_TL, l)
    nr = l // tl
    hb = tl // SUBLANE

    def body(x_ref, xp_ref, xn_ref, w_ref, dy_ref, dyn_ref, dx_ref, dw_ref):
        r = pl.program_id(2)
        xv = x_ref[...]
        prev = jnp.where(r > 0, xp_ref[...], 0.0)
        last = r == nr - 1
        nxt = jnp.where(last, 0.0, xn_ref[...])
        dyn = jnp.where(last, 0.0, dyn_ref[...])
        xc = jnp.concatenate([prev, xv, nxt], axis=0)
        wv = w_ref[...]
        shifted = [pltpu.roll(xc, CONV_W - 1 - j, 0) for j in range(CONV_W - 1)] + [xc]
        z = wv[0:1] * shifted[0]
        for j in range(1, CONV_W):
            z = z + wv[j:j + 1] * shifted[j]
        z = z[SUBLANE:]
        sg = _sigmoid(z)
        dyc = jnp.concatenate([dy_ref[...], dyn], axis=0)
        dz = dyc * (sg * (1.0 + z * (1.0 - sg)))
        dx = wv[CONV_W - 1:CONV_W] * dz[:tl]
        for j in range(CONV_W - 1):
            s = CONV_W - 1 - j
            dx = dx + wv[j:j + 1] * pltpu.roll(dz, tl + SUBLANE - s, 0)[:tl]
        conv_lane = _iota((tl, 512), 1) < QKV_LANES
        dx_ref[...] = jnp.where(conv_lane, dx, dy_ref[...])

        @pl.when(r == 0)
        def _():
            dw_ref[...] = jnp.zeros_like(dw_ref)

        dzt = dz[:tl]
        rows = [jnp.sum(dzt * shifted[j][SUBLANE:SUBLANE + tl], axis=0, keepdims=True) for j in range(CONV_W)]
        rows.append(jnp.zeros((SUBLANE - CONV_W, 512), F32))
        dw_ref[...] += jnp.concatenate(rows, axis=0)

    blk = pl.BlockSpec((None, tl, 512), lambda b, h, r: (b, r, h))
    prev = pl.BlockSpec((None, SUBLANE, 512), lambda b, h, r: (b, jnp.maximum(r * hb - 1, 0), h))
    nxt = pl.BlockSpec((None, SUBLANE, 512), lambda b, h, r: (b, jnp.minimum((r + 1) * hb, l // SUBLANE - 1), h))
    return pl.pallas_call(
        body, out_shape=[jax.ShapeDtypeStruct(x.shape, F32), jax.ShapeDtypeStruct((bsz, SUBLANE, wd), F32)],
        grid=(bsz, wd // 512, nr),
        in_specs=[blk, prev, nxt, pl.BlockSpec((SUBLANE, 512), lambda b, h, r: (0, h)), blk, nxt],
        out_specs=[blk, pl.BlockSpec((None, SUBLANE, 512), lambda b, h, r: (b, 0, h))], name=name,
        compiler_params=_cparams(("parallel", "parallel", "arbitrary")))(x, x, x, w, dy, dy)


S5_LC = 256
S5_TN = 512


def _s5_scan(xr, xi, pw, *, reverse, name):
    bsz, l, n = xr.shape
    lc = min(S5_LC, l)
    nc = l // lc
    steps = int(math.log2(lc))

    def shift(v, s):
        if reverse:
            if s >= SUBLANE:
                return jnp.concatenate([v[s:], jnp.zeros((s, v.shape[1]), F32)], axis=0)
            return jnp.where(_iota(v.shape, 0) < lc - s, pltpu.roll(v, lc - s, 0), 0.0)
        if s >= SUBLANE:
            return jnp.concatenate([jnp.zeros((s, v.shape[1]), F32), v[:lc - s]], axis=0)
        return jnp.where(_iota(v.shape, 0) >= s, pltpu.roll(v, s, 0), 0.0)

    def body(xr_ref, xi_ref, pw_ref, hr_ref, hi_ref, cr, ci):
        @pl.when(pl.program_id(2) == 0)
        def _():
            cr[...] = jnp.zeros_like(cr)
            ci[...] = jnp.zeros_like(ci)

        vr, vi = xr_ref[...], xi_ref[...]
        for k in range(steps):
            s = 1 << k
            ar, ai = pw_ref[0, s - 1:s, :], pw_ref[1, s - 1:s, :]
            sr, si = shift(vr, s), shift(vi, s)
            vr, vi = vr + ar * sr - ai * si, vi + ar * si + ai * sr
        pr, pi = pw_ref[2], pw_ref[3]
        c_r, c_i = cr[...], ci[...]
        vr, vi = vr + pr * c_r - pi * c_i, vi + pr * c_i + pi * c_r
        hr_ref[...] = vr
        hi_ref[...] = vi
        edge = 0 if reverse else lc - 1
        cr[...] = vr[edge:edge + 1]
        ci[...] = vi[edge:edge + 1]

    tmap = (lambda c: nc - 1 - c) if reverse else (lambda c: c)
    blk = pl.BlockSpec((None, lc, S5_TN), lambda b, j, c: (b, tmap(c), j))
    return pl.pallas_call(
        body, out_shape=[jax.ShapeDtypeStruct(xr.shape, F32), jax.ShapeDtypeStruct(xr.shape, F32)],
        grid=(bsz, n // S5_TN, nc),
        in_specs=[blk, blk, pl.BlockSpec((4, lc, S5_TN), lambda b, j, c: (0, 0, j))], out_specs=[blk, blk],
        scratch_shapes=[pltpu.VMEM((1, S5_TN), F32), pltpu.VMEM((1, S5_TN), F32)], name=name,
        compiler_params=_cparams(("parallel", "parallel", "arbitrary")))(xr, xi, pw)


def _s5_dabar(gr, gi, hr, hi, *, name):
    bsz, l, n = gr.shape
    lc = min(S5_LC, l)
    nc = l // lc
    hb = lc // SUBLANE

    def body(gr_ref, gi_ref, hr_ref, hi_ref, hrp_ref, hip_ref, o_ref):
        c = pl.program_id(2)

        @pl.when(c == 0)
        def _():
            o_ref[...] = jnp.zeros_like(o_ref)

        def prev(h_ref, hp_ref):
            first = jnp.where(c > 0, hp_ref[SUBLANE - 1:SUBLANE, :], 0.0)
            return jnp.where(_iota((lc, S5_TN), 0) == 0, first, pltpu.roll(h_ref[...], 1, 0))

        pr, pi = prev(hr_ref, hrp_ref), prev(hi_ref, hip_ref)
        g_r, g_i = gr_ref[...], gi_ref[...]
        d_re = jnp.sum(g_r * pr + g_i * pi, axis=0, keepdims=True)
        d_im = jnp.sum(g_i * pr - g_r * pi, axis=0, keepdims=True)
        o_ref[...] += jnp.concatenate([d_re, d_im, jnp.zeros((SUBLANE - 2, S5_TN), F32)], axis=0)

    blk = pl.BlockSpec((None, lc, S5_TN), lambda b, j, c: (b, c, j))
    prv = pl.BlockSpec((None, SUBLANE, S5_TN), lambda b, j, c: (b, jnp.maximum(c * hb - 1, 0), j))
    return pl.pallas_call(
        body, out_shape=jax.ShapeDtypeStruct((bsz, SUBLANE, n), F32), grid=(bsz, n // S5_TN, nc),
        in_specs=[blk, blk, blk, blk, prv, prv],
        out_specs=pl.BlockSpec((None, SUBLANE, S5_TN), lambda b, j, c: (b, 0, j)), name=name,
        compiler_params=_cparams(("parallel", "parallel", "arbitrary")))(gr, gi, hr, hi, hr, hi)


def _s5_post_fwd(ych, u, d, wglu, *, name):
    t, w = ych.shape
    tm = _pick(t, (512, 256, 128))

    def body(y_ref, u_ref, d_ref, w_ref, o_ref):
        o_ref[...] = s5_post(y_ref[...], u_ref[...], d_ref[...], w_ref[...])

    row = pl.BlockSpec((tm, w), lambda i: (i, 0))
    return pl.pallas_call(
        body, out_shape=jax.ShapeDtypeStruct((t, w), F32), grid=(t // tm,),
        in_specs=[row, row, pl.BlockSpec((1, w), lambda i: (0, 0)), pl.BlockSpec((w, w), lambda i: (0, 0))],
        out_specs=row, name=name, compiler_params=_cparams(("parallel",)))(ych, u, d, wglu)


def _s5_post_bwd(ych, u, d, wglu, dout, *, name):
    t, w = ych.shape
    tm = _pick(t, (512, 256, 128))

    def body(y_ref, u_ref, d_ref, w_ref, do_ref, dy_ref, du_ref, dd_ref, dw_ref):
        _, vjp = jax.vjp(s5_post, y_ref[...], u_ref[...], d_ref[...], w_ref[...])
        dy, du, dd, dw = vjp(do_ref[...])
        dy_ref[...] = dy
        du_ref[...] = du

        @pl.when(pl.program_id(0) == 0)
        def _():
            dd_ref[...] = jnp.zeros_like(dd_ref)
            dw_ref[...] = jnp.zeros_like(dw_ref)

        dd_ref[...] += dd
        dw_ref[...] += dw

    row = pl.BlockSpec((tm, w), lambda i: (i, 0))
    vec = pl.BlockSpec((1, w), lambda i: (0, 0))
    mat = pl.BlockSpec((w, w), lambda i: (0, 0))
    return pl.pallas_call(
        body, out_shape=[jax.ShapeDtypeStruct((t, w), F32), jax.ShapeDtypeStruct((t, w), F32),
                         jax.ShapeDtypeStruct((1, w), F32), jax.ShapeDtypeStruct((w, w), F32)], grid=(t // tm,),
        in_specs=[row, row, vec, mat, row], out_specs=[row, row, vec, mat], name=name,
        compiler_params=_cparams(("arbitrary",)))(ych, u, d, wglu, dout)


def _s5_disc(a_re, a_im, b_re, b_im, log_dt):
    dt = jnp.exp(log_dt)[:, None]
    mag = jnp.exp(dt * a_re)
    ang = dt * a_im
    abar_re = mag * jnp.cos(ang)
    abar_im = mag * jnp.sin(ang)
    den = a_re * a_re + a_im * a_im
    zr = abar_re - 1.0
    zi = abar_im
    coef_re = ((zr * a_re + zi * a_im) / den)[..., None]
    coef_im = ((zi * a_re - zr * a_im) / den)[..., None]
    return abar_re, abar_im, coef_re * b_re - coef_im * b_im, coef_re * b_im + coef_im * b_re


def _s5_powers(a_re, a_im, log_dt, lc):
    dt = jnp.exp(log_dt)[:, None]
    n = jnp.arange(1, lc + 1, dtype=F32)[:, None, None]
    mag = jnp.exp(n * (dt * a_re)[None])
    ang = n * (dt * a_im)[None]
    pr = (mag * jnp.cos(ang)).reshape(lc, S5_N)
    pi = (mag * jnp.sin(ang)).reshape(lc, S5_N)
    fwd = jnp.stack([pr, pi, pr, pi])
    bwd = jnp.stack([pr, -pi, pr[::-1], -pi[::-1]])
    return fwd, bwd


def _block_diag_in(bb):
    eye = jnp.eye(S5_GROUPS, dtype=bb.dtype)
    return jnp.einsum("gnp,gh->gphn", bb, eye).reshape(S5_W, S5_N)


def _block_diag_out(cc):
    eye = jnp.eye(S5_GROUPS, dtype=cc.dtype)
    return jnp.einsum("gpn,gh->gnhp", cc, eye).reshape(S5_N, S5_W)


def _adamw(w, g, m, v, *, name):
    shape = w.shape
    cols = shape[-1] if w.ndim > 1 else shape[0]
    rows = w.size // cols
    tr = _pick(rows, (512, 352, 256, 128, 64, 32, 16, 8)) if rows % SUBLANE == 0 else rows
    flat = lambda t: t.reshape(rows, cols)

    def body(w_ref, g_ref, m_ref, v_ref, d_ref, nm_ref, nv_ref):
        gv = g_ref[...]
        mn = ADAM_B1 * m_ref[...] + (1.0 - ADAM_B1) * gv
        vn = ADAM_B2 * v_ref[...] + (1.0 - ADAM_B2) * jnp.square(gv)
        m_hat = mn / (1.0 - ADAM_B1 ** ADAM_STEP)
        v_hat = vn / (1.0 - ADAM_B2 ** ADAM_STEP)
        d_ref[...] = -ADAM_LR * (m_hat / (jnp.sqrt(v_hat) + ADAM_EPS) + ADAM_WD * w_ref[...])
        nm_ref[...] = mn
        nv_ref[...] = vn

    blk = pl.BlockSpec((tr, cols), lambda i: (i, 0))
    sds = jax.ShapeDtypeStruct((rows, cols), F32)
    outs = pl.pallas_call(body, out_shape=[sds, sds, sds], grid=(rows // tr,), in_specs=[blk] * 4,
                          out_specs=[blk] * 3, name=name,
                          compiler_params=_cparams(("parallel",)))(flat(w), flat(g), flat(m), flat(v))
    return tuple(o.reshape(shape) for o in outs)


ANY = pl.BlockSpec(memory_space=pl.ANY)


def _coords():
    x, y, c = lax.axis_index("x"), lax.axis_index("y"), lax.axis_index("c")
    chips = [(1 - x, y), (x, 1 - y), (1 - x, 1 - y)]
    return x, y, c, chips


def _all_gather_chips(packs, *, name):
    n = len(packs)

    def body(*refs):
        p_refs, o_refs, (send, recv) = refs[:n], refs[n:2 * n], refs[2 * n:]
        x, y, c, chips = _coords()
        sibling = (x, y, 1 - c)

        def copy(i, k, src, dst, to):
            return pltpu.make_async_remote_copy(src_ref=src, dst_ref=dst, send_sem=send.at[6 * i + k],
                                                recv_sem=recv.at[6 * i + k], device_id=to, device_id_type=MESH_ID)

        def part(i, chip, half):
            return o_refs[i].at[2 * chip[0] + chip[1], half]

        first = [copy(i, k, p_refs[i].at[c], part(i, (x, y), c), (*chip, c))
                 for i in range(n) for k, chip in enumerate(chips)]
        for cp in first:
            cp.start()
        passed = []
        for i in range(n):
            for k, chip in enumerate(chips):
                copy(i, k, part(i, chip, c), part(i, chip, c), (x, y, c)).wait_recv()
                passed.append(copy(i, 3 + k, part(i, chip, c), part(i, chip, c), sibling))
                passed[-1].start()
        for i in range(n):
            for k, chip in enumerate(chips):
                copy(i, 3 + k, part(i, chip, 1 - c), part(i, chip, 1 - c), (x, y, c)).wait_recv()
        for cp in first + passed:
            cp.wait_send()

    return pl.pallas_call(
        body, out_shape=[jax.ShapeDtypeStruct((N_CHIPS,) + p.shape, p.dtype) for p in packs], in_specs=[ANY] * n,
        out_specs=[ANY] * n,
        scratch_shapes=[pltpu.SemaphoreType.DMA((6 * n,)), pltpu.SemaphoreType.DMA((6 * n,))], name=name)(*packs)


def _swap_halves(gs, *, name):
    n = len(gs)

    def body(*refs):
        g_refs, o_refs, (send, recv) = refs[:n], refs[n:2 * n], refs[2 * n:]
        x, y, c, _ = _coords()
        cps = [pltpu.make_async_remote_copy(src_ref=g_refs[i].at[:, 1 - c], dst_ref=o_refs[i], send_sem=send.at[i],
                                            recv_sem=recv.at[i], device_id=(x, y, 1 - c), device_id_type=MESH_ID)
               for i in range(n)]
        for cp in cps:
            cp.start()
        for cp in cps:
            cp.wait()

    return pl.pallas_call(
        body, out_shape=[jax.ShapeDtypeStruct(g.shape[:1] + g.shape[2:], g.dtype) for g in gs], in_specs=[ANY] * n,
        out_specs=[ANY] * n, scratch_shapes=[pltpu.SemaphoreType.DMA((n,)), pltpu.SemaphoreType.DMA((n,))],
        name=name)(*gs)


def _add_sibling(g, got, *, name):
    n, _, m, r, w = g.shape

    def body(g0_ref, g1_ref, r_ref, a_ref, own_ref):
        c = lax.axis_index("c")
        mine = 2 * lax.axis_index("x") + lax.axis_index("y")
        j = pl.program_id(1)
        s = jnp.where(c == 0, g0_ref[...], g1_ref[...]) + r_ref[...]
        a_ref[...] = s.astype(BF16)

        @pl.when(j == 0)
        def _():
            own_ref[...] = jnp.zeros_like(own_ref)

        own_ref[...] = jnp.where(j == mine, s, own_ref[...])

    return pl.pallas_call(
        body, out_shape=[jax.ShapeDtypeStruct((n, m, r, w), BF16), jax.ShapeDtypeStruct((m, r, w), F32)],
        grid=(m, n),
        in_specs=[pl.BlockSpec((None, None, None, r, w), lambda e, j: (j, 0, e, 0, 0)),
                  pl.BlockSpec((None, None, None, r, w), lambda e, j: (j, 1, e, 0, 0)),
                  pl.BlockSpec((None, None, r, w), lambda e, j: (j, e, 0, 0))],
        out_specs=[pl.BlockSpec((None, None, r, w), lambda e, j: (j, e, 0, 0)),
                   pl.BlockSpec((None, r, w), lambda e, j: (e, 0, 0))],
        name=name, compiler_params=_cparams(("parallel", "arbitrary")))(g, g, got)


def _exchange_chips(a16s, *, name):
    n = len(a16s)

    def body(*refs):
        a_refs, o_refs, (send, recv) = refs[:n], refs[n:2 * n], refs[2 * n:]
        x, y, c, chips = _coords()
        cps = [pltpu.make_async_remote_copy(src_ref=a_refs[i].at[2 * chip[0] + chip[1]], dst_ref=o_refs[i].at[k],
                                            send_sem=send.at[3 * i + k], recv_sem=recv.at[3 * i + k],
                                            device_id=(*chip, c), device_id_type=MESH_ID)
               for i in range(n) for k, chip in enumerate(chips)]
        for cp in cps:
            cp.start()
        for cp in cps:
            cp.wait()

    return pl.pallas_call(
        body, out_shape=[jax.ShapeDtypeStruct((N_CHIPS - 1,) + a.shape[1:], a.dtype) for a in a16s],
        in_specs=[ANY] * n, out_specs=[ANY] * n,
        scratch_shapes=[pltpu.SemaphoreType.DMA((3 * n,)), pltpu.SemaphoreType.DMA((3 * n,))], name=name)(*a16s)


def _add_chips(own, got, *, name):
    m, r, w = own.shape

    def body(o_ref, g_ref, s_ref):
        s = o_ref[...]
        for k in range(N_CHIPS - 1):
            s = s + g_ref[k].astype(F32)
        s_ref[...] = s

    return pl.pallas_call(
        body, out_shape=jax.ShapeDtypeStruct((m, r, w), F32), grid=(m,),
        in_specs=[pl.BlockSpec((None, r, w), lambda i: (i, 0, 0)),
                  pl.BlockSpec((N_CHIPS - 1, None, r, w), lambda i: (0, i, 0, 0))],
        out_specs=pl.BlockSpec((None, r, w), lambda i: (i, 0, 0)), name=name,
        compiler_params=_cparams(("parallel",)))(own, got)


def _swap_reduced(halves, *, name):
    n = len(halves)

    def body(*refs):
        h_refs, o_refs, (send, recv) = refs[:n], refs[n:2 * n], refs[2 * n:]
        x, y, c, _ = _coords()
        cps = [pltpu.make_async_remote_copy(src_ref=h_refs[i], dst_ref=o_refs[i], send_sem=send.at[i],
                                            recv_sem=recv.at[i], device_id=(x, y, 1 - c), device_id_type=MESH_ID)
               for i in range(n)]
        for cp in cps:
            cp.start()
        for cp in cps:
            cp.wait()

    return pl.pallas_call(
        body, out_shape=[jax.ShapeDtypeStruct(h.shape, h.dtype) for h in halves], in_specs=[ANY] * n,
        out_specs=[ANY] * n, scratch_shapes=[pltpu.SemaphoreType.DMA((n,)), pltpu.SemaphoreType.DMA((n,))],
        name=name)(*halves)


def _gather_all(vec, *, name):
    s, w = vec.shape

    def body(v_ref, o_ref, send, recv, local):
        x, y, c, _ = _coords()
        me = 4 * x + 2 * y + c
        mine = pltpu.make_async_copy(v_ref, o_ref.at[me], local)
        mine.start()
        cps = []
        for k in range(1, N_DEV):
            px = 1 - x if k & 4 else x
            py = 1 - y if k & 2 else y
            pc = 1 - c if k & 1 else c
            peer = 4 * px + 2 * py + pc
            cps.append((pltpu.make_async_remote_copy(src_ref=v_ref, dst_ref=o_ref.at[me], send_sem=send.at[k - 1],
                                                     recv_sem=recv.at[k - 1], device_id=(px, py, pc),
                                                     device_id_type=MESH_ID),
                        pltpu.make_async_remote_copy(src_ref=v_ref, dst_ref=o_ref.at[peer], send_sem=send.at[k - 1],
                                                     recv_sem=recv.at[k - 1], device_id=(px, py, pc),
                                                     device_id_type=MESH_ID)))
        for snd, _ in cps:
            snd.start()
        for snd, rcv in cps:
            rcv.wait_recv()
            snd.wait_send()
        mine.wait()

    return pl.pallas_call(
        body, out_shape=jax.ShapeDtypeStruct((N_DEV, s, w), vec.dtype), in_specs=[ANY], out_specs=ANY,
        scratch_shapes=[pltpu.SemaphoreType.DMA((N_DEV - 1,)), pltpu.SemaphoreType.DMA((N_DEV - 1,)),
                        pltpu.SemaphoreType.DMA], name=name)(vec)


def _sum_all(parts, *, name):
    n, s, w = parts.shape
    tr = _pick(s, (96, 72, 48, 32, 24, 16, 8))

    def body(p_ref, o_ref):
        acc = p_ref[0]
        for k in range(1, n):
            acc = acc + p_ref[k]
        o_ref[...] = acc

    return pl.pallas_call(
        body, out_shape=jax.ShapeDtypeStruct((s, w), F32), grid=(s // tr,),
        in_specs=[pl.BlockSpec((n, tr, w), lambda i: (0, i, 0))], out_specs=pl.BlockSpec((tr, w), lambda i: (i, 0)),
        name=name, compiler_params=_cparams(("parallel",)))(parts)


def _rows_pack(pieces, lead):
    rows = [p.reshape(p.shape[:lead] + (-1, PACK_COLS)) for p in pieces]
    spare = 2 * ROW_BLOCKS * ROW_BLOCK - sum(r.shape[lead] for r in rows)
    rows.append(jnp.zeros(rows[0].shape[:lead] + (spare, PACK_COLS), rows[0].dtype))
    return jnp.concatenate(rows, axis=lead).reshape(rows[0].shape[:lead] + (2, ROW_BLOCKS, ROW_BLOCK, PACK_COLS))


def _rows_unpack(packed, shapes, lead):
    flat = packed.reshape(packed.shape[:lead] + (-1, PACK_COLS))
    out, off = [], 0
    for shp in shapes:
        rows = int(np.prod(shp)) // PACK_COLS
        out.append(lax.slice_in_dim(flat, off, off + rows, axis=lead).reshape(packed.shape[:lead] + tuple(shp)))
        off += rows
    return out


def _unshard(piece, axis):
    return jnp.concatenate([piece[j] for j in range(N_CHIPS)], axis=axis)


def _to_shards(full, axis):
    return jnp.stack(jnp.split(full, N_CHIPS, axis=axis))


def _by_head(w, parts, heads):
    r = w.shape[0]
    return w.reshape(r, parts, heads, HEAD).transpose(0, 2, 1, 3).reshape(r, parts * heads * HEAD)


def _by_part(w, parts, heads):
    r = w.shape[0]
    return w.reshape(r, heads, parts, HEAD).transpose(0, 2, 1, 3).reshape(r, parts * heads * HEAD)


def kernel(x, ffn1_norm, ffn1_w_gate, ffn1_w_up, ffn1_w_down, mix_norm, ffn2_norm, ffn2_w_gate, ffn2_w_up, ffn2_w_down, ev_w_in, hg_lb_logits, hg_norm_w, s5_a_re, s5_a_im, s5_b_re, s5_b_im, s5_c_re, s5_c_im, s5_d, s5_log_dt, s5_w_glu, ev_w_out, od_w_in, gdn_conv_w, gdn_a_log, gdn_dt_bias, gdn_norm_w, od_w_out, final_norm, loss_target, m_ffn1_norm, m_ffn1_w_gate, m_ffn1_w_up, m_ffn1_w_down, m_mix_norm, m_ffn2_norm, m_ffn2_w_gate, m_ffn2_w_up, m_ffn2_w_down, m_ev_w_in, m_hg_lb_logits, m_hg_norm_w, m_s5_a_re, m_s5_a_im, m_s5_b_re, m_s5_b_im, m_s5_c_re, m_s5_c_im, m_s5_d, m_s5_log_dt, m_s5_w_glu, m_ev_w_out, m_od_w_in, m_gdn_conv_w, m_gdn_a_log, m_gdn_dt_bias, m_gdn_norm_w, m_od_w_out, m_final_norm, v_ffn1_norm, v_ffn1_w_gate, v_ffn1_w_up, v_ffn1_w_down, v_mix_norm, v_ffn2_norm, v_ffn2_w_gate, v_ffn2_w_up, v_ffn2_w_down, v_ev_w_in, v_hg_lb_logits, v_hg_norm_w, v_s5_a_re, v_s5_a_im, v_s5_b_re, v_s5_b_im, v_s5_c_re, v_s5_c_im, v_s5_d, v_s5_log_dt, v_s5_w_glu, v_ev_w_out, v_od_w_in, v_gdn_conv_w, v_gdn_a_log, v_gdn_dt_bias, v_gdn_norm_w, v_od_w_out, v_final_norm):
    args = locals()
    wts = {n: args[n] for n in WEIGHTS}
    mom = {n: args["m_" + n] for n in WEIGHTS}
    var = {n: args["v_" + n] for n in WEIGHTS}
    bsz, seq, _ = x.shape
    t = bsz * seq

    def wire(n):
        if n == "gdn_conv_w":
            return lax.bitcast_convert_type(wts[n], BF16)
        return wts[n].astype(BF16)

    chip = 2 * lax.axis_index("x") + lax.axis_index("y")
    is_own = (jnp.arange(N_CHIPS) == chip).reshape(N_CHIPS, 1, 1, 1, 1)
    n_wide = wts[WIDE[0]].shape[0]
    pack_a = jnp.concatenate([wts[n].astype(BF16) for n in WIDE], axis=0)
    pack_a = pack_a.reshape((2, len(WIDE) * n_wide // 2) + pack_a.shape[1:])
    pack_b = _rows_pack([wire(n) for n, _ in ROWS], 0)
    got_a, got_b = _all_gather_chips([pack_a, pack_b], name="gather_weights")
    got_a = jnp.where(is_own, pack_a[None], got_a).reshape((N_CHIPS, len(WIDE) * n_wide) + pack_a.shape[2:])
    got_b = jnp.where(is_own, pack_b[None], got_b)
    full = {}
    for k, n in enumerate(WIDE):
        full[n] = jnp.concatenate([got_a[j, k * n_wide:(k + 1) * n_wide] for j in range(N_CHIPS)], axis=2)
    for (n, axis), piece in zip(ROWS, _rows_unpack(got_b, [wire(n).shape for n, _ in ROWS], 1)):
        if n == "gdn_conv_w":
            piece = lax.bitcast_convert_type(piece, F32)
        full[n] = _unshard(piece, axis)

    ev_cols = lambda w: jnp.concatenate([_by_head(w[:, :4 * HG_W], 4, HG_HEADS), w[:, 4 * HG_W:]], axis=1)
    ev_cols_back = lambda w: jnp.concatenate([_by_part(w[:, :4 * HG_W], 4, HG_HEADS), w[:, 4 * HG_W:]], axis=1)

    lbs_fn = lambda lg: (lambda p: jnp.cumsum(p, axis=0) - p[0])(jax.nn.softmax(lg, axis=0))
    lbs, lbs_vjp = jax.vjp(lbs_fn, hg_lb_logits)
    lc = min(S5_LC, seq)

    xs = x.reshape(t, D_MODEL)
    saved = []
    for layer in range(DEPTH):
        j = layer // 2
        rec = {}
        xs, rec["ffn1"] = _ffn_fwd(xs, ffn1_norm[layer][None], full["ffn1_w_gate"][layer], full["ffn1_w_up"][layer],
                                   full["ffn1_w_down"][layer], f"a{layer}")
        rec["x_mix"] = xs
        h = _rms_fwd(xs, mix_norm[layer][None], name=f"rms_fwd_mix{layer}")
        rec["h"] = h
        if layer % 2 == 0:
            w_in = ev_cols(full["ev_w_in"][j])
            proj = _mm(h, w_in, "nn", name=f"ev_proj{layer}").reshape(bsz, seq, -1)
            rec["proj"] = proj
            y_a, rec["hg_states"] = _hg_fwd(proj, lbs[j][None], hg_norm_w[j][None], name=f"hgrn2_fwd{layer}")
            disc, rec["disc_vjp"] = jax.vjp(_s5_disc, s5_a_re[j], s5_a_im[j], s5_b_re[j], s5_b_im[j], s5_log_dt[j])
            wb_re, wb_im = _block_diag_in(disc[2]).astype(BF16), _block_diag_in(disc[3]).astype(BF16)
            wc_re = _block_diag_out(s5_c_re[j]).astype(BF16)
            wc_im = _block_diag_out(-s5_c_im[j]).astype(BF16)
            pw_f, pw_b = _s5_powers(s5_a_re[j], s5_a_im[j], s5_log_dt[j], lc)
            u = proj[:, :, 4 * HG_W:].reshape(t, S5_W)
            bu_re = _mm(u, wb_re, "nn", name=f"s5_bu_re{layer}").reshape(bsz, seq, S5_N)
            bu_im = _mm(u, wb_im, "nn", name=f"s5_bu_im{layer}").reshape(bsz, seq, S5_N)
            h_re, h_im = _s5_scan(bu_re, bu_im, pw_f, reverse=False, name=f"s5_scan_fwd{layer}")
            h_re2, h_im2 = h_re.reshape(t, S5_N), h_im.reshape(t, S5_N)
            ych = _mm(h_re2, wc_re, "nn", name=f"s5_out_re{layer}")
            ych = _mm(h_im2, wc_im, "nn", res=ych, name=f"s5_out_im{layer}")
            w_glu = full["s5_w_glu"][j]
            y_b = _s5_post_fwd(ych, u, s5_d[j][None], w_glu, name=f"s5_post_fwd{layer}")
            rec.update(u=u, h_re=h_re, h_im=h_im, ych=ych, wb=(wb_re, wb_im), wc=(wc_re, wc_im), pw_b=pw_b,
                       y_a=y_a.reshape(t, HG_W), y_b=y_b)
            w_out = full["ev_w_out"][j]
            xs = _mm(rec["y_a"], w_out[:HG_W], "nn", res=xs, name=f"ev_out_a{layer}")
            xs = _mm(y_b, w_out[HG_W:], "nn", res=xs, name=f"ev_out_b{layer}")
        else:
            w_in = full["od_w_in"][j]
            w_main = _by_head(w_in[:, :4 * GDN_W], 4, GDN_HEADS)
            w_bd = jnp.pad(w_in[:, 4 * GDN_W:], ((0, 0), (0, HEAD - 2 * GDN_HEADS)))
            proj = _mm(h, w_main, "nn", name=f"od_proj{layer}").reshape(bsz, seq, -1)
            bd = _mm(h, w_bd, "nn", name=f"od_proj_bd{layer}").reshape(bsz, seq, HEAD)
            conv_w = full["gdn_conv_w"][j].reshape(CONV_W, 3, GDN_HEADS, HEAD).transpose(0, 2, 1, 3)
            conv_w = jnp.pad(conv_w, ((0, SUBLANE - CONV_W), (0, 0), (0, 1), (0, 0))).reshape(SUBLANE, 4 * GDN_W)
            qkvg = _conv_fwd(proj, conv_w, name=f"gdn_conv_fwd{layer}")
            ab = jnp.pad(gdn_a_log[j][None], ((0, 0), (GDN_HEADS, HEAD - 2 * GDN_HEADS)))
            dtb = jnp.pad(gdn_dt_bias[j][None], ((0, 0), (GDN_HEADS, HEAD - 2 * GDN_HEADS)))
            y, states = _gdn_fwd(qkvg, bd, ab, dtb, gdn_norm_w[j][None], name=f"gdn_fwd{layer}")
            rec.update(proj=proj, bd=bd, conv_w=conv_w, qkvg=qkvg, ab=ab, dtb=dtb, states=states,
                       y=y.reshape(t, GDN_W), w_main=w_main, w_bd=w_bd)
            xs = _mm(rec["y"], full["od_w_out"][j], "nn", res=xs, name=f"od_out{layer}")
        xs, rec["ffn2"] = _ffn_fwd(xs, ffn2_norm[layer][None], full["ffn2_w_gate"][layer], full["ffn2_w_up"][layer],
                                   full["ffn2_w_down"][layer], f"b{layer}")
        saved.append(rec)

    loss_part, dx, d_final = _final_loss(xs, final_norm[None], loss_target.reshape(t, D_MODEL), name="loss_head")
    loss = lax.psum(loss_part[0, 0], ("x", "y", "c"))

    grads = {n: [None] * wts[n].shape[0] for n in WEIGHTS if n != "final_norm"}
    grads["final_norm"] = d_final[0]
    d_lbs = [None] * 2
    for layer in reversed(range(DEPTH)):
        j = layer // 2
        rec = saved[layer]
        dx, dn, dwg, dwu, dwd = _ffn_bwd(rec["ffn2"], ffn2_norm[layer][None], full["ffn2_w_gate"][layer],
                                         full["ffn2_w_up"][layer], full["ffn2_w_down"][layer], dx, f"b{layer}")
        grads["ffn2_norm"][layer], grads["ffn2_w_gate"][layer] = dn[0], dwg
        grads["ffn2_w_up"][layer], grads["ffn2_w_down"][layer] = dwu, dwd
        h = rec["h"]
        if layer % 2 == 0:
            w_out = full["ev_w_out"][j]
            dy_a = _mm(dx, w_out[:HG_W], "nt", name=f"ev_dya{layer}")
            dy_b = _mm(dx, w_out[HG_W:], "nt", name=f"ev_dyb{layer}")
            dw_out = jnp.concatenate([_mm(rec["y_a"], dx, "tn", name=f"ev_dwo_a{layer}"),
                                      _mm(rec["y_b"], dx, "tn", name=f"ev_dwo_b{layer}")], axis=0)
            grads["ev_w_out"][j] = dw_out
            wb_re, wb_im = rec["wb"]
            wc_re, wc_im = rec["wc"]
            dych, du, dd, dwglu = _s5_post_bwd(rec["ych"], rec["u"], s5_d[j][None], full["s5_w_glu"][j], dy_b,
                                               name=f"s5_post_bwd{layer}")
            grads["s5_d"][j], grads["s5_w_glu"][j] = dd[0], dwglu
            h_re2, h_im2 = rec["h_re"].reshape(t, S5_N), rec["h_im"].reshape(t, S5_N)
            dwc_re = _mm(h_re2, dych, "tn", name=f"s5_dwc_re{layer}")
            dwc_im = _mm(h_im2, dych, "tn", name=f"s5_dwc_im{layer}")
            dh_re = _mm(dych, wc_re, "nt", name=f"s5_dh_re{layer}").reshape(bsz, seq, S5_N)
            dh_im = _mm(dych, wc_im, "nt", name=f"s5_dh_im{layer}").reshape(bsz, seq, S5_N)
            g_re, g_im = _s5_scan(dh_re, dh_im, rec["pw_b"], reverse=True, name=f"s5_scan_bwd{layer}")
            da = _s5_dabar(g_re, g_im, rec["h_re"], rec["h_im"], name=f"s5_dabar{layer}").sum(axis=0)
            g_re2, g_im2 = g_re.reshape(t, S5_N), g_im.reshape(t, S5_N)
            dwb_re = _mm(rec["u"], g_re2, "tn", name=f"s5_dwb_re{layer}")
            dwb_im = _mm(rec["u"], g_im2, "tn", name=f"s5_dwb_im{layer}")
            du = _mm(g_re2, wb_re, "nt", res=du, name=f"s5_du_re{layer}")
            du = _mm(g_im2, wb_im, "nt", res=du, name=f"s5_du_im{layer}")
            diag_in = lambda m: jnp.einsum("gpgn->gnp", m.reshape(S5_GROUPS, S5_GROUP, S5_GROUPS, S5_STATE))
            diag_out = lambda m: jnp.einsum("gngp->gpn", m.reshape(S5_GROUPS, S5_STATE, S5_GROUPS, S5_GROUP))
            d_disc = (da[0].reshape(S5_GROUPS, S5_STATE), da[1].reshape(S5_GROUPS, S5_STATE), diag_in(dwb_re),
                      diag_in(dwb_im))
            ga_re, ga_im, gb_re, gb_im, g_dt = rec["disc_vjp"](d_disc)
            grads["s5_a_re"][j], grads["s5_a_im"][j], grads["s5_b_re"][j] = ga_re, ga_im, gb_re
            grads["s5_b_im"][j], grads["s5_log_dt"][j] = gb_im, g_dt
            grads["s5_c_re"][j], grads["s5_c_im"][j] = diag_out(dwc_re), -diag_out(dwc_im)
            dproj_hg, dlb, dnw = _hg_bwd(rec["proj"], lbs[j][None], hg_norm_w[j][None], rec["hg_states"],
                                         dy_a.reshape(bsz, seq, HG_W), name=f"hgrn2_bwd{layer}")
            d_lbs[j] = dlb.sum(axis=0).reshape(HG_W)
            grads["hg_norm_w"][j] = dnw.sum(axis=(0, 1))
            dproj = jnp.concatenate([dproj_hg.reshape(t, 4 * HG_W), du], axis=1)
            w_in = ev_cols(full["ev_w_in"][j])
            dw_in = _mm(h, dproj, "tn", name=f"ev_dwin{layer}")
            grads["ev_w_in"][j] = ev_cols_back(dw_in)
            dh = _mm(dproj, w_in, "nt", name=f"ev_dh{layer}")
        else:
            dy = _mm(dx, full["od_w_out"][j], "nt", name=f"od_dy{layer}")
            grads["od_w_out"][j] = _mm(rec["y"], dx, "tn", name=f"od_dwo{layer}")
            dqkvg, dbd_h, dab, ddt, dnw = _gdn_bwd(rec["qkvg"], rec["bd"], rec["ab"], rec["dtb"],
                                                   gdn_norm_w[j][None], rec["states"],
                                                   dy.reshape(bsz, seq, GDN_W), name=f"gdn_bwd{layer}")
            grads["gdn_norm_w"][j] = dnw.sum(axis=(0, 1))
            grads["gdn_a_log"][j] = dab.sum(axis=(0, 1))[GDN_HEADS:2 * GDN_HEADS]
            grads["gdn_dt_bias"][j] = ddt.sum(axis=(0, 1))[GDN_HEADS:2 * GDN_HEADS]
            dbd = dbd_h.sum(axis=1).reshape(t, HEAD)
            dproj, dconv = _conv_bwd(rec["proj"], rec["conv_w"], dqkvg, name=f"gdn_conv_bwd{layer}")
            dconv = dconv.sum(axis=0)[:CONV_W].reshape(CONV_W, GDN_HEADS, 4, HEAD)[:, :, :3]
            grads["gdn_conv_w"][j] = dconv.transpose(0, 2, 1, 3).reshape(CONV_W, 3 * GDN_W)
            dproj = dproj.reshape(t, 4 * GDN_W)
            dw_main = _by_part(_mm(h, dproj, "tn", name=f"od_dwin{layer}"), 4, GDN_HEADS)
            dw_bd = _mm(h, dbd, "tn", name=f"od_dwbd{layer}")[:, :2 * GDN_HEADS]
            grads["od_w_in"][j] = jnp.concatenate([dw_main, dw_bd], axis=1)
            dh = _mm(dproj, rec["w_main"], "nt", name=f"od_dh{layer}")
            dh = _mm(dbd, rec["w_bd"], "nt", res=dh, name=f"od_dh_bd{layer}")
        dx, dn = _rms_bwd(rec["x_mix"], mix_norm[layer][None], dh, dx, name=f"rms_bwd_mix{layer}")
        grads["mix_norm"][layer] = dn[0]
        dx, dn, dwg, dwu, dwd = _ffn_bwd(rec["ffn1"], ffn1_norm[layer][None], full["ffn1_w_gate"][layer],
                                         full["ffn1_w_up"][layer], full["ffn1_w_down"][layer], dx, f"a{layer}")
        grads["ffn1_norm"][layer], grads["ffn1_w_gate"][layer] = dn[0], dwg
        grads["ffn1_w_up"][layer], grads["ffn1_w_down"][layer] = dwu, dwd
    grad_x = dx.reshape(x.shape)
    (grads["hg_lb_logits"],) = lbs_vjp(jnp.stack(d_lbs))
    for n in WEIGHTS:
        if isinstance(grads[n], list):
            grads[n] = jnp.stack(grads[n])

    g_a = jnp.concatenate([_to_shards(grads[n], 2) for n in WIDE], axis=1)
    g_a = g_a.reshape((N_CHIPS, 2, g_a.shape[1] // 2) + g_a.shape[2:])
    g_b = _rows_pack([_to_shards(grads[n], axis) for n, axis in ROWS], 1)
    sib_a, sib_b = _swap_halves([g_a, g_b], name="reduce_swap_halves")
    wire_a, own_a = _add_sibling(g_a, sib_a, name="reduce_add_sibling_wide")
    wire_b, own_b = _add_sibling(g_b, sib_b, name="reduce_add_sibling_rows")
    oth_a, oth_b = _exchange_chips([wire_a, wire_b], name="reduce_exchange_chips")
    half_a = _add_chips(own_a, oth_a, name="reduce_add_chips_wide")
    half_b = _add_chips(own_b, oth_b, name="reduce_add_chips_rows")
    peer_a, peer_b = _swap_reduced([half_a, half_b], name="reduce_swap_reduced")
    south = lax.axis_index("c") == 0
    both = lambda mine, peer: jnp.where(south, jnp.stack([mine, peer]), jnp.stack([peer, mine]))
    red_a = both(half_a, peer_a).reshape((len(WIDE) * n_wide,) + half_a.shape[1:])
    for k, n in enumerate(WIDE):
        grads[n] = red_a[k * n_wide:(k + 1) * n_wide]
    for (n, _), piece in zip(ROWS, _rows_unpack(both(half_b, peer_b), [wts[n].shape for n, _ in ROWS], 0)):
        grads[n] = piece

    small = jnp.concatenate([grads[n].reshape(-1) for n in SMALL])
    small = jnp.pad(small, (0, SMALL_ROWS * PACK_COLS - small.shape[0])).reshape(SMALL_ROWS, PACK_COLS)
    small = _sum_all(_gather_all(small, name="gather_small_grads"), name="sum_small_grads").reshape(-1)
    off = 0
    for n in SMALL:
        size = int(np.prod(wts[n].shape))
        grads[n] = small[off:off + size].reshape(wts[n].shape)
        off += size

    small_w = jnp.concatenate([wts[n].reshape(-1) for n in SMALL])
    small_m = jnp.concatenate([mom[n].reshape(-1) for n in SMALL])
    small_v = jnp.concatenate([var[n].reshape(-1) for n in SMALL])
    n_small = small_w.shape[0]
    padv = lambda a, fill: jnp.pad(a, (0, SMALL_ROWS * PACK_COLS - n_small), constant_values=fill).reshape(
        SMALL_ROWS, PACK_COLS)
    sd, sm, sv = _adamw(padv(small_w, 0.0), small.reshape(SMALL_ROWS, PACK_COLS), padv(small_m, 0.0),
                        padv(small_v, 1.0), name="adamw_small")
    delta, new_m, new_v = {}, {}, {}
    off = 0
    for n in SMALL:
        size = int(np.prod(wts[n].shape))
        delta[n] = sd.reshape(-1)[off:off + size].reshape(wts[n].shape)
        new_m[n] = sm.reshape(-1)[off:off + size].reshape(wts[n].shape)
        new_v[n] = sv.reshape(-1)[off:off + size].reshape(wts[n].shape)
        off += size
    for n, _ in SHARDED:
        delta[n], new_m[n], new_v[n] = _adamw(wts[n], grads[n], mom[n], var[n], name=f"adamw_{n}")

    return (loss, grad_x, *[grads[n] for n in WEIGHTS], *[delta[n] for n in WEIGHTS],
            *[new_m[n] for n in WEIGHTS], *[new_v[n] for n in WEIGHTS])
```

```python
import functools
import math

import jax
import jax.numpy as jnp
import numpy as np
from jax import lax
from jax.experimental import pallas as pl
from jax.experimental.pallas import tpu as pltpu

F32 = jnp.float32
BF16 = jnp.bfloat16
LHS_EXACT = 2
MESH_ID = pl.DeviceIdType.MESH

D_MODEL = 1024
D_FF = 2816
DEPTH = 4
NORM_EPS = 1e-6
F_MIN = 1e-6
CHUNK = 64
HEAD = 128
HG_HEADS = 4
HG_W = 512
S5_W = 512
S5_GROUP = 16
S5_GROUPS = 32
S5_STATE = 64
S5_N = S5_GROUPS * S5_STATE
GDN_HEADS = 8
GDN_W = 1024
CONV_W = 4
N_CHIPS = 4
N_DEV = 8
LANE = 128
SUBLANE = 8
VMEM_LIMIT = 56 * 1024 * 1024
PACK_COLS = 1024
ROW_BLOCK = 512
ROW_BLOCKS = 10
SMALL_ROWS = 288

ADAM_LR = 0.001
ADAM_B1 = 0.9
ADAM_B2 = 0.999
ADAM_EPS = 1e-08
ADAM_WD = 0.01
ADAM_STEP = 10

SHARDED = (("ffn1_w_gate", 2), ("ffn1_w_up", 2), ("ffn1_w_down", 1), ("ffn2_w_gate", 2), ("ffn2_w_up", 2),
           ("ffn2_w_down", 1), ("ev_w_in", 2), ("s5_w_glu", 1), ("ev_w_out", 1), ("od_w_in", 2),
           ("gdn_conv_w", 2), ("od_w_out", 1))
WIDE = ("ffn1_w_gate", "ffn1_w_up", "ffn2_w_gate", "ffn2_w_up")
ROWS = (("ffn1_w_down", 1), ("ffn2_w_down", 1), ("ev_w_out", 1), ("od_w_out", 1), ("s5_w_glu", 1), ("ev_w_in", 2),
        ("od_w_in", 2), ("gdn_conv_w", 2))
SMALL = ("ffn1_norm", "mix_norm", "ffn2_norm", "hg_lb_logits", "hg_norm_w", "s5_a_re", "s5_a_im", "s5_b_re",
         "s5_b_im", "s5_c_re", "s5_c_im", "s5_d", "s5_log_dt", "gdn_a_log", "gdn_dt_bias", "gdn_norm_w",
         "final_norm")
WEIGHTS = ("ffn1_norm", "ffn1_w_gate", "ffn1_w_up", "ffn1_w_down", "mix_norm", "ffn2_norm", "ffn2_w_gate",
           "ffn2_w_up", "ffn2_w_down", "ev_w_in", "hg_lb_logits", "hg_norm_w", "s5_a_re", "s5_a_im", "s5_b_re",
           "s5_b_im", "s5_c_re", "s5_c_im", "s5_d", "s5_log_dt", "s5_w_glu", "ev_w_out", "od_w_in", "gdn_conv_w",
           "gdn_a_log", "gdn_dt_bias", "gdn_norm_w", "od_w_out", "final_norm")


def _cparams(sem=None):
    return pltpu.CompilerParams(dimension_semantics=sem, vmem_limit_bytes=VMEM_LIMIT)


def _dg(a, b, ca, cb, hi):
    if a.ndim == 3 or b.ndim == 3:
        n = a.shape[0] if a.ndim == 3 else b.shape[0]
        a = a if a.ndim == 3 else jnp.broadcast_to(a, (n,) + a.shape)
        b = b if b.ndim == 3 else jnp.broadcast_to(b, (n,) + b.shape)
        dims = (((ca + 1,), (cb + 1,)), ((0,), (0,)))
    else:
        dims = (((ca,), (cb,)), ((), ()))
    dot = lambda p, q: lax.dot_general(p, q, dims, preferred_element_type=F32)
    if hi:
        a_hi, b_hi = a.astype(BF16), b.astype(BF16)
        b_lo = (b - b_hi.astype(F32)).astype(BF16)
        if hi == LHS_EXACT:
            b_lo2 = (b - b_hi.astype(F32) - b_lo.astype(F32)).astype(BF16)
            return dot(a_hi, b_hi) + (dot(a_hi, b_lo) + dot(a_hi, b_lo2))
        a_lo = (a - a_hi.astype(F32)).astype(BF16)
        return dot(a_hi, b_hi) + (dot(a_hi, b_lo) + dot(a_lo, b_hi))
    return dot(a.astype(BF16), b.astype(BF16))


@functools.partial(jax.custom_vjp, nondiff_argnums=(2,))
def mm_nn(a, b, hi=False):
    return _dg(a, b, 1, 0, hi)


@functools.partial(jax.custom_vjp, nondiff_argnums=(2,))
def mm_nt(a, b, hi=False):
    return _dg(a, b, 1, 1, hi)


@functools.partial(jax.custom_vjp, nondiff_argnums=(2,))
def mm_tn(a, b, hi=False):
    return _dg(a, b, 0, 0, hi)


mm_nn.defvjp(lambda a, b, hi: (_dg(a, b, 1, 0, hi), (a, b)),
             lambda hi, r, g: (mm_nt(g, r[1], hi), mm_tn(r[0], g, hi)))
mm_nt.defvjp(lambda a, b, hi: (_dg(a, b, 1, 1, hi), (a, b)),
             lambda hi, r, g: (mm_nn(g, r[1], hi), mm_tn(g, r[0], hi)))
mm_tn.defvjp(lambda a, b, hi: (_dg(a, b, 0, 0, hi), (a, b)),
             lambda hi, r, g: (mm_nt(r[1], g, hi), mm_nn(r[0], g, hi)))


def _sigmoid(x):
    return 1.0 / (1.0 + jnp.exp(-x))


def _silu(x):
    return x * _sigmoid(x)


def _softplus(x):
    return jnp.maximum(x, 0.0) + jnp.log(1.0 + jnp.exp(-jnp.abs(x)))


def _gelu(x):
    return 0.5 * x * (1.0 + jnp.tanh(math.sqrt(2.0 / math.pi) * (x + 0.044715 * (x * x * x))))


def _iota(shape, dim):
    return lax.broadcasted_iota(jnp.int32, shape, dim)


def _l2norm(t):
    return t * lax.rsqrt(jnp.sum(t * t, axis=-1, keepdims=True) + NORM_EPS)


def gated_head_norm(o, gate, nw):
    o = o * lax.rsqrt(jnp.mean(o * o, axis=-1, keepdims=True) + NORM_EPS) * nw
    return o * _silu(gate)


HG_SUB = 8
HG_MID = 3


def hg_chunk(st, ql, fl, v, gl, lb, nw):
    c = CHUNK
    n = st.shape[0]
    q = _silu(ql)
    f = lb + (1.0 - lb) * _sigmoid(fl)
    lf = jnp.log(jnp.maximum(f, F_MIN))
    k = 1.0 - f
    t_i = _iota((2 * c, c), 0)
    s_i = _iota((2 * c, c), 1)
    lim = jnp.where(t_i < c, t_i, ((t_i - c) // HG_SUB) * HG_SUB + HG_MID)
    sums = jnp.broadcast_to(jnp.where(s_i <= lim, 1.0, 0.0).astype(F32), (n, 2 * c, c))
    cum = mm_nn(sums, lf, LHS_EXACT)
    b, rb = cum[:, :c], cum[:, c:]
    qp = q * jnp.exp(b - rb)
    row = _iota((1, c, HEAD), 1)
    blocks = []
    for i in range(c // HG_SUB):
        r_i = jnp.tile(rb[:, i * HG_SUB:(i + 1) * HG_SUB], (1, c // HG_SUB, 1))
        seen = row < (i + 1) * HG_SUB
        kp = k * jnp.exp(jnp.where(seen, r_i - b, 0.0))
        blocks.append(mm_nt(qp[:, i * HG_SUB:(i + 1) * HG_SUB], kp))
    a = jnp.concatenate(blocks, axis=1)
    a = jnp.where(_iota((1, c, c), 2) <= _iota((1, c, c), 1), a, 0.0)
    o = mm_nn(a, v) + mm_nt(q * jnp.exp(b), st)
    bl = jnp.sum(lf, axis=1, keepdims=True)
    st_new = st * jnp.exp(bl) + mm_tn(v, k * jnp.exp(bl - b))
    return st_new, gated_head_norm(o, gl, nw)


def gdn_chunk(st, qc, kc, vc, gate, bd, ab, dtb, nw, mb, ma):
    c = CHUNK
    n = st.shape[0]
    beta = _sigmoid(jnp.sum(bd * mb, axis=-1, keepdims=True))
    a_lin = jnp.sum(bd * ma, axis=-1, keepdims=True)
    a_log = jnp.sum(ab * ma, axis=-1, keepdims=True)
    dt_b = jnp.sum(dtb * ma, axis=-1, keepdims=True)
    la = -jnp.exp(a_log) * _softplus(a_lin + dt_b)
    la_b = jnp.broadcast_to(la, (n, c, HEAD))
    t_i = _iota((1, c, c), 1)
    s_i = _iota((1, c, c), 2)
    lower = s_i <= t_i
    strict = s_i < t_i
    tri = jnp.broadcast_to(jnp.where(lower, 1.0, 0.0).astype(F32), (n, c, c))
    g = mm_nn(tri, la_b, LHS_EXACT)
    d = mm_nn(tri, jnp.where(strict, la_b[:, :, :c], 0.0), LHS_EXACT)
    lm = jnp.where(lower, jnp.exp(jnp.where(lower, d, 0.0)), 0.0)
    q = _l2norm(qc) * (HEAD ** -0.5)
    k = _l2norm(kc)
    kb = k * beta
    vb = vc * beta
    m = jnp.where(strict, mm_nt(kb, k) * lm, 0.0)
    eye = jnp.where(s_i == t_i, 1.0, 0.0).astype(F32)
    t_inv = eye - m
    p = m
    for _ in range(int(math.log2(c)) - 1):
        p = mm_nn(p, p, True)
        t_inv = t_inv + mm_nn(t_inv, p, True)
    u = mm_nn(t_inv, vb)
    w = mm_nn(t_inv, kb * jnp.exp(g))
    attn = mm_nt(q, k) * lm
    v_new = u - mm_nt(w, st)
    o = mm_nt(q * jnp.exp(g), st) + mm_nn(attn, v_new)
    g_last = jnp.sum(la_b, axis=1, keepdims=True)
    st_new = st * jnp.exp(g_last) + mm_tn(v_new, k * jnp.exp(g_last - g))
    return st_new, gated_head_norm(o, gate, nw)


def s5_post(ych, u, d, wglu):
    y = _gelu(ych + d * u)
    return y * _sigmoid(mm_nn(y, wglu))


def _pick(n, cands):
    for c in cands:
        if n % c == 0:
            return c
    return n


def _mm(a, b, mode, *, name, out_dtype=F32, res=None, scale=1.0, tm=None, tn=None, tk=None):
    if mode == "nn":
        (m, k), (k2, n) = a.shape, b.shape
    elif mode == "nt":
        (m, k), (n, k2) = a.shape, b.shape
    else:
        (k, m), (k2, n) = a.shape, b.shape
    assert k == k2, (a.shape, b.shape, mode)
    tm = tm or _pick(m, (512, 1408, 256, 128) if mode == "tn" else (512, 704, 256, 128))
    tn = tn or _pick(n, (1024, 1408, 512, 256, 128))
    tk = tk or _pick(k, (1024, 1408, 512, 256, 128))
    nk = k // tk
    grid = (n // tn, m // tm, nk)
    if mode == "tn":
        a_spec = pl.BlockSpec((tk, tm), lambda j, i, kk: (kk, i))
    else:
        a_spec = pl.BlockSpec((tm, tk), lambda j, i, kk: (i, kk))
    if mode == "nt":
        b_spec = pl.BlockSpec((tn, tk), lambda j, i, kk: (j, kk))
    else:
        b_spec = pl.BlockSpec((tk, tn), lambda j, i, kk: (kk, j))
    o_spec = pl.BlockSpec((tm, tn), lambda j, i, kk: (i, j))
    ca = 0 if mode == "tn" else 1
    cb = 1 if mode == "nt" else 0
    has_res = res is not None

    def body(*refs):
        if has_res:
            a_ref, b_ref, r_ref, o_ref, acc = refs
        else:
            a_ref, b_ref, o_ref, acc = refs
        kk = pl.program_id(2)

        @pl.when(kk == 0)
        def _():
            acc[...] = jnp.zeros_like(acc)

        acc[...] += _dg(a_ref[...], b_ref[...], ca, cb, False)

        @pl.when(kk == nk - 1)
        def _():
            r = acc[...] * scale if scale != 1.0 else acc[...]
            if has_res:
                r = r_ref[...] + r
            o_ref[...] = r.astype(out_dtype)

    in_specs = [a_spec, b_spec] + ([o_spec] if has_res else [])
    args = (a, b) + ((res,) if has_res else ())
    return pl.pallas_call(
        body, out_shape=jax.ShapeDtypeStruct((m, n), out_dtype), grid=grid, in_specs=in_specs, out_specs=o_spec,
        scratch_shapes=[pltpu.VMEM((tm, tn), F32)], name=name,
        compiler_params=_cparams(("parallel", "parallel", "arbitrary")))(*args)


def _rms_fwd(x, w, *, name):
    t, d = x.shape
    tm = _pick(t, (512, 256, 128))

    def body(x_ref, w_ref, o_ref):
        xv = x_ref[...]
        r = lax.rsqrt(jnp.mean(xv * xv, axis=-1, keepdims=True) + NORM_EPS)
        o_ref[...] = (xv * r * w_ref[...]).astype(BF16)

    return pl.pallas_call(
        body, out_shape=jax.ShapeDtypeStruct((t, d), BF16), grid=(t // tm,),
        in_specs=[pl.BlockSpec((tm, d), lambda i: (i, 0)), pl.BlockSpec((1, d), lambda i: (0, 0))],
        out_specs=pl.BlockSpec((tm, d), lambda i: (i, 0)), name=name, compiler_params=_cparams(("parallel",)))(x, w)


def _rms_bwd(x, w, dh, dres, *, name):
    t, d = x.shape
    tm = _pick(t, (512, 256, 128))

    def body(x_ref, w_ref, dh_ref, dr_ref, dx_ref, dw_ref):
        xv = x_ref[...]
        r = lax.rsqrt(jnp.mean(xv * xv, axis=-1, keepdims=True) + NORM_EPS)
        xh = xv * r
        dhv = dh_ref[...]
        dxh = dhv * w_ref[...]
        dx_ref[...] = dr_ref[...] + r * (dxh - xh * jnp.mean(dxh * xh, axis=-1, keepdims=True))

        @pl.when(pl.program_id(0) == 0)
        def _():
            dw_ref[...] = jnp.zeros_like(dw_ref)

        dw_ref[...] += jnp.sum(dhv * xh, axis=0, keepdims=True)

    row = pl.BlockSpec((tm, d), lambda i: (i, 0))
    vec = pl.BlockSpec((1, d), lambda i: (0, 0))
    return pl.pallas_call(
        body, out_shape=[jax.ShapeDtypeStruct((t, d), F32), jax.ShapeDtypeStruct((1, d), F32)], grid=(t // tm,),
        in_specs=[row, vec, row, row], out_specs=[row, vec], name=name,
        compiler_params=_cparams(("arbitrary",)))(x, w, dh, dres)


def _final_loss(x, w, target, *, name):
    t, d = x.shape
    tm = _pick(t, (512, 256, 128))

    def body(x_ref, w_ref, t_ref, l_ref, dx_ref, dw_ref):
        xv = x_ref[...]
        wv = w_ref[...]
        r = lax.rsqrt(jnp.mean(xv * xv, axis=-1, keepdims=True) + NORM_EPS)
        xh = xv * r
        err = xh * wv - t_ref[...]
        dy = err * (1.0 / d)
        dxh = dy * wv
        dx_ref[...] = r * (dxh - xh * jnp.mean(dxh * xh, axis=-1, keepdims=True))

        @pl.when(pl.program_id(0) == 0)
        def _():
            dw_ref[...] = jnp.zeros_like(dw_ref)
            l_ref[...] = jnp.zeros_like(l_ref)

        dw_ref[...] += jnp.sum(dy * xh, axis=0, keepdims=True)
        l_ref[...] += 0.5 * jnp.sum(jnp.mean(err * err, axis=-1, keepdims=True), axis=0, keepdims=True)

    row = pl.BlockSpec((tm, d), lambda i: (i, 0))
    vec = pl.BlockSpec((1, d), lambda i: (0, 0))
    return pl.pallas_call(
        body, out_shape=[jax.ShapeDtypeStruct((SUBLANE, LANE), F32), jax.ShapeDtypeStruct((t, d), F32),
                         jax.ShapeDtypeStruct((1, d), F32)], grid=(t // tm,),
        in_specs=[row, vec, row], out_specs=[pl.BlockSpec((SUBLANE, LANE), lambda i: (0, 0)), row, vec], name=name,
        compiler_params=_cparams(("arbitrary",)))(x, w, target)


def _ffn_up(h, wg, wu, *, name):
    t, d = h.shape
    f = wg.shape[1]
    tm = _pick(t, (512, 256, 128))
    tn = _pick(f, (1408, 512, 256, 128))

    def body(h_ref, wg_ref, wu_ref, g_ref, u_ref, a_ref):
        hv = h_ref[...]
        g = _dg(hv, wg_ref[...], 1, 0, False)
        u = _dg(hv, wu_ref[...], 1, 0, False)
        g_ref[...] = g
        u_ref[...] = u
        a_ref[...] = (_silu(g) * u).astype(BF16)

    hs = pl.BlockSpec((tm, d), lambda j, i: (i, 0))
    ws = pl.BlockSpec((d, tn), lambda j, i: (0, j))
    os_ = pl.BlockSpec((tm, tn), lambda j, i: (i, j))
    return pl.pallas_call(
        body, out_shape=[jax.ShapeDtypeStruct((t, f), F32), jax.ShapeDtypeStruct((t, f), F32),
                         jax.ShapeDtypeStruct((t, f), BF16)], grid=(f // tn, t // tm),
        in_specs=[hs, ws, ws], out_specs=[os_, os_, os_], name=name,
        compiler_params=_cparams(("parallel", "parallel")))(h, wg, wu)


def _ffn_dact(dxo, wd, g, u, *, name):
    t, d = dxo.shape
    f = wd.shape[0]
    tm = _pick(t, (512, 256, 128))
    tn = _pick(f, (1408, 512, 256, 128))

    def body(dx_ref, wd_ref, g_ref, u_ref, dg_ref, du_ref):
        da = 0.5 * _dg(dx_ref[...], wd_ref[...], 1, 1, False)
        gv = g_ref[...]
        sg = _sigmoid(gv)
        dg_ref[...] = (da * u_ref[...] * (sg * (1.0 + gv * (1.0 - sg)))).astype(BF16)
        du_ref[...] = (da * (gv * sg)).astype(BF16)

    xs = pl.BlockSpec((tm, d), lambda j, i: (i, 0))
    ws = pl.BlockSpec((tn, d), lambda j, i: (j, 0))
    os_ = pl.BlockSpec((tm, tn), lambda j, i: (i, j))
    return pl.pallas_call(
        body, out_shape=[jax.ShapeDtypeStruct((t, f), BF16), jax.ShapeDtypeStruct((t, f), BF16)],
        grid=(f // tn, t // tm), in_specs=[xs, ws, os_, os_], out_specs=[os_, os_], name=name,
        compiler_params=_cparams(("parallel", "parallel")))(dxo, wd, g, u)


def _ffn_fwd(x, nw, wg, wu, wd, tag):
    h = _rms_fwd(x, nw, name=f"rms_fwd_{tag}")
    g, u, a = _ffn_up(h, wg, wu, name=f"ffn_up_{tag}")
    y = _mm(a, wd, "nn", res=x, scale=0.5, name=f"ffn_down_{tag}")
    return y, (x, h, g, u, a)


def _ffn_bwd(saved, nw, wg, wu, wd, dxo, tag):
    x, h, g, u, a = saved
    dg, du = _ffn_dact(dxo, wd, g, u, name=f"ffn_dact_{tag}")
    dwd = _mm(a, dxo, "tn", scale=0.5, name=f"ffn_dwd_{tag}")
    dwg = _mm(h, dg, "tn", name=f"ffn_dwg_{tag}")
    dwu = _mm(h, du, "tn", name=f"ffn_dwu_{tag}")
    dh = _mm(dg, wg, "nt", name=f"ffn_dh1_{tag}")
    dh = _mm(du, wu, "nt", res=dh, name=f"ffn_dh2_{tag}")
    dx, dnw = _rms_bwd(x, nw, dh, dxo, name=f"rms_bwd_{tag}")
    return dx, dnw, dwg, dwu, dwd


MIX_HP = 4


def _pairs(ref, bsz, part):
    return jnp.stack([ref[b, :, 512 * hh + HEAD * part:512 * hh + HEAD * (part + 1)]
                      for b in range(bsz) for hh in range(MIX_HP)])


def _hg_fwd(proj, lbs, nw, *, name):
    bsz, l, _ = proj.shape
    nc = l // CHUNK
    groups = HG_HEADS // MIX_HP

    def body(p_ref, lb_ref, nw_ref, y_ref, s_ref, st):
        @pl.when(pl.program_id(1) == 0)
        def _():
            st[...] = jnp.zeros_like(st)

        prev = st[...]
        lb = jnp.stack([lb_ref[:, HEAD * hh:HEAD * (hh + 1)] for _ in range(bsz) for hh in range(MIX_HP)])
        st_new, y = hg_chunk(prev, *[_pairs(p_ref, bsz, part) for part in range(4)], lb, nw_ref[...])
        s_ref[...] = prev.reshape(bsz, MIX_HP, HEAD, HEAD)
        st[...] = st_new
        for b in range(bsz):
            for hh in range(MIX_HP):
                y_ref[b, :, HEAD * hh:HEAD * (hh + 1)] = y[b * MIX_HP + hh]

    return pl.pallas_call(
        body,
        out_shape=[jax.ShapeDtypeStruct((bsz, l, HG_W), F32),
                   jax.ShapeDtypeStruct((bsz, HG_HEADS, nc, HEAD, HEAD), F32)],
        grid=(groups, nc),
        in_specs=[pl.BlockSpec((bsz, CHUNK, MIX_HP * 512), lambda g, c: (0, c, g)),
                  pl.BlockSpec((1, MIX_HP * HEAD), lambda g, c: (0, g)),
                  pl.BlockSpec((1, HEAD), lambda g, c: (0, 0))],
        out_specs=[pl.BlockSpec((bsz, CHUNK, MIX_HP * HEAD), lambda g, c: (0, c, g)),
                   pl.BlockSpec((bsz, MIX_HP, None, HEAD, HEAD), lambda g, c: (0, g, c, 0, 0))],
        scratch_shapes=[pltpu.VMEM((bsz * MIX_HP, HEAD, HEAD), F32)], name=name,
        compiler_params=_cparams(("parallel", "arbitrary")))(proj, lbs, nw)


def _hg_bwd(proj, lbs, nw, states, dy, *, name):
    bsz, l, _ = proj.shape
    nc = l // CHUNK
    groups = HG_HEADS // MIX_HP
    n = bsz * MIX_HP

    def body(p_ref, lb_ref, nw_ref, s_ref, dy_ref, dp_ref, dlb_ref, dnw_ref, dst):
        @pl.when(pl.program_id(1) == 0)
        def _():
            dst[...] = jnp.zeros_like(dst)
            dlb_ref[...] = jnp.zeros_like(dlb_ref)
            dnw_ref[...] = jnp.zeros_like(dnw_ref)

        lb = jnp.stack([lb_ref[:, HEAD * hh:HEAD * (hh + 1)] for _ in range(bsz) for hh in range(MIX_HP)])
        dy3 = jnp.stack([dy_ref[b, :, HEAD * hh:HEAD * (hh + 1)] for b in range(bsz) for hh in range(MIX_HP)])
        _, vjp = jax.vjp(hg_chunk, s_ref[...].reshape(n, HEAD, HEAD), *[_pairs(p_ref, bsz, part) for part in range(4)],
                         lb, nw_ref[...])
        d_st, dq, df, dv, dgl, dlb, dnw = vjp((dst[...], dy3))
        dst[...] = d_st
        for b in range(bsz):
            for hh in range(MIX_HP):
                for part, d in enumerate((dq, df, dv, dgl)):
                    dp_ref[b, :, 512 * hh + HEAD * part:512 * hh + HEAD * (part + 1)] = d[b * MIX_HP + hh]
        dlb_ref[...] += dlb.reshape(bsz, MIX_HP, 1, HEAD)
        dnw_ref[...] += dnw

    rev = lambda c: nc - 1 - c
    return pl.pallas_call(
        body,
        out_shape=[jax.ShapeDtypeStruct((bsz, l, 4 * HG_W), F32),
                   jax.ShapeDtypeStruct((bsz, HG_HEADS, 1, HEAD), F32),
                   jax.ShapeDtypeStruct((groups, 1, HEAD), F32)],
        grid=(groups, nc),
        in_specs=[pl.BlockSpec((bsz, CHUNK, MIX_HP * 512), lambda g, c: (0, rev(c), g)),
                  pl.BlockSpec((1, MIX_HP * HEAD), lambda g, c: (0, g)),
                  pl.BlockSpec((1, HEAD), lambda g, c: (0, 0)),
                  pl.BlockSpec((bsz, MIX_HP, None, HEAD, HEAD), lambda g, c: (0, g, rev(c), 0, 0)),
                  pl.BlockSpec((bsz, CHUNK, MIX_HP * HEAD), lambda g, c: (0, rev(c), g))],
        out_specs=[pl.BlockSpec((bsz, CHUNK, MIX_HP * 512), lambda g, c: (0, rev(c), g)),
                   pl.BlockSpec((bsz, MIX_HP, 1, HEAD), lambda g, c: (0, g, 0, 0)),
                   pl.BlockSpec((None, 1, HEAD), lambda g, c: (g, 0, 0))],
        scratch_shapes=[pltpu.VMEM((n, HEAD, HEAD), F32)], name=name,
        compiler_params=_cparams(("parallel", "arbitrary")))(proj, lbs, nw, states, dy)


def _head_masks(group, bsz):
    n = bsz * MIX_HP
    head = group * MIX_HP + _iota((n, 1, HEAD), 0) % MIX_HP
    lane = _iota((n, 1, HEAD), 2)
    return jnp.where(lane == head, 1.0, 0.0).astype(F32), jnp.where(lane == GDN_HEADS + head, 1.0, 0.0).astype(F32)


def _gdn_fwd(qkvg, bd, ab, dtb, nw, *, name):
    bsz, l, _ = qkvg.shape
    nc = l // CHUNK
    groups = GDN_HEADS // MIX_HP

    def body(p_ref, bd_ref, ab_ref, dtb_ref, nw_ref, y_ref, s_ref, st):
        @pl.when(pl.program_id(1) == 0)
        def _():
            st[...] = jnp.zeros_like(st)

        mb, ma = _head_masks(pl.program_id(0), bsz)
        prev = st[...]
        bd3 = jnp.stack([bd_ref[b] for b in range(bsz) for _ in range(MIX_HP)])
        st_new, y = gdn_chunk(prev, *[_pairs(p_ref, bsz, part) for part in range(4)], bd3, ab_ref[...],
                              dtb_ref[...], nw_ref[...], mb, ma)
        s_ref[...] = prev.reshape(bsz, MIX_HP, HEAD, HEAD)
        st[...] = st_new
        for b in range(bsz):
            for hh in range(MIX_HP):
                y_ref[b, :, HEAD * hh:HEAD * (hh + 1)] = y[b * MIX_HP + hh]

    vec = pl.BlockSpec((1, HEAD), lambda g, c: (0, 0))
    return pl.pallas_call(
        body,
        out_shape=[jax.ShapeDtypeStruct((bsz, l, GDN_W), F32),
                   jax.ShapeDtypeStruct((bsz, GDN_HEADS, nc, HEAD, HEAD), F32)],
        grid=(groups, nc),
        in_specs=[pl.BlockSpec((bsz, CHUNK, MIX_HP * 512), lambda g, c: (0, c, g)),
                  pl.BlockSpec((bsz, CHUNK, HEAD), lambda g, c: (0, c, 0)), vec, vec, vec],
        out_specs=[pl.BlockSpec((bsz, CHUNK, MIX_HP * HEAD), lambda g, c: (0, c, g)),
                   pl.BlockSpec((bsz, MIX_HP, None, HEAD, HEAD), lambda g, c: (0, g, c, 0, 0))],
        scratch_shapes=[pltpu.VMEM((bsz * MIX_HP, HEAD, HEAD), F32)], name=name,
        compiler_params=_cparams(("parallel", "arbitrary")))(qkvg, bd, ab, dtb, nw)


def _gdn_bwd(qkvg, bd, ab, dtb, nw, states, dy, *, name):
    bsz, l, _ = qkvg.shape
    nc = l // CHUNK
    groups = GDN_HEADS // MIX_HP
    n = bsz * MIX_HP

    def body(p_ref, bd_ref, ab_ref, dtb_ref, nw_ref, s_ref, dy_ref, dp_ref, dbd_ref, dab_ref, ddt_ref, dnw_ref,
             dst):
        @pl.when(pl.program_id(1) == 0)
        def _():
            dst[...] = jnp.zeros_like(dst)
            dab_ref[...] = jnp.zeros_like(dab_ref)
            ddt_ref[...] = jnp.zeros_like(ddt_ref)
            dnw_ref[...] = jnp.zeros_like(dnw_ref)

        mb, ma = _head_masks(pl.program_id(0), bsz)
        fn = lambda st, q, k, v, gt, bdv, abv, dtv, nwv: gdn_chunk(st, q, k, v, gt, bdv, abv, dtv, nwv, mb, ma)
        bd3 = jnp.stack([bd_ref[b] for b in range(bsz) for _ in range(MIX_HP)])
        dy3 = jnp.stack([dy_ref[b, :, HEAD * hh:HEAD * (hh + 1)] for b in range(bsz) for hh in range(MIX_HP)])
        _, vjp = jax.vjp(fn, s_ref[...].reshape(n, HEAD, HEAD), *[_pairs(p_ref, bsz, part) for part in range(4)], bd3,
                         ab_ref[...], dtb_ref[...], nw_ref[...])
        d_st, dq, dk, dv, dgt, dbd, dab, ddt, dnw = vjp((dst[...], dy3))
        dst[...] = d_st
        for b in range(bsz):
            for hh in range(MIX_HP):
                for part, d in enumerate((dq, dk, dv, dgt)):
                    dp_ref[b, :, 512 * hh + HEAD * part:512 * hh + HEAD * (part + 1)] = d[b * MIX_HP + hh]
        dbd_ref[...] = jnp.sum(dbd.reshape(bsz, MIX_HP, CHUNK, HEAD), axis=1)
        dab_ref[...] += dab
        ddt_ref[...] += ddt
        dnw_ref[...] += dnw

    rev = lambda c: nc - 1 - c
    vec = pl.BlockSpec((1, HEAD), lambda g, c: (0, 0))
    acc = pl.BlockSpec((None, 1, HEAD), lambda g, c: (g, 0, 0))
    acc_shape = jax.ShapeDtypeStruct((groups, 1, HEAD), F32)
    return pl.pallas_call(
        body,
        out_shape=[jax.ShapeDtypeStruct((bsz, l, 4 * GDN_W), F32),
                   jax.ShapeDtypeStruct((bsz, groups, l, HEAD), F32), acc_shape, acc_shape, acc_shape],
        grid=(groups, nc),
        in_specs=[pl.BlockSpec((bsz, CHUNK, MIX_HP * 512), lambda g, c: (0, rev(c), g)),
                  pl.BlockSpec((bsz, CHUNK, HEAD), lambda g, c: (0, rev(c), 0)), vec, vec, vec,
                  pl.BlockSpec((bsz, MIX_HP, None, HEAD, HEAD), lambda g, c: (0, g, rev(c), 0, 0)),
                  pl.BlockSpec((bsz, CHUNK, MIX_HP * HEAD), lambda g, c: (0, rev(c), g))],
        out_specs=[pl.BlockSpec((bsz, CHUNK, MIX_HP * 512), lambda g, c: (0, rev(c), g)),
                   pl.BlockSpec((bsz, None, CHUNK, HEAD), lambda g, c: (0, g, rev(c), 0)), acc, acc, acc],
        scratch_shapes=[pltpu.VMEM((n, HEAD, HEAD), F32)], name=name,
        compiler_params=_cparams(("parallel", "arbitrary")))(qkvg, bd, ab, dtb, nw, states, dy)


CONV_TL = 256
QKV_LANES = 3 * HEAD


def _conv_fwd(x, w, *, name):
    bsz, l, wd = x.shape
    tl = min(CONV_TL, l)
    nr = l // tl
    hb = tl // SUBLANE

    def body(x_ref, xp_ref, w_ref, o_ref):
        xv = x_ref[...]
        halo = jnp.where(pl.program_id(2) > 0, xp_ref[...], 0.0)
        xc = jnp.concatenate([halo, xv], axis=0)
        wv = w_ref[...]
        z = wv[CONV_W - 1:CONV_W] * xv
        for j in range(CONV_W - 1):
            z = z + wv[j:j + 1] * pltpu.roll(xc, CONV_W - 1 - j, 0)[SUBLANE:]
        o_ref[...] = jnp.where(_iota((tl, 512), 1) < QKV_LANES, _silu(z), xv)

    return pl.pallas_call(
        body, out_shape=jax.ShapeDtypeStruct(x.shape, F32), grid=(bsz, wd // 512, nr),
        in_specs=[pl.BlockSpec((None, tl, 512), lambda b, h, r: (b, r, h)),
                  pl.BlockSpec((None, SUBLANE, 512), lambda b, h, r: (b, jnp.maximum(r * hb - 1, 0), h)),
                  pl.BlockSpec((SUBLANE, 512), lambda b, h, r: (0, h))],
        out_specs=pl.BlockSpec((None, tl, 512), lambda b, h, r: (b, r, h)), name=name,
        compiler_params=_cparams(("parallel", "parallel", "arbitrary")))(x, x, w)


def _conv_bwd(x, w, dy, *, name):
    bsz, l, wd = x.shape
    tl = min(CONV_TL, l)
    nr = l // tl
    hb = tl // SUBLANE

    def body(x_ref, xp_ref, xn_ref, w_ref, dy_ref, dyn_ref, dx_ref, dw_ref):
        r = pl.program_id(2)
        xv = x_ref[...]
        prev = jnp.where(r > 0, xp_ref[...], 0.0)
        last = r == nr - 1
        nxt = jnp.where(last, 0.0, xn_ref[...])
        dyn = jnp.where(last, 0.0, dyn_ref[...])
        xc = jnp.concatenate([prev, xv, nxt], axis=0)
        wv = w_ref[...]
        shifted = [pltpu.roll(xc, CONV_W - 1 - j, 0) for j in range(CONV_W - 1)] + [xc]
        z = wv[0:1] * shifted[0]
        for j in range(1, CONV_W):
            z = z + wv[j:j + 1] * shifted[j]
        z = z[SUBLANE:]
        sg = _sigmoid(z)
        dyc = jnp.concatenate([dy_ref[...], dyn], axis=0)
        dz = dyc * (sg * (1.0 + z * (1.0 - sg)))
        dx = wv[CONV_W - 1:CONV_W] * dz[:tl]
        for j in range(CONV_W - 1):
            s = CONV_W - 1 - j
            dx = dx + wv[j:j + 1] * pltpu.roll(dz, tl + SUBLANE - s, 0)[:tl]
        conv_lane = _iota((tl, 512), 1) < QKV_LANES
        dx_ref[...] = jnp.where(conv_lane, dx, dy_ref[...])

        @pl.when(r == 0)
        def _():
            dw_ref[...] = jnp.zeros_like(dw_ref)

        dzt = dz[:tl]
        rows = [jnp.sum(dzt * shifted[j][SUBLANE:SUBLANE + tl], axis=0, keepdims=True) for j in range(CONV_W)]
        rows.append(jnp.zeros((SUBLANE - CONV_W, 512), F32))
        dw_ref[...] += jnp.concatenate(rows, axis=0)

    blk = pl.BlockSpec((None, tl, 512), lambda b, h, r: (b, r, h))
    prev = pl.BlockSpec((None, SUBLANE, 512), lambda b, h, r: (b, jnp.maximum(r * hb - 1, 0), h))
    nxt = pl.BlockSpec((None, SUBLANE, 512), lambda b, h, r: (b, jnp.minimum((r + 1) * hb, l // SUBLANE - 1), h))
    return pl.pallas_call(
        body, out_shape=[jax.ShapeDtypeStruct(x.shape, F32), jax.ShapeDtypeStruct((bsz, SUBLANE, wd), F32)],
        grid=(bsz, wd // 512, nr),
        in_specs=[blk, prev, nxt, pl.BlockSpec((SUBLANE, 512), lambda b, h, r: (0, h)), blk, nxt],
        out_specs=[blk, pl.BlockSpec((None, SUBLANE, 512), lambda b, h, r: (b, 0, h))], name=name,
        compiler_params=_cparams(("parallel", "parallel", "arbitrary")))(x, x, x, w, dy, dy)


S5_LC = 256
S5_TN = 512


def _s5_scan(xr, xi, pw, *, reverse, name):
    bsz, l, n = xr.shape
    lc = min(S5_LC, l)
    nc = l // lc
    steps = int(math.log2(lc))

    def shift(v, s):
        if reverse:
            if s >= SUBLANE:
                return jnp.concatenate([v[s:], jnp.zeros((s, v.shape[1]), F32)], axis=0)
            return jnp.where(_iota(v.shape, 0) < lc - s, pltpu.roll(v, lc - s, 0), 0.0)
        if s >= SUBLANE:
            return jnp.concatenate([jnp.zeros((s, v.shape[1]), F32), v[:lc - s]], axis=0)
        return jnp.where(_iota(v.shape, 0) >= s, pltpu.roll(v, s, 0), 0.0)

    def body(xr_ref, xi_ref, pw_ref, hr_ref, hi_ref, cr, ci):
        @pl.when(pl.program_id(2) == 0)
        def _():
            cr[...] = jnp.zeros_like(cr)
            ci[...] = jnp.zeros_like(ci)

        vr, vi = xr_ref[...], xi_ref[...]
        for k in range(steps):
            s = 1 << k
            ar, ai = pw_ref[0, s - 1:s, :], pw_ref[1, s - 1:s, :]
            sr, si = shift(vr, s), shift(vi, s)
            vr, vi = vr + ar * sr - ai * si, vi + ar * si + ai * sr
        pr, pi = pw_ref[2], pw_ref[3]
        c_r, c_i = cr[...], ci[...]
        vr, vi = vr + pr * c_r - pi * c_i, vi + pr * c_i + pi * c_r
        hr_ref[...] = vr
        hi_ref[...] = vi
        edge = 0 if reverse else lc - 1
        cr[...] = vr[edge:edge + 1]
        ci[...] = vi[edge:edge + 1]

    tmap = (lambda c: nc - 1 - c) if reverse else (lambda c: c)
    blk = pl.BlockSpec((None, lc, S5_TN), lambda b, j, c: (b, tmap(c), j))
    return pl.pallas_call(
        body, out_shape=[jax.ShapeDtypeStruct(xr.shape, F32), jax.ShapeDtypeStruct(xr.shape, F32)],
        grid=(bsz, n // S5_TN, nc),
        in_specs=[blk, blk, pl.BlockSpec((4, lc, S5_TN), lambda b, j, c: (0, 0, j))], out_specs=[blk, blk],
        scratch_shapes=[pltpu.VMEM((1, S5_TN), F32), pltpu.VMEM((1, S5_TN), F32)], name=name,
        compiler_params=_cparams(("parallel", "parallel", "arbitrary")))(xr, xi, pw)


def _s5_dabar(gr, gi, hr, hi, *, name):
    bsz, l, n = gr.shape
    lc = min(S5_LC, l)
    nc = l // lc
    hb = lc // SUBLANE

    def body(gr_ref, gi_ref, hr_ref, hi_ref, hrp_ref, hip_ref, o_ref):
        c = pl.program_id(2)

        @pl.when(c == 0)
        def _():
            o_ref[...] = jnp.zeros_like(o_ref)

        def prev(h_ref, hp_ref):
            first = jnp.where(c > 0, hp_ref[SUBLANE - 1:SUBLANE, :], 0.0)
            return jnp.where(_iota((lc, S5_TN), 0) == 0, first, pltpu.roll(h_ref[...], 1, 0))

        pr, pi = prev(hr_ref, hrp_ref), prev(hi_ref, hip_ref)
        g_r, g_i = gr_ref[...], gi_ref[...]
        d_re = jnp.sum(g_r * pr + g_i * pi, axis=0, keepdims=True)
        d_im = jnp.sum(g_i * pr - g_r * pi, axis=0, keepdims=True)
        o_ref[...] += jnp.concatenate([d_re, d_im, jnp.zeros((SUBLANE - 2, S5_TN), F32)], axis=0)

    blk = pl.BlockSpec((None, lc, S5_TN), lambda b, j, c: (b, c, j))
    prv = pl.BlockSpec((None, SUBLANE, S5_TN), lambda b, j, c: (b, jnp.maximum(c * hb - 1, 0), j))
    return pl.pallas_call(
        body, out_shape=jax.ShapeDtypeStruct((bsz, SUBLANE, n), F32), grid=(bsz, n // S5_TN, nc),
        in_specs=[blk, blk, blk, blk, prv, prv],
        out_specs=pl.BlockSpec((None, SUBLANE, S5_TN), lambda b, j, c: (b, 0, j)), name=name,
        compiler_params=_cparams(("parallel", "parallel", "arbitrary")))(gr, gi, hr, hi, hr, hi)


def _s5_post_fwd(ych, u, d, wglu, *, name):
    t, w = ych.shape
    tm = _pick(t, (512, 256, 128))

    def body(y_ref, u_ref, d_ref, w_ref, o_ref):
        o_ref[...] = s5_post(y_ref[...], u_ref[...], d_ref[...], w_ref[...])

    row = pl.BlockSpec((tm, w), lambda i: (i, 0))
    return pl.pallas_call(
        body, out_shape=jax.ShapeDtypeStruct((t, w), F32), grid=(t // tm,),
        in_specs=[row, row, pl.BlockSpec((1, w), lambda i: (0, 0)), pl.BlockSpec((w, w), lambda i: (0, 0))],
        out_specs=row, name=name, compiler_params=_cparams(("parallel",)))(ych, u, d, wglu)


def _s5_post_bwd(ych, u, d, wglu, dout, *, name):
    t, w = ych.shape
    tm = _pick(t, (512, 256, 128))

    def body(y_ref, u_ref, d_ref, w_ref, do_ref, dy_ref, du_ref, dd_ref, dw_ref):
        _, vjp = jax.vjp(s5_post, y_ref[...], u_ref[...], d_ref[...], w_ref[...])
        dy, du, dd, dw = vjp(do_ref[...])
        dy_ref[...] = dy
        du_ref[...] = du

        @pl.when(pl.program_id(0) == 0)
        def _():
            dd_ref[...] = jnp.zeros_like(dd_ref)
            dw_ref[...] = jnp.zeros_like(dw_ref)

        dd_ref[...] += dd
        dw_ref[...] += dw

    row = pl.BlockSpec((tm, w), lambda i: (i, 0))
    vec = pl.BlockSpec((1, w), lambda i: (0, 0))
    mat = pl.BlockSpec((w, w), lambda i: (0, 0))
    return pl.pallas_call(
        body, out_shape=[jax.ShapeDtypeStruct((t, w), F32), jax.ShapeDtypeStruct((t, w), F32),
                         jax.ShapeDtypeStruct((1, w), F32), jax.ShapeDtypeStruct((w, w), F32)], grid=(t // tm,),
        in_specs=[row, row, vec, mat, row], out_specs=[row, row, vec, mat], name=name,
        compiler_params=_cparams(("arbitrary",)))(ych, u, d, wglu, dout)


def _s5_disc(a_re, a_im, b_re, b_im, log_dt):
    dt = jnp.exp(log_dt)[:, None]
    mag = jnp.exp(dt * a_re)
    ang = dt * a_im
    abar_re = mag * jnp.cos(ang)
    abar_im = mag * jnp.sin(ang)
    den = a_re * a_re + a_im * a_im
    zr = abar_re - 1.0
    zi = abar_im
    coef_re = ((zr * a_re + zi * a_im) / den)[..., None]
    coef_im = ((zi * a_re - zr * a_im) / den)[..., None]
    return abar_re, abar_im, coef_re * b_re - coef_im * b_im, coef_re * b_im + coef_im * b_re


def _s5_powers(a_re, a_im, log_dt, lc):
    dt = jnp.exp(log_dt)[:, None]
    n = jnp.arange(1, lc + 1, dtype=F32)[:, None, None]
    mag = jnp.exp(n * (dt * a_re)[None])
    ang = n * (dt * a_im)[None]
    pr = (mag * jnp.cos(ang)).reshape(lc, S5_N)
    pi = (mag * jnp.sin(ang)).reshape(lc, S5_N)
    fwd = jnp.stack([pr, pi, pr, pi])
    bwd = jnp.stack([pr, -pi, pr[::-1], -pi[::-1]])
    return fwd, bwd


def _block_diag_in(bb):
    eye = jnp.eye(S5_GROUPS, dtype=bb.dtype)
    return jnp.einsum("gnp,gh->gphn", bb, eye).reshape(S5_W, S5_N)


def _block_diag_out(cc):
    eye = jnp.eye(S5_GROUPS, dtype=cc.dtype)
    return jnp.einsum("gpn,gh->gnhp", cc, eye).reshape(S5_N, S5_W)


def _adamw(w, g, m, v, *, name):
    shape = w.shape
    cols = shape[-1] if w.ndim > 1 else shape[0]
    rows = w.size // cols
    tr = _pick(rows, (512, 352, 256, 128, 64, 32, 16, 8)) if rows % SUBLANE == 0 else rows
    flat = lambda t: t.reshape(rows, cols)

    def body(w_ref, g_ref, m_ref, v_ref, d_ref, nm_ref, nv_ref):
        gv = g_ref[...]
        mn = ADAM_B1 * m_ref[...] + (1.0 - ADAM_B1) * gv
        vn = ADAM_B2 * v_ref[...] + (1.0 - ADAM_B2) * jnp.square(gv)
        m_hat = mn / (1.0 - ADAM_B1 ** ADAM_STEP)
        v_hat = vn / (1.0 - ADAM_B2 ** ADAM_STEP)
        d_ref[...] = -ADAM_LR * (m_hat / (jnp.sqrt(v_hat) + ADAM_EPS) + ADAM_WD * w_ref[...])
        nm_ref[...] = mn
        nv_ref[...] = vn

    blk = pl.BlockSpec((tr, cols), lambda i: (i, 0))
    sds = jax.ShapeDtypeStruct((rows, cols), F32)
    outs = pl.pallas_call(body, out_shape=[sds, sds, sds], grid=(rows // tr,), in_specs=[blk] * 4,
                          out_specs=[blk] * 3, name=name,
                          compiler_params=_cparams(("parallel",)))(flat(w), flat(g), flat(m), flat(v))
    return tuple(o.reshape(shape) for o in outs)


ANY = pl.BlockSpec(memory_space=pl.ANY)


def _coords():
    x, y, c = lax.axis_index("x"), lax.axis_index("y"), lax.axis_index("c")
    chips = [(1 - x, y), (x, 1 - y), (1 - x, 1 - y)]
    return x, y, c, chips


def _all_gather_chips(packs, *, name):
    n = len(packs)

    def body(*refs):
        p_refs, o_refs, (send, recv) = refs[:n], refs[n:2 * n], refs[2 * n:]
        x, y, c, chips = _coords()
        sibling = (x, y, 1 - c)

        def copy(i, k, src, dst, to):
            return pltpu.make_async_remote_copy(src_ref=src, dst_ref=dst, send_sem=send.at[6 * i + k],
                                                recv_sem=recv.at[6 * i + k], device_id=to, device_id_type=MESH_ID)

        def part(i, chip, half):
            return o_refs[i].at[2 * chip[0] + chip[1], half]

        first = [copy(i, k, p_refs[i].at[c], part(i, (x, y), c), (*chip, c))
                 for i in range(n) for k, chip in enumerate(chips)]
        for cp in first:
            cp.start()
        passed = []
        for i in range(n):
            for k, chip in enumerate(chips):
                copy(i, k, part(i, chip, c), part(i, chip, c), (x, y, c)).wait_recv()
                passed.append(copy(i, 3 + k, part(i, chip, c), part(i, chip, c), sibling))
                passed[-1].start()
        for i in range(n):
            for k, chip in enumerate(chips):
                copy(i, 3 + k, part(i, chip, 1 - c), part(i, chip, 1 - c), (x, y, c)).wait_recv()
        for cp in first + passed:
            cp.wait_send()

    return pl.pallas_call(
        body, out_shape=[jax.ShapeDtypeStruct((N_CHIPS,) + p.shape, p.dtype) for p in packs], in_specs=[ANY] * n,
        out_specs=[ANY] * n,
        scratch_shapes=[pltpu.SemaphoreType.DMA((6 * n,)), pltpu.SemaphoreType.DMA((6 * n,))], name=name)(*packs)


def _swap_halves(gs, *, name):
    n = len(gs)

    def body(*refs):
        g_refs, o_refs, (send, recv) = refs[:n], refs[n:2 * n], refs[2 * n:]
        x, y, c, _ = _coords()
        cps = [pltpu.make_async_remote_copy(src_ref=g_refs[i].at[:, 1 - c], dst_ref=o_refs[i], send_sem=send.at[i],
                                            recv_sem=recv.at[i], device_id=(x, y, 1 - c), device_id_type=MESH_ID)
               for i in range(n)]
        for cp in cps:
            cp.start()
        for cp in cps:
            cp.wait()

    return pl.pallas_call(
        body, out_shape=[jax.ShapeDtypeStruct(g.shape[:1] + g.shape[2:], g.dtype) for g in gs], in_specs=[ANY] * n,
        out_specs=[ANY] * n, scratch_shapes=[pltpu.SemaphoreType.DMA((n,)), pltpu.SemaphoreType.DMA((n,))],
        name=name)(*gs)


def _add_sibling(g, got, *, name):
    n, _, m, r, w = g.shape

    def body(g0_ref, g1_ref, r_ref, a_ref, own_ref):
        c = lax.axis_index("c")
        mine = 2 * lax.axis_index("x") + lax.axis_index("y")
        j = pl.program_id(1)
        s = jnp.where(c == 0, g0_ref[...], g1_ref[...]) + r_ref[...]
        a_ref[...] = s.astype(BF16)

        @pl.when(j == 0)
        def _():
            own_ref[...] = jnp.zeros_like(own_ref)

        own_ref[...] = jnp.where(j == mine, s, own_ref[...])

    return pl.pallas_call(
        body, out_shape=[jax.ShapeDtypeStruct((n, m, r, w), BF16), jax.ShapeDtypeStruct((m, r, w), F32)],
        grid=(m, n),
        in_specs=[pl.BlockSpec((None, None, None, r, w), lambda e, j: (j, 0, e, 0, 0)),
                  pl.BlockSpec((None, None, None, r, w), lambda e, j: (j, 1, e, 0, 0)),
                  pl.BlockSpec((None, None, r, w), lambda e, j: (j, e, 0, 0))],
        out_specs=[pl.BlockSpec((None, None, r, w), lambda e, j: (j, e, 0, 0)),
                   pl.BlockSpec((None, r, w), lambda e, j: (e, 0, 0))],
        name=name, compiler_params=_cparams(("parallel", "arbitrary")))(g, g, got)


def _exchange_chips(a16s, *, name):
    n = len(a16s)

    def body(*refs):
        a_refs, o_refs, (send, recv) = refs[:n], refs[n:2 * n], refs[2 * n:]
        x, y, c, chips = _coords()
        cps = [pltpu.make_async_remote_copy(src_ref=a_refs[i].at[2 * chip[0] + chip[1]], dst_ref=o_refs[i].at[k],
                                            send_sem=send.at[3 * i + k], recv_sem=recv.at[3 * i + k],
                                            device_id=(*chip, c), device_id_type=MESH_ID)
               for i in range(n) for k, chip in enumerate(chips)]
        for cp in cps:
            cp.start()
        for cp in cps:
            cp.wait()

    return pl.pallas_call(
        body, out_shape=[jax.ShapeDtypeStruct((N_CHIPS - 1,) + a.shape[1:], a.dtype) for a in a16s],
        in_specs=[ANY] * n, out_specs=[ANY] * n,
        scratch_shapes=[pltpu.SemaphoreType.DMA((3 * n,)), pltpu.SemaphoreType.DMA((3 * n,))], name=name)(*a16s)


def _add_chips(own, got, *, name):
    m, r, w = own.shape

    def body(o_ref, g_ref, s_ref):
        s = o_ref[...]
        for k in range(N_CHIPS - 1):
            s = s + g_ref[k].astype(F32)
        s_ref[...] = s

    return pl.pallas_call(
        body, out_shape=jax.ShapeDtypeStruct((m, r, w), F32), grid=(m,),
        in_specs=[pl.BlockSpec((None, r, w), lambda i: (i, 0, 0)),
                  pl.BlockSpec((N_CHIPS - 1, None, r, w), lambda i: (0, i, 0, 0))],
        out_specs=pl.BlockSpec((None, r, w), lambda i: (i, 0, 0)), name=name,
        compiler_params=_cparams(("parallel",)))(own, got)


def _swap_reduced(halves, *, name):
    n = len(halves)

    def body(*refs):
        h_refs, o_refs, (send, recv) = refs[:n], refs[n:2 * n], refs[2 * n:]
        x, y, c, _ = _coords()
        cps = [pltpu.make_async_remote_copy(src_ref=h_refs[i], dst_ref=o_refs[i], send_sem=send.at[i],
                                            recv_sem=recv.at[i], device_id=(x, y, 1 - c), device_id_type=MESH_ID)
               for i in range(n)]
        for cp in cps:
            cp.start()
        for cp in cps:
            cp.wait()

    return pl.pallas_call(
        body, out_shape=[jax.ShapeDtypeStruct(h.shape, h.dtype) for h in halves], in_specs=[ANY] * n,
        out_specs=[ANY] * n, scratch_shapes=[pltpu.SemaphoreType.DMA((n,)), pltpu.SemaphoreType.DMA((n,))],
        name=name)(*halves)


def _gather_all(vec, *, name):
    s, w = vec.shape

    def body(v_ref, o_ref, send, recv, local):
        x, y, c, _ = _coords()
        me = 4 * x + 2 * y + c
        mine = pltpu.make_async_copy(v_ref, o_ref.at[me], local)
        mine.start()
        cps = []
        for k in range(1, N_DEV):
            px = 1 - x if k & 4 else x
            py = 1 - y if k & 2 else y
            pc = 1 - c if k & 1 else c
            peer = 4 * px + 2 * py + pc
            cps.append((pltpu.make_async_remote_copy(src_ref=v_ref, dst_ref=o_ref.at[me], send_sem=send.at[k - 1],
                                                     recv_sem=recv.at[k - 1], device_id=(px, py, pc),
                                                     device_id_type=MESH_ID),
                        pltpu.make_async_remote_copy(src_ref=v_ref, dst_ref=o_ref.at[peer], send_sem=send.at[k - 1],
                                                     recv_sem=recv.at[k - 1], device_id=(px, py, pc),
                                                     device_id_type=MESH_ID)))
        for snd, _ in cps:
            snd.start()
        for snd, rcv in cps:
            rcv.wait_recv()
            snd.wait_send()
        mine.wait()

    return pl.pallas_call(
        body, out_shape=jax.ShapeDtypeStruct((N_DEV, s, w), vec.dtype), in_specs=[ANY], out_specs=ANY,
        scratch_shapes=[pltpu.SemaphoreType.DMA((N_DEV - 1,)), pltpu.SemaphoreType.DMA((N_DEV - 1,)),
                        pltpu.SemaphoreType.DMA], name=name)(vec)


def _sum_all(parts, *, name):
    n, s, w = parts.shape
    tr = _pick(s, (96, 72, 48, 32, 24, 16, 8))

    def body(p_ref, o_ref):
        acc = p_ref[0]
        for k in range(1, n):
            acc = acc + p_ref[k]
        o_ref[...] = acc

    return pl.pallas_call(
        body, out_shape=jax.ShapeDtypeStruct((s, w), F32), grid=(s // tr,),
        in_specs=[pl.BlockSpec((n, tr, w), lambda i: (0, i, 0))], out_specs=pl.BlockSpec((tr, w), lambda i: (i, 0)),
        name=name, compiler_params=_cparams(("parallel",)))(parts)


def _rows_pack(pieces, lead):
    rows = [p.reshape(p.shape[:lead] + (-1, PACK_COLS)) for p in pieces]
    spare = 2 * ROW_BLOCKS * ROW_BLOCK - sum(r.shape[lead] for r in rows)
    rows.append(jnp.zeros(rows[0].shape[:lead] + (spare, PACK_COLS), rows[0].dtype))
    return jnp.concatenate(rows, axis=lead).reshape(rows[0].shape[:lead] + (2, ROW_BLOCKS, ROW_BLOCK, PACK_COLS))


def _rows_unpack(packed, shapes, lead):
    flat = packed.reshape(packed.shape[:lead] + (-1, PACK_COLS))
    out, off = [], 0
    for shp in shapes:
        rows = int(np.prod(shp)) // PACK_COLS
        out.append(lax.slice_in_dim(flat, off, off + rows, axis=lead).reshape(packed.shape[:lead] + tuple(shp)))
        off += rows
    return out


def _to_shards(full, axis):
    return jnp.stack(jnp.split(full, N_CHIPS, axis=axis))


def _by_head(w, parts, heads):
    r = w.shape[0]
    return w.reshape(r, parts, heads, HEAD).transpose(0, 2, 1, 3).reshape(r, parts * heads * HEAD)


def _by_part(w, parts, heads):
    r = w.shape[0]
    return w.reshape(r, heads, parts, HEAD).transpose(0, 2, 1, 3).reshape(r, parts * heads * HEAD)


def kernel(x, ffn1_norm, ffn1_w_gate, ffn1_w_up, ffn1_w_down, mix_norm, ffn2_norm, ffn2_w_gate, ffn2_w_up, ffn2_w_down, ev_w_in, hg_lb_logits, hg_norm_w, s5_a_re, s5_a_im, s5_b_re, s5_b_im, s5_c_re, s5_c_im, s5_d, s5_log_dt, s5_w_glu, ev_w_out, od_w_in, gdn_conv_w, gdn_a_log, gdn_dt_bias, gdn_norm_w, od_w_out, final_norm, loss_target, m_ffn1_norm, m_ffn1_w_gate, m_ffn1_w_up, m_ffn1_w_down, m_mix_norm, m_ffn2_norm, m_ffn2_w_gate, m_ffn2_w_up, m_ffn2_w_down, m_ev_w_in, m_hg_lb_logits, m_hg_norm_w, m_s5_a_re, m_s5_a_im, m_s5_b_re, m_s5_b_im, m_s5_c_re, m_s5_c_im, m_s5_d, m_s5_log_dt, m_s5_w_glu, m_ev_w_out, m_od_w_in, m_gdn_conv_w, m_gdn_a_log, m_gdn_dt_bias, m_gdn_norm_w, m_od_w_out, m_final_norm, v_ffn1_norm, v_ffn1_w_gate, v_ffn1_w_up, v_ffn1_w_down, v_mix_norm, v_ffn2_norm, v_ffn2_w_gate, v_ffn2_w_up, v_ffn2_w_down, v_ev_w_in, v_hg_lb_logits, v_hg_norm_w, v_s5_a_re, v_s5_a_im, v_s5_b_re, v_s5_b_im, v_s5_c_re, v_s5_c_im, v_s5_d, v_s5_log_dt, v_s5_w_glu, v_ev_w_out, v_od_w_in, v_gdn_conv_w, v_gdn_a_log, v_gdn_dt_bias, v_gdn_norm_w, v_od_w_out, v_final_norm):
    args = locals()
    wts = {n: args[n] for n in WEIGHTS}
    mom = {n: args["m_" + n] for n in WEIGHTS}
    var = {n: args["v_" + n] for n in WEIGHTS}
    bsz, seq, _ = x.shape
    t = bsz * seq

    def wire(n):
        if n == "gdn_conv_w":
            return lax.bitcast_convert_type(wts[n], BF16)
        return wts[n].astype(BF16)

    chip = 2 * lax.axis_index("x") + lax.axis_index("y")
    is_own = (jnp.arange(N_CHIPS) == chip).reshape(N_CHIPS, 1, 1, 1, 1)
    n_wide = wts[WIDE[0]].shape[0]
    pack_a = jnp.concatenate([wts[n].astype(BF16) for n in WIDE], axis=0)
    pack_a = pack_a.reshape((2, len(WIDE) * n_wide // 2) + pack_a.shape[1:])
    pack_b = _rows_pack([wire(n) for n, _ in ROWS], 0)
    got_a, got_b = _all_gather_chips([pack_a, pack_b], name="gather_weights")
    got_a = jnp.where(is_own, pack_a[None], got_a).reshape((N_CHIPS, len(WIDE) * n_wide) + pack_a.shape[2:])
    got_b = jnp.where(is_own, pack_b[None], got_b)
    full = {}
    for k, n in enumerate(WIDE):
        full[n] = [jnp.concatenate([got_a[j, k * n_wide + i] for j in range(N_CHIPS)], axis=1) for i in range(n_wide)]
    for (n, axis), piece in zip(ROWS, _rows_unpack(got_b, [wire(n).shape for n, _ in ROWS], 1)):
        if n == "gdn_conv_w":
            piece = lax.bitcast_convert_type(piece, F32)
        full[n] = [jnp.concatenate([piece[j, i] for j in range(N_CHIPS)], axis=axis - 1)
                   for i in range(piece.shape[1])]

    ev_cols = lambda w: jnp.concatenate([_by_head(w[:, :4 * HG_W], 4, HG_HEADS), w[:, 4 * HG_W:]], axis=1)
    ev_cols_back = lambda w: jnp.concatenate([_by_part(w[:, :4 * HG_W], 4, HG_HEADS), w[:, 4 * HG_W:]], axis=1)

    lbs_fn = lambda lg: (lambda p: jnp.cumsum(p, axis=0) - p[0])(jax.nn.softmax(lg, axis=0))
    lbs, lbs_vjp = jax.vjp(lbs_fn, hg_lb_logits)
    lc = min(S5_LC, seq)

    xs = x.reshape(t, D_MODEL)
    saved = []
    for layer in range(DEPTH):
        j = layer // 2
        rec = {}
        xs, rec["ffn1"] = _ffn_fwd(xs, ffn1_norm[layer][None], full["ffn1_w_gate"][layer], full["ffn1_w_up"][layer],
                                   full["ffn1_w_down"][layer], f"a{layer}")
        rec["x_mix"] = xs
        h = _rms_fwd(xs, mix_norm[layer][None], name=f"rms_fwd_mix{layer}")
        rec["h"] = h
        if layer % 2 == 0:
            w_in = ev_cols(full["ev_w_in"][j])
            proj = _mm(h, w_in, "nn", name=f"ev_proj{layer}").reshape(bsz, seq, -1)
            rec["proj"] = proj
            y_a, rec["hg_states"] = _hg_fwd(proj, lbs[j][None], hg_norm_w[j][None], name=f"hgrn2_fwd{layer}")
            disc, rec["disc_vjp"] = jax.vjp(_s5_disc, s5_a_re[j], s5_a_im[j], s5_b_re[j], s5_b_im[j], s5_log_dt[j])
            wb_re, wb_im = _block_diag_in(disc[2]).astype(BF16), _block_diag_in(disc[3]).astype(BF16)
            wc_re = _block_diag_out(s5_c_re[j]).astype(BF16)
            wc_im = _block_diag_out(-s5_c_im[j]).astype(BF16)
            pw_f, pw_b = _s5_powers(s5_a_re[j], s5_a_im[j], s5_log_dt[j], lc)
            u = proj[:, :, 4 * HG_W:].reshape(t, S5_W)
            bu_re = _mm(u, wb_re, "nn", name=f"s5_bu_re{layer}").reshape(bsz, seq, S5_N)
            bu_im = _mm(u, wb_im, "nn", name=f"s5_bu_im{layer}").reshape(bsz, seq, S5_N)
            h_re, h_im = _s5_scan(bu_re, bu_im, pw_f, reverse=False, name=f"s5_scan_fwd{layer}")
            h_re2, h_im2 = h_re.reshape(t, S5_N), h_im.reshape(t, S5_N)
            ych = _mm(h_re2, wc_re, "nn", name=f"s5_out_re{layer}")
            ych = _mm(h_im2, wc_im, "nn", res=ych, name=f"s5_out_im{layer}")
            w_glu = full["s5_w_glu"][j]
            y_b = _s5_post_fwd(ych, u, s5_d[j][None], w_glu, name=f"s5_post_fwd{layer}")
            rec.update(u=u, h_re=h_re, h_im=h_im, ych=ych, wb=(wb_re, wb_im), wc=(wc_re, wc_im), pw_b=pw_b,
                       y_a=y_a.reshape(t, HG_W), y_b=y_b)
            w_out = full["ev_w_out"][j]
            xs = _mm(rec["y_a"], w_out[:HG_W], "nn", res=xs, name=f"ev_out_a{layer}")
            xs = _mm(y_b, w_out[HG_W:], "nn", res=xs, name=f"ev_out_b{layer}")
        else:
            w_in = full["od_w_in"][j]
            w_main = _by_head(w_in[:, :4 * GDN_W], 4, GDN_HEADS)
            w_bd = jnp.pad(w_in[:, 4 * GDN_W:], ((0, 0), (0, HEAD - 2 * GDN_HEADS)))
            proj = _mm(h, w_main, "nn", name=f"od_proj{layer}").reshape(bsz, seq, -1)
            bd = _mm(h, w_bd, "nn", name=f"od_proj_bd{layer}").reshape(bsz, seq, HEAD)
            conv_w = full["gdn_conv_w"][j].reshape(CONV_W, 3, GDN_HEADS, HEAD).transpose(0, 2, 1, 3)
            conv_w = jnp.pad(conv_w, ((0, SUBLANE - CONV_W), (0, 0), (0, 1), (0, 0))).reshape(SUBLANE, 4 * GDN_W)
            qkvg = _conv_fwd(proj, conv_w, name=f"gdn_conv_fwd{layer}")
            ab = jnp.pad(gdn_a_log[j][None], ((0, 0), (GDN_HEADS, HEAD - 2 * GDN_HEADS)))
            dtb = jnp.pad(gdn_dt_bias[j][None], ((0, 0), (GDN_HEADS, HEAD - 2 * GDN_HEADS)))
            y, states = _gdn_fwd(qkvg, bd, ab, dtb, gdn_norm_w[j][None], name=f"gdn_fwd{layer}")
            rec.update(proj=proj, bd=bd, conv_w=conv_w, qkvg=qkvg, ab=ab, dtb=dtb, states=states,
                       y=y.reshape(t, GDN_W), w_main=w_main, w_bd=w_bd)
            xs = _mm(rec["y"], full["od_w_out"][j], "nn", res=xs, name=f"od_out{layer}")
        xs, rec["ffn2"] = _ffn_fwd(xs, ffn2_norm[layer][None], full["ffn2_w_gate"][layer], full["ffn2_w_up"][layer],
                                   full["ffn2_w_down"][layer], f"b{layer}")
        saved.append(rec)

    loss_part, dx, d_final = _final_loss(xs, final_norm[None], loss_target.reshape(t, D_MODEL), name="loss_head")
    loss = lax.psum(loss_part[0, 0], ("x", "y", "c"))

    grads = {n: [None] * wts[n].shape[0] for n in WEIGHTS if n != "final_norm"}
    grads["final_norm"] = d_final[0]
    d_lbs = [None] * 2
    for layer in reversed(range(DEPTH)):
        j = layer // 2
        rec = saved[layer]
        dx, dn, dwg, dwu, dwd = _ffn_bwd(rec["ffn2"], ffn2_norm[layer][None], full["ffn2_w_gate"][layer],
                                         full["ffn2_w_up"][layer], full["ffn2_w_down"][layer], dx, f"b{layer}")
        grads["ffn2_norm"][layer], grads["ffn2_w_gate"][layer] = dn[0], dwg
        grads["ffn2_w_up"][layer], grads["ffn2_w_down"][layer] = dwu, dwd
        h = rec["h"]
        if layer % 2 == 0:
            w_out = full["ev_w_out"][j]
            dy_a = _mm(dx, w_out[:HG_W], "nt", name=f"ev_dya{layer}")
            dy_b = _mm(dx, w_out[HG_W:], "nt", name=f"ev_dyb{layer}")
            dw_out = jnp.concatenate([_mm(rec["y_a"], dx, "tn", name=f"ev_dwo_a{layer}"),
                                      _mm(rec["y_b"], dx, "tn", name=f"ev_dwo_b{layer}")], axis=0)
            grads["ev_w_out"][j] = dw_out
            wb_re, wb_im = rec["wb"]
            wc_re, wc_im = rec["wc"]
            dych, du, dd, dwglu = _s5_post_bwd(rec["ych"], rec["u"], s5_d[j][None], full["s5_w_glu"][j], dy_b,
                                               name=f"s5_post_bwd{layer}")
            grads["s5_d"][j], grads["s5_w_glu"][j] = dd[0], dwglu
            h_re2, h_im2 = rec["h_re"].reshape(t, S5_N), rec["h_im"].reshape(t, S5_N)
            dwc_re = _mm(h_re2, dych, "tn", name=f"s5_dwc_re{layer}")
            dwc_im = _mm(h_im2, dych, "tn", name=f"s5_dwc_im{layer}")
            dh_re = _mm(dych, wc_re, "nt", name=f"s5_dh_re{layer}").reshape(bsz, seq, S5_N)
            dh_im = _mm(dych, wc_im, "nt", name=f"s5_dh_im{layer}").reshape(bsz, seq, S5_N)
            g_re, g_im = _s5_scan(dh_re, dh_im, rec["pw_b"], reverse=True, name=f"s5_scan_bwd{layer}")
            da = _s5_dabar(g_re, g_im, rec["h_re"], rec["h_im"], name=f"s5_dabar{layer}").sum(axis=0)
            g_re2, g_im2 = g_re.reshape(t, S5_N), g_im.reshape(t, S5_N)
            dwb_re = _mm(rec["u"], g_re2, "tn", name=f"s5_dwb_re{layer}")
            dwb_im = _mm(rec["u"], g_im2, "tn", name=f"s5_dwb_im{layer}")
            du = _mm(g_re2, wb_re, "nt", res=du, name=f"s5_du_re{layer}")
            du = _mm(g_im2, wb_im, "nt", res=du, name=f"s5_du_im{layer}")
            diag_in = lambda m: jnp.einsum("gpgn->gnp", m.reshape(S5_GROUPS, S5_GROUP, S5_GROUPS, S5_STATE))
            diag_out = lambda m: jnp.einsum("gngp->gpn", m.reshape(S5_GROUPS, S5_STATE, S5_GROUPS, S5_GROUP))
            d_disc = (da[0].reshape(S5_GROUPS, S5_STATE), da[1].reshape(S5_GROUPS, S5_STATE), diag_in(dwb_re),
                      diag_in(dwb_im))
            ga_re, ga_im, gb_re, gb_im, g_dt = rec["disc_vjp"](d_disc)
            grads["s5_a_re"][j], grads["s5_a_im"][j], grads["s5_b_re"][j] = ga_re, ga_im, gb_re
            grads["s5_b_im"][j], grads["s5_log_dt"][j] = gb_im, g_dt
            grads["s5_c_re"][j], grads["s5_c_im"][j] = diag_out(dwc_re), -diag_out(dwc_im)
            dproj_hg, dlb, dnw = _hg_bwd(rec["proj"], lbs[j][None], hg_norm_w[j][None], rec["hg_states"],
                                         dy_a.reshape(bsz, seq, HG_W), name=f"hgrn2_bwd{layer}")
            d_lbs[j] = dlb.sum(axis=0).reshape(HG_W)
            grads["hg_norm_w"][j] = dnw.sum(axis=(0, 1))
            dproj = jnp.concatenate([dproj_hg.reshape(t, 4 * HG_W), du], axis=1)
            w_in = ev_cols(full["ev_w_in"][j])
            dw_in = _mm(h, dproj, "tn", name=f"ev_dwin{layer}")
            grads["ev_w_in"][j] = ev_cols_back(dw_in)
            dh = _mm(dproj, w_in, "nt", name=f"ev_dh{layer}")
        else:
            dy = _mm(dx, full["od_w_out"][j], "nt", name=f"od_dy{layer}")
            grads["od_w_out"][j] = _mm(rec["y"], dx, "tn", name=f"od_dwo{layer}")
            dqkvg, dbd_h, dab, ddt, dnw = _gdn_bwd(rec["qkvg"], rec["bd"], rec["ab"], rec["dtb"],
                                                   gdn_norm_w[j][None], rec["states"],
                                                   dy.reshape(bsz, seq, GDN_W), name=f"gdn_bwd{layer}")
            grads["gdn_norm_w"][j] = dnw.sum(axis=(0, 1))
            grads["gdn_a_log"][j] = dab.sum(axis=(0, 1))[GDN_HEADS:2 * GDN_HEADS]
            grads["gdn_dt_bias"][j] = ddt.sum(axis=(0, 1))[GDN_HEADS:2 * GDN_HEADS]
            dbd = dbd_h.sum(axis=1).reshape(t, HEAD)
            dproj, dconv = _conv_bwd(rec["proj"], rec["conv_w"], dqkvg, name=f"gdn_conv_bwd{layer}")
            dconv = dconv.sum(axis=0)[:CONV_W].reshape(CONV_W, GDN_HEADS, 4, HEAD)[:, :, :3]
            grads["gdn_conv_w"][j] = dconv.transpose(0, 2, 1, 3).reshape(CONV_W, 3 * GDN_W)
            dproj = dproj.reshape(t, 4 * GDN_W)
            dw_main = _by_part(_mm(h, dproj, "tn", name=f"od_dwin{layer}"), 4, GDN_HEADS)
            dw_bd = _mm(h, dbd, "tn", name=f"od_dwbd{layer}")[:, :2 * GDN_HEADS]
            grads["od_w_in"][j] = jnp.concatenate([dw_main, dw_bd], axis=1)
            dh = _mm(dproj, rec["w_main"], "nt", name=f"od_dh{layer}")
            dh = _mm(dbd, rec["w_bd"], "nt", res=dh, name=f"od_dh_bd{layer}")
        dx, dn = _rms_bwd(rec["x_mix"], mix_norm[layer][None], dh, dx, name=f"rms_bwd_mix{layer}")
        grads["mix_norm"][layer] = dn[0]
        dx, dn, dwg, dwu, dwd = _ffn_bwd(rec["ffn1"], ffn1_norm[layer][None], full["ffn1_w_gate"][layer],
                                         full["ffn1_w_up"][layer], full["ffn1_w_down"][layer], dx, f"a{layer}")
        grads["ffn1_norm"][layer], grads["ffn1_w_gate"][layer] = dn[0], dwg
        grads["ffn1_w_up"][layer], grads["ffn1_w_down"][layer] = dwu, dwd
    grad_x = dx.reshape(x.shape)
    (grads["hg_lb_logits"],) = lbs_vjp(jnp.stack(d_lbs))
    wide_grads = {n: grads[n] for n in WIDE}
    for n in WEIGHTS:
        if isinstance(grads[n], list) and n not in WIDE:
            grads[n] = jnp.stack(grads[n])

    g_a = jnp.stack([wide_grads[n][i].reshape(D_MODEL, N_CHIPS, -1).transpose(1, 0, 2)
                     for n in WIDE for i in range(n_wide)], axis=1)
    g_a = g_a.reshape((N_CHIPS, 2, g_a.shape[1] // 2) + g_a.shape[2:])
    g_b = _rows_pack([_to_shards(grads[n], axis) for n, axis in ROWS], 1)
    sib_a, sib_b = _swap_halves([g_a, g_b], name="reduce_swap_halves")
    wire_a, own_a = _add_sibling(g_a, sib_a, name="reduce_add_sibling_wide")
    wire_b, own_b = _add_sibling(g_b, sib_b, name="reduce_add_sibling_rows")
    oth_a, oth_b = _exchange_chips([wire_a, wire_b], name="reduce_exchange_chips")
    half_a = _add_chips(own_a, oth_a, name="reduce_add_chips_wide")
    half_b = _add_chips(own_b, oth_b, name="reduce_add_chips_rows")
    peer_a, peer_b = _swap_reduced([half_a, half_b], name="reduce_swap_reduced")
    south = lax.axis_index("c") == 0
    both = lambda mine, peer: jnp.where(south, jnp.stack([mine, peer]), jnp.stack([peer, mine]))
    red_a = both(half_a, peer_a).reshape((len(WIDE) * n_wide,) + half_a.shape[1:])
    for k, n in enumerate(WIDE):
        grads[n] = red_a[k * n_wide:(k + 1) * n_wide]
    for (n, _), piece in zip(ROWS, _rows_unpack(both(half_b, peer_b), [wts[n].shape for n, _ in ROWS], 0)):
        grads[n] = piece

    small = jnp.concatenate([grads[n].reshape(-1) for n in SMALL])
    small = jnp.pad(small, (0, SMALL_ROWS * PACK_COLS - small.shape[0])).reshape(SMALL_ROWS, PACK_COLS)
    small = _sum_all(_gather_all(small, name="gather_small_grads"), name="sum_small_grads").reshape(-1)
    off = 0
    for n in SMALL:
        size = int(np.prod(wts[n].shape))
        grads[n] = small[off:off + size].reshape(wts[n].shape)
        off += size

    small_w = jnp.concatenate([wts[n].reshape(-1) for n in SMALL])
    small_m = jnp.concatenate([mom[n].reshape(-1) for n in SMALL])
    small_v = jnp.concatenate([var[n].reshape(-1) for n in SMALL])
    n_small = small_w.shape[0]
    padv = lambda a, fill: jnp.pad(a, (0, SMALL_ROWS * PACK_COLS - n_small), constant_values=fill).reshape(
        SMALL_ROWS, PACK_COLS)
    sd, sm, sv = _adamw(padv(small_w, 0.0), small.reshape(SMALL_ROWS, PACK_COLS), padv(small_m, 0.0),
                        padv(small_v, 1.0), name="adamw_small")
    delta, new_m, new_v = {}, {}, {}
    off = 0
    for n in SMALL:
        size = int(np.prod(wts[n].shape))
        delta[n] = sd.reshape(-1)[off:off + size].reshape(wts[n].shape)
        new_m[n] = sm.reshape(-1)[off:off + size].reshape(wts[n].shape)
        new_v[n] = sv.reshape(-1)[off:off + size].reshape(wts[n].shape)
        off += size
    for n, _ in SHARDED:
        delta[n], new_m[n], new_v[n] = _adamw(wts[n], grads[n], mom[n], var[n], name=f"adamw_{n}")

    return (loss, grad_x, *[grads[n] for n in WEIGHTS], *[delta[n] for n in WEIGHTS],
            *[new_m[n] for n in WEIGHTS], *[new_v[n] for n in WEIGHTS])
```

```python
import functools
import math

import jax
import jax.numpy as jnp
import numpy as np
from jax import lax
from jax.experimental import pallas as pl
from jax.experimental.pallas import tpu as pltpu

F32 = jnp.float32
BF16 = jnp.bfloat16
LHS_EXACT = 2
MESH_ID = pl.DeviceIdType.MESH

D_MODEL = 1024
D_FF = 2816
DEPTH = 4
NORM_EPS = 1e-6
F_MIN = 1e-6
CHUNK = 64
HEAD = 128
HG_HEADS = 4
HG_W = 512
S5_W = 512
S5_GROUP = 16
S5_GROUPS = 32
S5_STATE = 64
S5_N = S5_GROUPS * S5_STATE
GDN_HEADS = 8
GDN_W = 1024
CONV_W = 4
N_CHIPS = 4
N_DEV = 8
LANE = 128
SUBLANE = 8
VMEM_LIMIT = 56 * 1024 * 1024
PACK_COLS = 1024
ROW_BLOCK = 512
ROW_BLOCKS = 10
SMALL_ROWS = 288

ADAM_LR = 0.001
ADAM_B1 = 0.9
ADAM_B2 = 0.999
ADAM_EPS = 1e-08
ADAM_WD = 0.01
ADAM_STEP = 10

SHARDED = (("ffn1_w_gate", 2), ("ffn1_w_up", 2), ("ffn1_w_down", 1), ("ffn2_w_gate", 2), ("ffn2_w_up", 2),
           ("ffn2_w_down", 1), ("ev_w_in", 2), ("s5_w_glu", 1), ("ev_w_out", 1), ("od_w_in", 2),
           ("gdn_conv_w", 2), ("od_w_out", 1))
WIDE = ("ffn1_w_gate", "ffn1_w_up", "ffn2_w_gate", "ffn2_w_up")
ROWS = (("ffn1_w_down", 1), ("ffn2_w_down", 1), ("ev_w_out", 1), ("od_w_out", 1), ("s5_w_glu", 1), ("ev_w_in", 2),
        ("od_w_in", 2), ("gdn_conv_w", 2))
SMALL = ("ffn1_norm", "mix_norm", "ffn2_norm", "hg_lb_logits", "hg_norm_w", "s5_a_re", "s5_a_im", "s5_b_re",
         "s5_b_im", "s5_c_re", "s5_c_im", "s5_d", "s5_log_dt", "gdn_a_log", "gdn_dt_bias", "gdn_norm_w",
         "final_norm")
WEIGHTS = ("ffn1_norm", "ffn1_w_gate", "ffn1_w_up", "ffn1_w_down", "mix_norm", "ffn2_norm", "ffn2_w_gate",
           "ffn2_w_up", "ffn2_w_down", "ev_w_in", "hg_lb_logits", "hg_norm_w", "s5_a_re", "s5_a_im", "s5_b_re",
           "s5_b_im", "s5_c_re", "s5_c_im", "s5_d", "s5_log_dt", "s5_w_glu", "ev_w_out", "od_w_in", "gdn_conv_w",
           "gdn_a_log", "gdn_dt_bias", "gdn_norm_w", "od_w_out", "final_norm")


def _cparams(sem=None):
    return pltpu.CompilerParams(dimension_semantics=sem, vmem_limit_bytes=VMEM_LIMIT)


def _dg(a, b, ca, cb, hi):
    if a.ndim == 3 or b.ndim == 3:
        n = a.shape[0] if a.ndim == 3 else b.shape[0]
        a = a if a.ndim == 3 else jnp.broadcast_to(a, (n,) + a.shape)
        b = b if b.ndim == 3 else jnp.broadcast_to(b, (n,) + b.shape)
        dims = (((ca + 1,), (cb + 1,)), ((0,), (0,)))
    else:
        dims = (((ca,), (cb,)), ((), ()))
    dot = lambda p, q: lax.dot_general(p, q, dims, preferred_element_type=F32)
    if hi:
        a_hi, b_hi = a.astype(BF16), b.astype(BF16)
        b_lo = (b - b_hi.astype(F32)).astype(BF16)
        if hi == LHS_EXACT:
            b_lo2 = (b - b_hi.astype(F32) - b_lo.astype(F32)).astype(BF16)
            return dot(a_hi, b_hi) + (dot(a_hi, b_lo) + dot(a_hi, b_lo2))
        a_lo = (a - a_hi.astype(F32)).astype(BF16)
        return dot(a_hi, b_hi) + (dot(a_hi, b_lo) + dot(a_lo, b_hi))
    return dot(a.astype(BF16), b.astype(BF16))


@functools.partial(jax.custom_vjp, nondiff_argnums=(2,))
def mm_nn(a, b, hi=False):
    return _dg(a, b, 1, 0, hi)


@functools.partial(jax.custom_vjp, nondiff_argnums=(2,))
def mm_nt(a, b, hi=False):
    return _dg(a, b, 1, 1, hi)


@functools.partial(jax.custom_vjp, nondiff_argnums=(2,))
def mm_tn(a, b, hi=False):
    return _dg(a, b, 0, 0, hi)


mm_nn.defvjp(lambda a, b, hi: (_dg(a, b, 1, 0, hi), (a, b)),
             lambda hi, r, g: (mm_nt(g, r[1], hi), mm_tn(r[0], g, hi)))
mm_nt.defvjp(lambda a, b, hi: (_dg(a, b, 1, 1, hi), (a, b)),
             lambda hi, r, g: (mm_nn(g, r[1], hi), mm_tn(g, r[0], hi)))
mm_tn.defvjp(lambda a, b, hi: (_dg(a, b, 0, 0, hi), (a, b)),
             lambda hi, r, g: (mm_nt(r[1], g, hi), mm_nn(r[0], g, hi)))


def _sigmoid(x):
    return 1.0 / (1.0 + jnp.exp(-x))


def _silu(x):
    return x * _sigmoid(x)


def _softplus(x):
    return jnp.maximum(x, 0.0) + jnp.log(1.0 + jnp.exp(-jnp.abs(x)))


def _gelu(x):
    return 0.5 * x * (1.0 + jnp.tanh(math.sqrt(2.0 / math.pi) * (x + 0.044715 * (x * x * x))))


def _iota(shape, dim):
    return lax.broadcasted_iota(jnp.int32, shape, dim)


def _l2norm(t):
    return t * lax.rsqrt(jnp.sum(t * t, axis=-1, keepdims=True) + NORM_EPS)


def gated_head_norm(o, gate, nw):
    o = o * lax.rsqrt(jnp.mean(o * o, axis=-1, keepdims=True) + NORM_EPS) * nw
    return o * _silu(gate)


HG_SUB = 8
HG_MID = 3


def hg_chunk(st, ql, fl, v, gl, lb, nw):
    c = CHUNK
    n = st.shape[0]
    q = _silu(ql)
    f = lb + (1.0 - lb) * _sigmoid(fl)
    lf = jnp.log(jnp.maximum(f, F_MIN))
    k = 1.0 - f
    t_i = _iota((2 * c, c), 0)
    s_i = _iota((2 * c, c), 1)
    lim = jnp.where(t_i < c, t_i, ((t_i - c) // HG_SUB) * HG_SUB + HG_MID)
    sums = jnp.broadcast_to(jnp.where(s_i <= lim, 1.0, 0.0).astype(F32), (n, 2 * c, c))
    cum = mm_nn(sums, lf, LHS_EXACT)
    b, rb = cum[:, :c], cum[:, c:]
    qp = q * jnp.exp(b - rb)
    row = _iota((1, c, HEAD), 1)
    blocks = []
    for i in range(c // HG_SUB):
        r_i = jnp.tile(rb[:, i * HG_SUB:(i + 1) * HG_SUB], (1, c // HG_SUB, 1))
        seen = row < (i + 1) * HG_SUB
        kp = k * jnp.exp(jnp.where(seen, r_i - b, 0.0))
        blocks.append(mm_nt(qp[:, i * HG_SUB:(i + 1) * HG_SUB], kp))
    a = jnp.concatenate(blocks, axis=1)
    a = jnp.where(_iota((1, c, c), 2) <= _iota((1, c, c), 1), a, 0.0)
    o = mm_nn(a, v) + mm_nt(q * jnp.exp(b), st)
    bl = jnp.sum(lf, axis=1, keepdims=True)
    st_new = st * jnp.exp(bl) + mm_tn(v, k * jnp.exp(bl - b))
    return st_new, gated_head_norm(o, gl, nw)


def gdn_chunk(st, qc, kc, vc, gate, bd, ab, dtb, nw, mb, ma):
    c = CHUNK
    n = st.shape[0]
    beta = _sigmoid(jnp.sum(bd * mb, axis=-1, keepdims=True))
    a_lin = jnp.sum(bd * ma, axis=-1, keepdims=True)
    a_log = jnp.sum(ab * ma, axis=-1, keepdims=True)
    dt_b = jnp.sum(dtb * ma, axis=-1, keepdims=True)
    la = -jnp.exp(a_log) * _softplus(a_lin + dt_b)
    la_b = jnp.broadcast_to(la, (n, c, HEAD))
    t_i = _iota((1, c, c), 1)
    s_i = _iota((1, c, c), 2)
    lower = s_i <= t_i
    strict = s_i < t_i
    tri = jnp.broadcast_to(jnp.where(lower, 1.0, 0.0).astype(F32), (n, c, c))
    g = mm_nn(tri, la_b, LHS_EXACT)
    d = mm_nn(tri, jnp.where(strict, la_b[:, :, :c], 0.0), LHS_EXACT)
    lm = jnp.where(lower, jnp.exp(jnp.where(lower, d, 0.0)), 0.0)
    q = _l2norm(qc) * (HEAD ** -0.5)
    k = _l2norm(kc)
    kb = k * beta
    vb = vc * beta
    m = jnp.where(strict, mm_nt(kb, k) * lm, 0.0)
    eye = jnp.where(s_i == t_i, 1.0, 0.0).astype(F32)
    t_inv = eye - m
    p = m
    for _ in range(int(math.log2(c)) - 1):
        p = mm_nn(p, p, True)
        t_inv = t_inv + mm_nn(t_inv, p, True)
    u = mm_nn(t_inv, vb)
    w = mm_nn(t_inv, kb * jnp.exp(g))
    attn = mm_nt(q, k) * lm
    v_new = u - mm_nt(w, st)
    o = mm_nt(q * jnp.exp(g), st) + mm_nn(attn, v_new)
    g_last = jnp.sum(la_b, axis=1, keepdims=True)
    st_new = st * jnp.exp(g_last) + mm_tn(v_new, k * jnp.exp(g_last - g))
    return st_new, gated_head_norm(o, gate, nw)


def s5_post(ych, u, d, wglu):
    y = _gelu(ych + d * u)
    return y * _sigmoid(mm_nn(y, wglu))


def _pick(n, cands):
    for c in cands:
        if n % c == 0:
            return c
    return n


def _mm(a, b, mode, *, name, out_dtype=F32, res=None, scale=1.0, tm=None, tn=None, tk=None, a2=None, b2=None):
    if mode == "nn":
        (m, k), (k2, n) = a.shape, b.shape
    elif mode == "nt":
        (m, k), (n, k2) = a.shape, b.shape
    else:
        (k, m), (k2, n) = a.shape, b.shape
    assert k == k2, (a.shape, b.shape, mode)
    tm = tm or _pick(m, (512, 1408, 256, 128) if mode == "tn" else (512, 704, 256, 128))
    tn = tn or _pick(n, (1024, 1408, 512, 256, 128))
    tk = tk or _pick(k, (1024, 1408, 512, 256, 128))
    nk = k // tk
    grid = (n // tn, m // tm, nk)
    if mode == "tn":
        a_spec = pl.BlockSpec((tk, tm), lambda j, i, kk: (kk, i))
    else:
        a_spec = pl.BlockSpec((tm, tk), lambda j, i, kk: (i, kk))
    if mode == "nt":
        b_spec = pl.BlockSpec((tn, tk), lambda j, i, kk: (j, kk))
    else:
        b_spec = pl.BlockSpec((tk, tn), lambda j, i, kk: (kk, j))
    o_spec = pl.BlockSpec((tm, tn), lambda j, i, kk: (i, j))
    ca = 0 if mode == "tn" else 1
    cb = 1 if mode == "nt" else 0
    has_res = res is not None
    pairs = 1 if a2 is None else 2
    assert pairs == 1 or (a2.shape == a.shape and b2.shape == b.shape)

    def body(*refs):
        ab_refs, refs = refs[:2 * pairs], refs[2 * pairs:]
        if has_res:
            r_ref, o_ref, acc = refs
        else:
            o_ref, acc = refs
        kk = pl.program_id(2)

        @pl.when(kk == 0)
        def _():
            acc[...] = jnp.zeros_like(acc)

        part = _dg(ab_refs[0][...], ab_refs[1][...], ca, cb, False)
        if pairs == 2:
            part = part + _dg(ab_refs[2][...], ab_refs[3][...], ca, cb, False)
        acc[...] += part

        @pl.when(kk == nk - 1)
        def _():
            r = acc[...] * scale if scale != 1.0 else acc[...]
            if has_res:
                r = r_ref[...] + r
            o_ref[...] = r.astype(out_dtype)

    in_specs = [a_spec, b_spec] * pairs + ([o_spec] if has_res else [])
    args = (a, b) + ((a2, b2) if pairs == 2 else ()) + ((res,) if has_res else ())
    return pl.pallas_call(
        body, out_shape=jax.ShapeDtypeStruct((m, n), out_dtype), grid=grid, in_specs=in_specs, out_specs=o_spec,
        scratch_shapes=[pltpu.VMEM((tm, tn), F32)], name=name,
        compiler_params=_cparams(("parallel", "parallel", "arbitrary")))(*args)


def _rms_fwd(x, w, *, name):
    t, d = x.shape
    tm = _pick(t, (512, 256, 128))

    def body(x_ref, w_ref, o_ref):
        xv = x_ref[...]
        r = lax.rsqrt(jnp.mean(xv * xv, axis=-1, keepdims=True) + NORM_EPS)
        o_ref[...] = (xv * r * w_ref[...]).astype(BF16)

    return pl.pallas_call(
        body, out_shape=jax.ShapeDtypeStruct((t, d), BF16), grid=(t // tm,),
        in_specs=[pl.BlockSpec((tm, d), lambda i: (i, 0)), pl.BlockSpec((1, d), lambda i: (0, 0))],
        out_specs=pl.BlockSpec((tm, d), lambda i: (i, 0)), name=name, compiler_params=_cparams(("parallel",)))(x, w)


def _rms_bwd(x, w, dh, dres, *, name):
    t, d = x.shape
    tm = _pick(t, (512, 256, 128))

    def body(x_ref, w_ref, dh_ref, dr_ref, dx_ref, dw_ref):
        xv = x_ref[...]
        r = lax.rsqrt(jnp.mean(xv * xv, axis=-1, keepdims=True) + NORM_EPS)
        xh = xv * r
        dhv = dh_ref[...]
        dxh = dhv * w_ref[...]
        dx_ref[...] = dr_ref[...] + r * (dxh - xh * jnp.mean(dxh * xh, axis=-1, keepdims=True))

        @pl.when(pl.program_id(0) == 0)
        def _():
            dw_ref[...] = jnp.zeros_like(dw_ref)

        dw_ref[...] += jnp.sum(dhv * xh, axis=0, keepdims=True)

    row = pl.BlockSpec((tm, d), lambda i: (i, 0))
    vec = pl.BlockSpec((1, d), lambda i: (0, 0))
    return pl.pallas_call(
        body, out_shape=[jax.ShapeDtypeStruct((t, d), F32), jax.ShapeDtypeStruct((1, d), F32)], grid=(t // tm,),
        in_specs=[row, vec, row, row], out_specs=[row, vec], name=name,
        compiler_params=_cparams(("arbitrary",)))(x, w, dh, dres)


def _final_loss(x, w, target, *, name):
    t, d = x.shape
    tm = _pick(t, (512, 256, 128))

    def body(x_ref, w_ref, t_ref, l_ref, dx_ref, dw_ref):
        xv = x_ref[...]
        wv = w_ref[...]
        r = lax.rsqrt(jnp.mean(xv * xv, axis=-1, keepdims=True) + NORM_EPS)
        xh = xv * r
        err = xh * wv - t_ref[...]
        dy = err * (1.0 / d)
        dxh = dy * wv
        dx_ref[...] = r * (dxh - xh * jnp.mean(dxh * xh, axis=-1, keepdims=True))

        @pl.when(pl.program_id(0) == 0)
        def _():
            dw_ref[...] = jnp.zeros_like(dw_ref)
            l_ref[...] = jnp.zeros_like(l_ref)

        dw_ref[...] += jnp.sum(dy * xh, axis=0, keepdims=True)
        l_ref[...] += 0.5 * jnp.sum(jnp.mean(err * err, axis=-1, keepdims=True), axis=0, keepdims=True)

    row = pl.BlockSpec((tm, d), lambda i: (i, 0))
    vec = pl.BlockSpec((1, d), lambda i: (0, 0))
    return pl.pallas_call(
        body, out_shape=[jax.ShapeDtypeStruct((SUBLANE, LANE), F32), jax.ShapeDtypeStruct((t, d), F32),
                         jax.ShapeDtypeStruct((1, d), F32)], grid=(t // tm,),
        in_specs=[row, vec, row], out_specs=[pl.BlockSpec((SUBLANE, LANE), lambda i: (0, 0)), row, vec], name=name,
        compiler_params=_cparams(("arbitrary",)))(x, w, target)


def _ffn_up(h, wg, wu, *, name):
    t, d = h.shape
    f = wg.shape[1]
    tm = _pick(t, (512, 256, 128))
    tn = _pick(f, (1408, 512, 256, 128))

    def body(h_ref, wg_ref, wu_ref, g_ref, u_ref, a_ref):
        hv = h_ref[...]
        g = _dg(hv, wg_ref[...], 1, 0, False)
        u = _dg(hv, wu_ref[...], 1, 0, False)
        g_ref[...] = g
        u_ref[...] = u
        a_ref[...] = (_silu(g) * u).astype(BF16)

    hs = pl.BlockSpec((tm, d), lambda j, i: (i, 0))
    ws = pl.BlockSpec((d, tn), lambda j, i: (0, j))
    os_ = pl.BlockSpec((tm, tn), lambda j, i: (i, j))
    return pl.pallas_call(
        body, out_shape=[jax.ShapeDtypeStruct((t, f), F32), jax.ShapeDtypeStruct((t, f), F32),
                         jax.ShapeDtypeStruct((t, f), BF16)], grid=(f // tn, t // tm),
        in_specs=[hs, ws, ws], out_specs=[os_, os_, os_], name=name,
        compiler_params=_cparams(("parallel", "parallel")))(h, wg, wu)


def _ffn_dact(dxo, wd, g, u, *, name):
    t, d = dxo.shape
    f = wd.shape[0]
    tm = _pick(t, (512, 256, 128))
    tn = _pick(f, (1408, 512, 256, 128))

    def body(dx_ref, wd_ref, g_ref, u_ref, dg_ref, du_ref):
        da = 0.5 * _dg(dx_ref[...], wd_ref[...], 1, 1, False)
        gv = g_ref[...]
        sg = _sigmoid(gv)
        dg_ref[...] = (da * u_ref[...] * (sg * (1.0 + gv * (1.0 - sg)))).astype(BF16)
        du_ref[...] = (da * (gv * sg)).astype(BF16)

    xs = pl.BlockSpec((tm, d), lambda j, i: (i, 0))
    ws = pl.BlockSpec((tn, d), lambda j, i: (j, 0))
    os_ = pl.BlockSpec((tm, tn), lambda j, i: (i, j))
    return pl.pallas_call(
        body, out_shape=[jax.ShapeDtypeStruct((t, f), BF16), jax.ShapeDtypeStruct((t, f), BF16)],
        grid=(f // tn, t // tm), in_specs=[xs, ws, os_, os_], out_specs=[os_, os_], name=name,
        compiler_params=_cparams(("parallel", "parallel")))(dxo, wd, g, u)


def _ffn_fwd(x, nw, wg, wu, wd, tag):
    h = _rms_fwd(x, nw, name=f"rms_fwd_{tag}")
    g, u, a = _ffn_up(h, wg, wu, name=f"ffn_up_{tag}")
    y = _mm(a, wd, "nn", res=x, scale=0.5, name=f"ffn_down_{tag}")
    return y, (x, h, g, u, a)


def _ffn_bwd(saved, nw, wg, wu, wd, dxo, tag):
    x, h, g, u, a = saved
    dg, du = _ffn_dact(dxo, wd, g, u, name=f"ffn_dact_{tag}")
    dwd = _mm(a, dxo, "tn", scale=0.5, name=f"ffn_dwd_{tag}")
    dwg = _mm(h, dg, "tn", name=f"ffn_dwg_{tag}")
    dwu = _mm(h, du, "tn", name=f"ffn_dwu_{tag}")
    dh = _mm(dg, wg, "nt", a2=du, b2=wu, name=f"ffn_dh_{tag}")
    dx, dnw = _rms_bwd(x, nw, dh, dxo, name=f"rms_bwd_{tag}")
    return dx, dnw, dwg, dwu, dwd


MIX_HP = 4


def _pairs(ref, bsz, part):
    return jnp.stack([ref[b, :, 512 * hh + HEAD * part:512 * hh + HEAD * (part + 1)]
                      for b in range(bsz) for hh in range(MIX_HP)])


def _hg_fwd(proj, lbs, nw, *, name):
    bsz, l, _ = proj.shape
    nc = l // CHUNK
    groups = HG_HEADS // MIX_HP

    def body(p_ref, lb_ref, nw_ref, y_ref, s_ref, st):
        @pl.when(pl.program_id(1) == 0)
        def _():
            st[...] = jnp.zeros_like(st)

        prev = st[...]
        lb = jnp.stack([lb_ref[:, HEAD * hh:HEAD * (hh + 1)] for _ in range(bsz) for hh in range(MIX_HP)])
        st_new, y = hg_chunk(prev, *[_pairs(p_ref, bsz, part) for part in range(4)], lb, nw_ref[...])
        s_ref[...] = prev.reshape(bsz, MIX_HP, HEAD, HEAD)
        st[...] = st_new
        for b in range(bsz):
            for hh in range(MIX_HP):
                y_ref[b, :, HEAD * hh:HEAD * (hh + 1)] = y[b * MIX_HP + hh]

    return pl.pallas_call(
        body,
        out_shape=[jax.ShapeDtypeStruct((bsz, l, HG_W), F32),
                   jax.ShapeDtypeStruct((bsz, HG_HEADS, nc, HEAD, HEAD), F32)],
        grid=(groups, nc),
        in_specs=[pl.BlockSpec((bsz, CHUNK, MIX_HP * 512), lambda g, c: (0, c, g)),
                  pl.BlockSpec((1, MIX_HP * HEAD), lambda g, c: (0, g)),
                  pl.BlockSpec((1, HEAD), lambda g, c: (0, 0))],
        out_specs=[pl.BlockSpec((bsz, CHUNK, MIX_HP * HEAD), lambda g, c: (0, c, g)),
                   pl.BlockSpec((bsz, MIX_HP, None, HEAD, HEAD), lambda g, c: (0, g, c, 0, 0))],
        scratch_shapes=[pltpu.VMEM((bsz * MIX_HP, HEAD, HEAD), F32)], name=name,
        compiler_params=_cparams(("parallel", "arbitrary")))(proj, lbs, nw)


def _hg_bwd(proj, lbs, nw, states, dy, *, name):
    bsz, l, _ = proj.shape
    nc = l // CHUNK
    groups = HG_HEADS // MIX_HP
    n = bsz * MIX_HP

    def body(p_ref, lb_ref, nw_ref, s_ref, dy_ref, dp_ref, dlb_ref, dnw_ref, dst):
        @pl.when(pl.program_id(1) == 0)
        def _():
            dst[...] = jnp.zeros_like(dst)
            dlb_ref[...] = jnp.zeros_like(dlb_ref)
            dnw_ref[...] = jnp.zeros_like(dnw_ref)

        lb = jnp.stack([lb_ref[:, HEAD * hh:HEAD * (hh + 1)] for _ in range(bsz) for hh in range(MIX_HP)])
        dy3 = jnp.stack([dy_ref[b, :, HEAD * hh:HEAD * (hh + 1)] for b in range(bsz) for hh in range(MIX_HP)])
        _, vjp = jax.vjp(hg_chunk, s_ref[...].reshape(n, HEAD, HEAD), *[_pairs(p_ref, bsz, part) for part in range(4)],
                         lb, nw_ref[...])
        d_st, dq, df, dv, dgl, dlb, dnw = vjp((dst[...], dy3))
        dst[...] = d_st
        for b in range(bsz):
            for hh in range(MIX_HP):
                for part, d in enumerate((dq, df, dv, dgl)):
                    dp_ref[b, :, 512 * hh + HEAD * part:512 * hh + HEAD * (part + 1)] = d[b * MIX_HP + hh]
        dlb_ref[...] += dlb.reshape(bsz, MIX_HP, 1, HEAD)
        dnw_ref[...] += dnw

    rev = lambda c: nc - 1 - c
    return pl.pallas_call(
        body,
        out_shape=[jax.ShapeDtypeStruct((bsz, l, 4 * HG_W), F32),
                   jax.ShapeDtypeStruct((bsz, HG_HEADS, 1, HEAD), F32),
                   jax.ShapeDtypeStruct((groups, 1, HEAD), F32)],
        grid=(groups, nc),
        in_specs=[pl.BlockSpec((bsz, CHUNK, MIX_HP * 512), lambda g, c: (0, rev(c), g)),
                  pl.BlockSpec((1, MIX_HP * HEAD), lambda g, c: (0, g)),
                  pl.BlockSpec((1, HEAD), lambda g, c: (0, 0)),
                  pl.BlockSpec((bsz, MIX_HP, None, HEAD, HEAD), lambda g, c: (0, g, rev(c), 0, 0)),
                  pl.BlockSpec((bsz, CHUNK, MIX_HP * HEAD), lambda g, c: (0, rev(c), g))],
        out_specs=[pl.BlockSpec((bsz, CHUNK, MIX_HP * 512), lambda g, c: (0, rev(c), g)),
                   pl.BlockSpec((bsz, MIX_HP, 1, HEAD), lambda g, c: (0, g, 0, 0)),
                   pl.BlockSpec((None, 1, HEAD), lambda g, c: (g, 0, 0))],
        scratch_shapes=[pltpu.VMEM((n, HEAD, HEAD), F32)], name=name,
        compiler_params=_cparams(("parallel", "arbitrary")))(proj, lbs, nw, states, dy)


def _head_masks(group, bsz):
    n = bsz * MIX_HP
    head = group * MIX_HP + _iota((n, 1, HEAD), 0) % MIX_HP
    lane = _iota((n, 1, HEAD), 2)
    return jnp.where(lane == head, 1.0, 0.0).astype(F32), jnp.where(lane == GDN_HEADS + head, 1.0, 0.0).astype(F32)


def _gdn_fwd(qkvg, bd, ab, dtb, nw, *, name):
    bsz, l, _ = qkvg.shape
    nc = l // CHUNK
    groups = GDN_HEADS // MIX_HP

    def body(p_ref, bd_ref, ab_ref, dtb_ref, nw_ref, y_ref, s_ref, st):
        @pl.when(pl.program_id(1) == 0)
        def _():
            st[...] = jnp.zeros_like(st)

        mb, ma = _head_masks(pl.program_id(0), bsz)
        prev = st[...]
        bd3 = jnp.stack([bd_ref[b] for b in range(bsz) for _ in range(MIX_HP)])
        st_new, y = gdn_chunk(prev, *[_pairs(p_ref, bsz, part) for part in range(4)], bd3, ab_ref[...],
                              dtb_ref[...], nw_ref[...], mb, ma)
        s_ref[...] = prev.reshape(bsz, MIX_HP, HEAD, HEAD)
        st[...] = st_new
        for b in range(bsz):
            for hh in range(MIX_HP):
                y_ref[b, :, HEAD * hh:HEAD * (hh + 1)] = y[b * MIX_HP + hh]

    vec = pl.BlockSpec((1, HEAD), lambda g, c: (0, 0))
    return pl.pallas_call(
        body,
        out_shape=[jax.ShapeDtypeStruct((bsz, l, GDN_W), F32),
                   jax.ShapeDtypeStruct((bsz, GDN_HEADS, nc, HEAD, HEAD), F32)],
        grid=(groups, nc),
        in_specs=[pl.BlockSpec((bsz, CHUNK, MIX_HP * 512), lambda g, c: (0, c, g)),
                  pl.BlockSpec((bsz, CHUNK, HEAD), lambda g, c: (0, c, 0)), vec, vec, vec],
        out_specs=[pl.BlockSpec((bsz, CHUNK, MIX_HP * HEAD), lambda g, c: (0, c, g)),
                   pl.BlockSpec((bsz, MIX_HP, None, HEAD, HEAD), lambda g, c: (0, g, c, 0, 0))],
        scratch_shapes=[pltpu.VMEM((bsz * MIX_HP, HEAD, HEAD), F32)], name=name,
        compiler_params=_cparams(("parallel", "arbitrary")))(qkvg, bd, ab, dtb, nw)


def _gdn_bwd(qkvg, bd, ab, dtb, nw, states, dy, *, name):
    bsz, l, _ = qkvg.shape
    nc = l // CHUNK
    groups = GDN_HEADS // MIX_HP
    n = bsz * MIX_HP

    def body(p_ref, bd_ref, ab_ref, dtb_ref, nw_ref, s_ref, dy_ref, dp_ref, dbd_ref, dab_ref, ddt_ref, dnw_ref,
             dst):
        @pl.when(pl.program_id(1) == 0)
        def _():
            dst[...] = jnp.zeros_like(dst)
            dab_ref[...] = jnp.zeros_like(dab_ref)
            ddt_ref[...] = jnp.zeros_like(ddt_ref)
            dnw_ref[...] = jnp.zeros_like(dnw_ref)

        mb, ma = _head_masks(pl.program_id(0), bsz)
        fn = lambda st, q, k, v, gt, bdv, abv, dtv, nwv: gdn_chunk(st, q, k, v, gt, bdv, abv, dtv, nwv, mb, ma)
        bd3 = jnp.stack([bd_ref[b] for b in range(bsz) for _ in range(MIX_HP)])
        dy3 = jnp.stack([dy_ref[b, :, HEAD * hh:HEAD * (hh + 1)] for b in range(bsz) for hh in range(MIX_HP)])
        _, vjp = jax.vjp(fn, s_ref[...].reshape(n, HEAD, HEAD), *[_pairs(p_ref, bsz, part) for part in range(4)], bd3,
                         ab_ref[...], dtb_ref[...], nw_ref[...])
        d_st, dq, dk, dv, dgt, dbd, dab, ddt, dnw = vjp((dst[...], dy3))
        dst[...] = d_st
        for b in range(bsz):
            for hh in range(MIX_HP):
                for part, d in enumerate((dq, dk, dv, dgt)):
                    dp_ref[b, :, 512 * hh + HEAD * part:512 * hh + HEAD * (part + 1)] = d[b * MIX_HP + hh]
        dbd_ref[...] = jnp.sum(dbd.reshape(bsz, MIX_HP, CHUNK, HEAD), axis=1)
        dab_ref[...] += dab
        ddt_ref[...] += ddt
        dnw_ref[...] += dnw

    rev = lambda c: nc - 1 - c
    vec = pl.BlockSpec((1, HEAD), lambda g, c: (0, 0))
    acc = pl.BlockSpec((None, 1, HEAD), lambda g, c: (g, 0, 0))
    acc_shape = jax.ShapeDtypeStruct((groups, 1, HEAD), F32)
    return pl.pallas_call(
        body,
        out_shape=[jax.ShapeDtypeStruct((bsz, l, 4 * GDN_W), F32),
                   jax.ShapeDtypeStruct((bsz, groups, l, HEAD), F32), acc_shape, acc_shape, acc_shape],
        grid=(groups, nc),
        in_specs=[pl.BlockSpec((bsz, CHUNK, MIX_HP * 512), lambda g, c: (0, rev(c), g)),
                  pl.BlockSpec((bsz, CHUNK, HEAD), lambda g, c: (0, rev(c), 0)), vec, vec, vec,
                  pl.BlockSpec((bsz, MIX_HP, None, HEAD, HEAD), lambda g, c: (0, g, rev(c), 0, 0)),
                  pl.BlockSpec((bsz, CHUNK, MIX_HP * HEAD), lambda g, c: (0, rev(c), g))],
        out_specs=[pl.BlockSpec((bsz, CHUNK, MIX_HP * 512), lambda g, c: (0, rev(c), g)),
                   pl.BlockSpec((bsz, None, CHUNK, HEAD), lambda g, c: (0, g, rev(c), 0)), acc, acc, acc],
        scratch_shapes=[pltpu.VMEM((n, HEAD, HEAD), F32)], name=name,
        compiler_params=_cparams(("parallel", "arbitrary")))(qkvg, bd, ab, dtb, nw, states, dy)


CONV_TL = 256
QKV_LANES = 3 * HEAD


def _conv_fwd(x, w, *, name):
    bsz, l, wd = x.shape
    tl = min(CONV_TL, l)
    nr = l // tl
    hb = tl // SUBLANE

    def body(x_ref, xp_ref, w_ref, o_ref):
        xv = x_ref[...]
        halo = jnp.where(pl.program_id(2) > 0, xp_ref[...], 0.0)
        xc = jnp.concatenate([halo, xv], axis=0)
        wv = w_ref[...]
        z = wv[CONV_W - 1:CONV_W] * xv
        for j in range(CONV_W - 1):
            z = z + wv[j:j + 1] * pltpu.roll(xc, CONV_W - 1 - j, 0)[SUBLANE:]
        o_ref[...] = jnp.where(_iota((tl, 512), 1) < QKV_LANES, _silu(z), xv)

    return pl.pallas_call(
        body, out_shape=jax.ShapeDtypeStruct(x.shape, F32), grid=(bsz, wd // 512, nr),
        in_specs=[pl.BlockSpec((None, tl, 512), lambda b, h, r: (b, r, h)),
                  pl.BlockSpec((None, SUBLANE, 512), lambda b, h, r: (b, jnp.maximum(r * hb - 1, 0), h)),
                  pl.BlockSpec((SUBLANE, 512), lambda b, h, r: (0, h))],
        out_specs=pl.BlockSpec((None, tl, 512), lambda b, h, r: (b, r, h)), name=name,
        compiler_params=_cparams(("parallel", "parallel", "arbitrary")))(x, x, w)


def _conv_bwd(x, w, dy, *, name):
    bsz, l, wd = x.shape
    tl = min(CONV_TL, l)
    nr = l // tl
    hb = tl // SUBLANE

    def body(x_ref, xp_ref, xn_ref, w_ref, dy_ref, dyn_ref, dx_ref, dw_ref):
        r = pl.program_id(2)
        xv = x_ref[...]
        prev = jnp.where(r > 0, xp_ref[...], 0.0)
        last = r == nr - 1
        nxt = jnp.where(last, 0.0, xn_ref[...])
        dyn = jnp.where(last, 0.0, dyn_ref[...])
        xc = jnp.concatenate([prev, xv, nxt], axis=0)
        wv = w_ref[...]
        shifted = [pltpu.roll(xc, CONV_W - 1 - j, 0) for j in range(CONV_W - 1)] + [xc]
        z = wv[0:1] * shifted[0]
        for j in range(1, CONV_W):
            z = z + wv[j:j + 1] * shifted[j]
        z = z[SUBLANE:]
        sg = _sigmoid(z)
        dyc = jnp.concatenate([dy_ref[...], dyn], axis=0)
        dz = dyc * (sg * (1.0 + z * (1.0 - sg)))
        dx = wv[CONV_W - 1:CONV_W] * dz[:tl]
        for j in range(CONV_W - 1):
            s = CONV_W - 1 - j
            dx = dx + wv[j:j + 1] * pltpu.roll(dz, tl + SUBLANE - s, 0)[:tl]
        conv_lane = _iota((tl, 512), 1) < QKV_LANES
        dx_ref[...] = jnp.where(conv_lane, dx, dy_ref[...])

        @pl.when(r == 0)
        def _():
            dw_ref[...] = jnp.zeros_like(dw_ref)

        dzt = dz[:tl]
        rows = [jnp.sum(dzt * shifted[j][SUBLANE:SUBLANE + tl], axis=0, keepdims=True) for j in range(CONV_W)]
        rows.append(jnp.zeros((SUBLANE - CONV_W, 512), F32))
        dw_ref[...] += jnp.concatenate(rows, axis=0)

    blk = pl.BlockSpec((None, tl, 512), lambda b, h, r: (b, r, h))
    prev = pl.BlockSpec((None, SUBLANE, 512), lambda b, h, r: (b, jnp.maximum(r * hb - 1, 0), h))
    nxt = pl.BlockSpec((None, SUBLANE, 512), lambda b, h, r: (b, jnp.minimum((r + 1) * hb, l // SUBLANE - 1), h))
    return pl.pallas_call(
        body, out_shape=[jax.ShapeDtypeStruct(x.shape, F32), jax.ShapeDtypeStruct((bsz, SUBLANE, wd), F32)],
        grid=(bsz, wd // 512, nr),
        in_specs=[blk, prev, nxt, pl.BlockSpec((SUBLANE, 512), lambda b, h, r: (0, h)), blk, nxt],
        out_specs=[blk, pl.BlockSpec((None, SUBLANE, 512), lambda b, h, r: (b, 0, h))], name=name,
        compiler_params=_cparams(("parallel", "parallel", "arbitrary")))(x, x, x, w, dy, dy)


S5_LC = 256
S5_TN = 512


def _s5_scan(xr, xi, pw, *, reverse, name):
    bsz, l, n = xr.shape
    lc = min(S5_LC, l)
    nc = l // lc
    steps = int(math.log2(lc))

    def shift(v, s):
        if reverse:
            if s >= SUBLANE:
                return jnp.concatenate([v[s:], jnp.zeros((s, v.shape[1]), F32)], axis=0)
            return jnp.where(_iota(v.shape, 0) < lc - s, pltpu.roll(v, lc - s, 0), 0.0)
        if s >= SUBLANE:
            return jnp.concatenate([jnp.zeros((s, v.shape[1]), F32), v[:lc - s]], axis=0)
        return jnp.where(_iota(v.shape, 0) >= s, pltpu.roll(v, s, 0), 0.0)

    def body(xr_ref, xi_ref, pw_ref, hr_ref, hi_ref, cr, ci):
        @pl.when(pl.program_id(2) == 0)
        def _():
            cr[...] = jnp.zeros_like(cr)
            ci[...] = jnp.zeros_like(ci)

        vr, vi = xr_ref[...], xi_ref[...]
        for k in range(steps):
            s = 1 << k
            ar, ai = pw_ref[0, s - 1:s, :], pw_ref[1, s - 1:s, :]
            sr, si = shift(vr, s), shift(vi, s)
            vr, vi = vr + ar * sr - ai * si, vi + ar * si + ai * sr
        pr, pi = pw_ref[2], pw_ref[3]
        c_r, c_i = cr[...], ci[...]
        vr, vi = vr + pr * c_r - pi * c_i, vi + pr * c_i + pi * c_r
        hr_ref[...] = vr
        hi_ref[...] = vi
        edge = 0 if reverse else lc - 1
        cr[...] = vr[edge:edge + 1]
        ci[...] = vi[edge:edge + 1]

    tmap = (lambda c: nc - 1 - c) if reverse else (lambda c: c)
    blk = pl.BlockSpec((None, lc, S5_TN), lambda b, j, c: (b, tmap(c), j))
    return pl.pallas_call(
        body, out_shape=[jax.ShapeDtypeStruct(xr.shape, F32), jax.ShapeDtypeStruct(xr.shape, F32)],
        grid=(bsz, n // S5_TN, nc),
        in_specs=[blk, blk, pl.BlockSpec((4, lc, S5_TN), lambda b, j, c: (0, 0, j))], out_specs=[blk, blk],
        scratch_shapes=[pltpu.VMEM((1, S5_TN), F32), pltpu.VMEM((1, S5_TN), F32)], name=name,
        compiler_params=_cparams(("parallel", "parallel", "arbitrary")))(xr, xi, pw)


def _s5_dabar(gr, gi, hr, hi, *, name):
    bsz, l, n = gr.shape
    lc = min(S5_LC, l)
    nc = l // lc
    hb = lc // SUBLANE

    def body(gr_ref, gi_ref, hr_ref, hi_ref, hrp_ref, hip_ref, o_ref):
        c = pl.program_id(2)

        @pl.when(c == 0)
        def _():
            o_ref[...] = jnp.zeros_like(o_ref)

        def prev(h_ref, hp_ref):
            first = jnp.where(c > 0, hp_ref[SUBLANE - 1:SUBLANE, :], 0.0)
            return jnp.where(_iota((lc, S5_TN), 0) == 0, first, pltpu.roll(h_ref[...], 1, 0))

        pr, pi = prev(hr_ref, hrp_ref), prev(hi_ref, hip_ref)
        g_r, g_i = gr_ref[...], gi_ref[...]
        d_re = jnp.sum(g_r * pr + g_i * pi, axis=0, keepdims=True)
        d_im = jnp.sum(g_i * pr - g_r * pi, axis=0, keepdims=True)
        o_ref[...] += jnp.concatenate([d_re, d_im, jnp.zeros((SUBLANE - 2, S5_TN), F32)], axis=0)

    blk = pl.BlockSpec((None, lc, S5_TN), lambda b, j, c: (b, c, j))
    prv = pl.BlockSpec((None, SUBLANE, S5_TN), lambda b, j, c: (b, jnp.maximum(c * hb - 1, 0), j))
    return pl.pallas_call(
        body, out_shape=jax.ShapeDtypeStruct((bsz, SUBLANE, n), F32), grid=(bsz, n // S5_TN, nc),
        in_specs=[blk, blk, blk, blk, prv, prv],
        out_specs=pl.BlockSpec((None, SUBLANE, S5_TN), lambda b, j, c: (b, 0, j)), name=name,
        compiler_params=_cparams(("parallel", "parallel", "arbitrary")))(gr, gi, hr, hi, hr, hi)


def _s5_post_fwd(ych, u, d, wglu, *, name):
    t, w = ych.shape
    tm = _pick(t, (512, 256, 128))

    def body(y_ref, u_ref, d_ref, w_ref, o_ref):
        o_ref[...] = s5_post(y_ref[...], u_ref[...], d_ref[...], w_ref[...])

    row = pl.BlockSpec((tm, w), lambda i: (i, 0))
    return pl.pallas_call(
        body, out_shape=jax.ShapeDtypeStruct((t, w), F32), grid=(t // tm,),
        in_specs=[row, row, pl.BlockSpec((1, w), lambda i: (0, 0)), pl.BlockSpec((w, w), lambda i: (0, 0))],
        out_specs=row, name=name, compiler_params=_cparams(("parallel",)))(ych, u, d, wglu)


def _s5_post_bwd(ych, u, d, wglu, dout, *, name):
    t, w = ych.shape
    tm = _pick(t, (512, 256, 128))

    def body(y_ref, u_ref, d_ref, w_ref, do_ref, dy_ref, du_ref, dd_ref, dw_ref):
        _, vjp = jax.vjp(s5_post, y_ref[...], u_ref[...], d_ref[...], w_ref[...])
        dy, du, dd, dw = vjp(do_ref[...])
        dy_ref[...] = dy
        du_ref[...] = du

        @pl.when(pl.program_id(0) == 0)
        def _():
            dd_ref[...] = jnp.zeros_like(dd_ref)
            dw_ref[...] = jnp.zeros_like(dw_ref)

        dd_ref[...] += dd
        dw_ref[...] += dw

    row = pl.BlockSpec((tm, w), lambda i: (i, 0))
    vec = pl.BlockSpec((1, w), lambda i: (0, 0))
    mat = pl.BlockSpec((w, w), lambda i: (0, 0))
    return pl.pallas_call(
        body, out_shape=[jax.ShapeDtypeStruct((t, w), F32), jax.ShapeDtypeStruct((t, w), F32),
                         jax.ShapeDtypeStruct((1, w), F32), jax.ShapeDtypeStruct((w, w), F32)], grid=(t // tm,),
        in_specs=[row, row, vec, mat, row], out_specs=[row, row, vec, mat], name=name,
        compiler_params=_cparams(("arbitrary",)))(ych, u, d, wglu, dout)


def _s5_disc(a_re, a_im, b_re, b_im, log_dt):
    dt = jnp.exp(log_dt)[:, None]
    mag = jnp.exp(dt * a_re)
    ang = dt * a_im
    abar_re = mag * jnp.cos(ang)
    abar_im = mag * jnp.sin(ang)
    den = a_re * a_re + a_im * a_im
    zr = abar_re - 1.0
    zi = abar_im
    coef_re = ((zr * a_re + zi * a_im) / den)[..., None]
    coef_im = ((zi * a_re - zr * a_im) / den)[..., None]
    return abar_re, abar_im, coef_re * b_re - coef_im * b_im, coef_re * b_im + coef_im * b_re


def _s5_powers(a_re, a_im, log_dt, lc):
    dt = jnp.exp(log_dt)[:, None]
    n = jnp.arange(1, lc + 1, dtype=F32)[:, None, None]
    mag = jnp.exp(n * (dt * a_re)[None])
    ang = n * (dt * a_im)[None]
    pr = (mag * jnp.cos(ang)).reshape(lc, S5_N)
    pi = (mag * jnp.sin(ang)).reshape(lc, S5_N)
    fwd = jnp.stack([pr, pi, pr, pi])
    bwd = jnp.stack([pr, -pi, pr[::-1], -pi[::-1]])
    return fwd, bwd


def _block_diag_in(bb):
    eye = jnp.eye(S5_GROUPS, dtype=bb.dtype)
    return jnp.einsum("gnp,gh->gphn", bb, eye).reshape(S5_W, S5_N)


def _block_diag_out(cc):
    eye = jnp.eye(S5_GROUPS, dtype=cc.dtype)
    return jnp.einsum("gpn,gh->gnhp", cc, eye).reshape(S5_N, S5_W)


def _adamw(w, g, m, v, *, name):
    shape = w.shape
    cols = shape[-1] if w.ndim > 1 else shape[0]
    rows = w.size // cols
    tr = _pick(rows, (512, 352, 256, 128, 64, 32, 16, 8)) if rows % SUBLANE == 0 else rows
    flat = lambda t: t.reshape(rows, cols)

    def body(w_ref, g_ref, m_ref, v_ref, d_ref, nm_ref, nv_ref):
        gv = g_ref[...]
        mn = ADAM_B1 * m_ref[...] + (1.0 - ADAM_B1) * gv
        vn = ADAM_B2 * v_ref[...] + (1.0 - ADAM_B2) * jnp.square(gv)
        m_hat = mn / (1.0 - ADAM_B1 ** ADAM_STEP)
        v_hat = vn / (1.0 - ADAM_B2 ** ADAM_STEP)
        d_ref[...] = -ADAM_LR * (m_hat / (jnp.sqrt(v_hat) + ADAM_EPS) + ADAM_WD * w_ref[...])
        nm_ref[...] = mn
        nv_ref[...] = vn

    blk = pl.BlockSpec((tr, cols), lambda i: (i, 0))
    sds = jax.ShapeDtypeStruct((rows, cols), F32)
    outs = pl.pallas_call(body, out_shape=[sds, sds, sds], grid=(rows // tr,), in_specs=[blk] * 4,
                          out_specs=[blk] * 3, name=name,
                          compiler_params=_cparams(("parallel",)))(flat(w), flat(g), flat(m), flat(v))
    return tuple(o.reshape(shape) for o in outs)


ANY = pl.BlockSpec(memory_space=pl.ANY)


def _coords():
    x, y, c = lax.axis_index("x"), lax.axis_index("y"), lax.axis_index("c")
    chips = [(1 - x, y), (x, 1 - y), (1 - x, 1 - y)]
    return x, y, c, chips


def _all_gather_chips(packs, *, name):
    n = len(packs)

    def body(*refs):
        p_refs, o_refs, (send, recv) = refs[:n], refs[n:2 * n], refs[2 * n:]
        x, y, c, chips = _coords()
        sibling = (x, y, 1 - c)

        def copy(i, k, src, dst, to):
            return pltpu.make_async_remote_copy(src_ref=src, dst_ref=dst, send_sem=send.at[6 * i + k],
                                                recv_sem=recv.at[6 * i + k], device_id=to, device_id_type=MESH_ID)

        def part(i, chip, half):
            return o_refs[i].at[2 * chip[0] + chip[1], half]

        first = [copy(i, k, p_refs[i].at[c], part(i, (x, y), c), (*chip, c))
                 for i in range(n) for k, chip in enumerate(chips)]
        for cp in first:
            cp.start()
        passed = []
        for i in range(n):
            for k, chip in enumerate(chips):
                copy(i, k, part(i, chip, c), part(i, chip, c), (x, y, c)).wait_recv()
                passed.append(copy(i, 3 + k, part(i, chip, c), part(i, chip, c), sibling))
                passed[-1].start()
        for i in range(n):
            for k, chip in enumerate(chips):
                copy(i, 3 + k, part(i, chip, 1 - c), part(i, chip, 1 - c), (x, y, c)).wait_recv()
        for cp in first + passed:
            cp.wait_send()

    return pl.pallas_call(
        body, out_shape=[jax.ShapeDtypeStruct((N_CHIPS,) + p.shape, p.dtype) for p in packs], in_specs=[ANY] * n,
        out_specs=[ANY] * n,
        scratch_shapes=[pltpu.SemaphoreType.DMA((6 * n,)), pltpu.SemaphoreType.DMA((6 * n,))], name=name)(*packs)


def _swap_halves(gs, *, name):
    n = len(gs)

    def body(*refs):
        g_refs, o_refs, (send, recv) = refs[:n], refs[n:2 * n], refs[2 * n:]
        x, y, c, _ = _coords()
        cps = [pltpu.make_async_remote_copy(src_ref=g_refs[i].at[:, 1 - c], dst_ref=o_refs[i], send_sem=send.at[i],
                                            recv_sem=recv.at[i], device_id=(x, y, 1 - c), device_id_type=MESH_ID)
               for i in range(n)]
        for cp in cps:
            cp.start()
        for cp in cps:
            cp.wait()

    return pl.pallas_call(
        body, out_shape=[jax.ShapeDtypeStruct(g.shape[:1] + g.shape[2:], g.dtype) for g in gs], in_specs=[ANY] * n,
        out_specs=[ANY] * n, scratch_shapes=[pltpu.SemaphoreType.DMA((n,)), pltpu.SemaphoreType.DMA((n,))],
        name=name)(*gs)


def _add_sibling(g, got, *, name):
    n, _, m, r, w = g.shape

    def body(g0_ref, g1_ref, r_ref, a_ref, own_ref):
        c = lax.axis_index("c")
        mine = 2 * lax.axis_index("x") + lax.axis_index("y")
        j = pl.program_id(1)
        s = jnp.where(c == 0, g0_ref[...], g1_ref[...]) + r_ref[...]
        a_ref[...] = s.astype(BF16)

        @pl.when(j == 0)
        def _():
            own_ref[...] = jnp.zeros_like(own_ref)

        own_ref[...] = jnp.where(j == mine, s, own_ref[...])

    return pl.pallas_call(
        body, out_shape=[jax.ShapeDtypeStruct((n, m, r, w), BF16), jax.ShapeDtypeStruct((m, r, w), F32)],
        grid=(m, n),
        in_specs=[pl.BlockSpec((None, None, None, r, w), lambda e, j: (j, 0, e, 0, 0)),
                  pl.BlockSpec((None, None, None, r, w), lambda e, j: (j, 1, e, 0, 0)),
                  pl.BlockSpec((None, None, r, w), lambda e, j: (j, e, 0, 0))],
        out_specs=[pl.BlockSpec((None, None, r, w), lambda e, j: (j, e, 0, 0)),
                   pl.BlockSpec((None, r, w), lambda e, j: (e, 0, 0))],
        name=name, compiler_params=_cparams(("parallel", "arbitrary")))(g, g, got)


def _exchange_chips(a16s, *, name):
    n = len(a16s)

    def body(*refs):
        a_refs, o_refs, (send, recv) = refs[:n], refs[n:2 * n], refs[2 * n:]
        x, y, c, chips = _coords()
        cps = [pltpu.make_async_remote_copy(src_ref=a_refs[i].at[2 * chip[0] + chip[1]], dst_ref=o_refs[i].at[k],
                                            send_sem=send.at[3 * i + k], recv_sem=recv.at[3 * i + k],
                                            device_id=(*chip, c), device_id_type=MESH_ID)
               for i in range(n) for k, chip in enumerate(chips)]
        for cp in cps:
            cp.start()
        for cp in cps:
            cp.wait()

    return pl.pallas_call(
        body, out_shape=[jax.ShapeDtypeStruct((N_CHIPS - 1,) + a.shape[1:], a.dtype) for a in a16s],
        in_specs=[ANY] * n, out_specs=[ANY] * n,
        scratch_shapes=[pltpu.SemaphoreType.DMA((3 * n,)), pltpu.SemaphoreType.DMA((3 * n,))], name=name)(*a16s)


def _add_chips(own, got, *, name):
    m, r, w = own.shape

    def body(o_ref, g_ref, s_ref):
        s = o_ref[...]
        for k in range(N_CHIPS - 1):
            s = s + g_ref[k].astype(F32)
        s_ref[...] = s

    return pl.pallas_call(
        body, out_shape=jax.ShapeDtypeStruct((m, r, w), F32), grid=(m,),
        in_specs=[pl.BlockSpec((None, r, w), lambda i: (i, 0, 0)),
                  pl.BlockSpec((N_CHIPS - 1, None, r, w), lambda i: (0, i, 0, 0))],
        out_specs=pl.BlockSpec((None, r, w), lambda i: (i, 0, 0)), name=name,
        compiler_params=_cparams(("parallel",)))(own, got)


def _swap_reduced(halves, *, name):
    n = len(halves)

    def body(*refs):
        h_refs, o_refs, (send, recv) = refs[:n], refs[n:2 * n], refs[2 * n:]
        x, y, c, _ = _coords()
        cps = [pltpu.make_async_remote_copy(src_ref=h_refs[i], dst_ref=o_refs[i], send_sem=send.at[i],
                                            recv_sem=recv.at[i], device_id=(x, y, 1 - c), device_id_type=MESH_ID)
               for i in range(n)]
        for cp in cps:
            cp.start()
        for cp in cps:
            cp.wait()

    return pl.pallas_call(
        body, out_shape=[jax.ShapeDtypeStruct(h.shape, h.dtype) for h in halves], in_specs=[ANY] * n,
        out_specs=[ANY] * n, scratch_shapes=[pltpu.SemaphoreType.DMA((n,)), pltpu.SemaphoreType.DMA((n,))],
        name=name)(*halves)


def _gather_all(vec, *, name):
    s, w = vec.shape

    def body(v_ref, o_ref, send, recv, local):
        x, y, c, _ = _coords()
        me = 4 * x + 2 * y + c
        mine = pltpu.make_async_copy(v_ref, o_ref.at[me], local)
        mine.start()
        cps = []
        for k in range(1, N_DEV):
            px = 1 - x if k & 4 else x
            py = 1 - y if k & 2 else y
            pc = 1 - c if k & 1 else c
            peer = 4 * px + 2 * py + pc
            cps.append((pltpu.make_async_remote_copy(src_ref=v_ref, dst_ref=o_ref.at[me], send_sem=send.at[k - 1],
                                                     recv_sem=recv.at[k - 1], device_id=(px, py, pc),
                                                     device_id_type=MESH_ID),
                        pltpu.make_async_remote_copy(src_ref=v_ref, dst_ref=o_ref.at[peer], send_sem=send.at[k - 1],
                                                     recv_sem=recv.at[k - 1], device_id=(px, py, pc),
                                                     device_id_type=MESH_ID)))
        for snd, _ in cps:
            snd.start()
        for snd, rcv in cps:
            rcv.wait_recv()
            snd.wait_send()
        mine.wait()

    return pl.pallas_call(
        body, out_shape=jax.ShapeDtypeStruct((N_DEV, s, w), vec.dtype), in_specs=[ANY], out_specs=ANY,
        scratch_shapes=[pltpu.SemaphoreType.DMA((N_DEV - 1,)), pltpu.SemaphoreType.DMA((N_DEV - 1,)),
                        pltpu.SemaphoreType.DMA], name=name)(vec)


def _sum_all(parts, *, name):
    n, s, w = parts.shape
    tr = _pick(s, (96, 72, 48, 32, 24, 16, 8))

    def body(p_ref, o_ref):
        acc = p_ref[0]
        for k in range(1, n):
            acc = acc + p_ref[k]
        o_ref[...] = acc

    return pl.pallas_call(
        body, out_shape=jax.ShapeDtypeStruct((s, w), F32), grid=(s // tr,),
        in_specs=[pl.BlockSpec((n, tr, w), lambda i: (0, i, 0))], out_specs=pl.BlockSpec((tr, w), lambda i: (i, 0)),
        name=name, compiler_params=_cparams(("parallel",)))(parts)


def _rows_pack(pieces, lead):
    rows = [p.reshape(p.shape[:lead] + (-1, PACK_COLS)) for p in pieces]
    spare = 2 * ROW_BLOCKS * ROW_BLOCK - sum(r.shape[lead] for r in rows)
    rows.append(jnp.zeros(rows[0].shape[:lead] + (spare, PACK_COLS), rows[0].dtype))
    return jnp.concatenate(rows, axis=lead).reshape(rows[0].shape[:lead] + (2, ROW_BLOCKS, ROW_BLOCK, PACK_COLS))


def _rows_unpack(packed, shapes, lead):
    flat = packed.reshape(packed.shape[:lead] + (-1, PACK_COLS))
    out, off = [], 0
    for shp in shapes:
        rows = int(np.prod(shp)) // PACK_COLS
        out.append(lax.slice_in_dim(flat, off, off + rows, axis=lead).reshape(packed.shape[:lead] + tuple(shp)))
        off += rows
    return out


def _to_shards(full, axis):
    return jnp.stack(jnp.split(full, N_CHIPS, axis=axis))


def _by_head(w, parts, heads):
    r = w.shape[0]
    return w.reshape(r, parts, heads, HEAD).transpose(0, 2, 1, 3).reshape(r, parts * heads * HEAD)


def _by_part(w, parts, heads):
    r = w.shape[0]
    return w.reshape(r, heads, parts, HEAD).transpose(0, 2, 1, 3).reshape(r, parts * heads * HEAD)


def kernel(x, ffn1_norm, ffn1_w_gate, ffn1_w_up, ffn1_w_down, mix_norm, ffn2_norm, ffn2_w_gate, ffn2_w_up, ffn2_w_down, ev_w_in, hg_lb_logits, hg_norm_w, s5_a_re, s5_a_im, s5_b_re, s5_b_im, s5_c_re, s5_c_im, s5_d, s5_log_dt, s5_w_glu, ev_w_out, od_w_in, gdn_conv_w, gdn_a_log, gdn_dt_bias, gdn_norm_w, od_w_out, final_norm, loss_target, m_ffn1_norm, m_ffn1_w_gate, m_ffn1_w_up, m_ffn1_w_down, m_mix_norm, m_ffn2_norm, m_ffn2_w_gate, m_ffn2_w_up, m_ffn2_w_down, m_ev_w_in, m_hg_lb_logits, m_hg_norm_w, m_s5_a_re, m_s5_a_im, m_s5_b_re, m_s5_b_im, m_s5_c_re, m_s5_c_im, m_s5_d, m_s5_log_dt, m_s5_w_glu, m_ev_w_out, m_od_w_in, m_gdn_conv_w, m_gdn_a_log, m_gdn_dt_bias, m_gdn_norm_w, m_od_w_out, m_final_norm, v_ffn1_norm, v_ffn1_w_gate, v_ffn1_w_up, v_ffn1_w_down, v_mix_norm, v_ffn2_norm, v_ffn2_w_gate, v_ffn2_w_up, v_ffn2_w_down, v_ev_w_in, v_hg_lb_logits, v_hg_norm_w, v_s5_a_re, v_s5_a_im, v_s5_b_re, v_s5_b_im, v_s5_c_re, v_s5_c_im, v_s5_d, v_s5_log_dt, v_s5_w_glu, v_ev_w_out, v_od_w_in, v_gdn_conv_w, v_gdn_a_log, v_gdn_dt_bias, v_gdn_norm_w, v_od_w_out, v_final_norm):
    args = locals()
    wts = {n: args[n] for n in WEIGHTS}
    mom = {n: args["m_" + n] for n in WEIGHTS}
    var = {n: args["v_" + n] for n in WEIGHTS}
    bsz, seq, _ = x.shape
    t = bsz * seq

    def wire(n):
        if n == "gdn_conv_w":
            return lax.bitcast_convert_type(wts[n], BF16)
        return wts[n].astype(BF16)

    chip = 2 * lax.axis_index("x") + lax.axis_index("y")
    is_own = (jnp.arange(N_CHIPS) == chip).reshape(N_CHIPS, 1, 1, 1, 1)
    n_wide = wts[WIDE[0]].shape[0]
    pack_a = jnp.concatenate([wts[n].astype(BF16) for n in WIDE], axis=0)
    pack_a = pack_a.reshape((2, len(WIDE) * n_wide // 2) + pack_a.shape[1:])
    pack_b = _rows_pack([wire(n) for n, _ in ROWS], 0)
    got_a, got_b = _all_gather_chips([pack_a, pack_b], name="gather_weights")
    got_a = jnp.where(is_own, pack_a[None], got_a).reshape((N_CHIPS, len(WIDE) * n_wide) + pack_a.shape[2:])
    got_b = jnp.where(is_own, pack_b[None], got_b)
    full = {}
    for k, n in enumerate(WIDE):
        full[n] = [jnp.concatenate([got_a[j, k * n_wide + i] for j in range(N_CHIPS)], axis=1) for i in range(n_wide)]
    for (n, axis), piece in zip(ROWS, _rows_unpack(got_b, [wire(n).shape for n, _ in ROWS], 1)):
        if n == "gdn_conv_w":
            piece = lax.bitcast_convert_type(piece, F32)
        full[n] = [jnp.concatenate([piece[j, i] for j in range(N_CHIPS)], axis=axis - 1)
                   for i in range(piece.shape[1])]

    ev_cols = lambda w: jnp.concatenate([_by_head(w[:, :4 * HG_W], 4, HG_HEADS), w[:, 4 * HG_W:]], axis=1)
    ev_cols_back = lambda w: jnp.concatenate([_by_part(w[:, :4 * HG_W], 4, HG_HEADS), w[:, 4 * HG_W:]], axis=1)

    lbs_fn = lambda lg: (lambda p: jnp.cumsum(p, axis=0) - p[0])(jax.nn.softmax(lg, axis=0))
    lbs, lbs_vjp = jax.vjp(lbs_fn, hg_lb_logits)
    lc = min(S5_LC, seq)

    xs = x.reshape(t, D_MODEL)
    saved = []
    for layer in range(DEPTH):
        j = layer // 2
        rec = {}
        xs, rec["ffn1"] = _ffn_fwd(xs, ffn1_norm[layer][None], full["ffn1_w_gate"][layer], full["ffn1_w_up"][layer],
                                   full["ffn1_w_down"][layer], f"a{layer}")
        rec["x_mix"] = xs
        h = _rms_fwd(xs, mix_norm[layer][None], name=f"rms_fwd_mix{layer}")
        rec["h"] = h
        if layer % 2 == 0:
            w_in = ev_cols(full["ev_w_in"][j])
            proj = _mm(h, w_in, "nn", name=f"ev_proj{layer}").reshape(bsz, seq, -1)
            rec["proj"] = proj
            y_a, rec["hg_states"] = _hg_fwd(proj, lbs[j][None], hg_norm_w[j][None], name=f"hgrn2_fwd{layer}")
            disc, rec["disc_vjp"] = jax.vjp(_s5_disc, s5_a_re[j], s5_a_im[j], s5_b_re[j], s5_b_im[j], s5_log_dt[j])
            wb_re, wb_im = _block_diag_in(disc[2]).astype(BF16), _block_diag_in(disc[3]).astype(BF16)
            wc_re = _block_diag_out(s5_c_re[j]).astype(BF16)
            wc_im = _block_diag_out(-s5_c_im[j]).astype(BF16)
            pw_f, pw_b = _s5_powers(s5_a_re[j], s5_a_im[j], s5_log_dt[j], lc)
            u = proj[:, :, 4 * HG_W:].reshape(t, S5_W)
            bu_re = _mm(u, wb_re, "nn", name=f"s5_bu_re{layer}").reshape(bsz, seq, S5_N)
            bu_im = _mm(u, wb_im, "nn", name=f"s5_bu_im{layer}").reshape(bsz, seq, S5_N)
            h_re, h_im = _s5_scan(bu_re, bu_im, pw_f, reverse=False, name=f"s5_scan_fwd{layer}")
            h_re2, h_im2 = h_re.reshape(t, S5_N), h_im.reshape(t, S5_N)
            ych = _mm(h_re2, wc_re, "nn", a2=h_im2, b2=wc_im, name=f"s5_out{layer}")
            w_glu = full["s5_w_glu"][j]
            y_b = _s5_post_fwd(ych, u, s5_d[j][None], w_glu, name=f"s5_post_fwd{layer}")
            rec.update(u=u, h_re=h_re, h_im=h_im, ych=ych, wb=(wb_re, wb_im), wc=(wc_re, wc_im), pw_b=pw_b,
                       y_a=y_a.reshape(t, HG_W), y_b=y_b)
            w_out = full["ev_w_out"][j]
            xs = _mm(rec["y_a"], w_out[:HG_W], "nn", a2=y_b, b2=w_out[HG_W:], res=xs, name=f"ev_out{layer}")
        else:
            w_in = full["od_w_in"][j]
            w_main = _by_head(w_in[:, :4 * GDN_W], 4, GDN_HEADS)
            w_bd = jnp.pad(w_in[:, 4 * GDN_W:], ((0, 0), (0, HEAD - 2 * GDN_HEADS)))
            proj = _mm(h, w_main, "nn", name=f"od_proj{layer}").reshape(bsz, seq, -1)
            bd = _mm(h, w_bd, "nn", name=f"od_proj_bd{layer}").reshape(bsz, seq, HEAD)
            conv_w = full["gdn_conv_w"][j].reshape(CONV_W, 3, GDN_HEADS, HEAD).transpose(0, 2, 1, 3)
            conv_w = jnp.pad(conv_w, ((0, SUBLANE - CONV_W), (0, 0), (0, 1), (0, 0))).reshape(SUBLANE, 4 * GDN_W)
            qkvg = _conv_fwd(proj, conv_w, name=f"gdn_conv_fwd{layer}")
            ab = jnp.pad(gdn_a_log[j][None], ((0, 0), (GDN_HEADS, HEAD - 2 * GDN_HEADS)))
            dtb = jnp.pad(gdn_dt_bias[j][None], ((0, 0), (GDN_HEADS, HEAD - 2 * GDN_HEADS)))
            y, states = _gdn_fwd(qkvg, bd, ab, dtb, gdn_norm_w[j][None], name=f"gdn_fwd{layer}")
            rec.update(proj=proj, bd=bd, conv_w=conv_w, qkvg=qkvg, ab=ab, dtb=dtb, states=states,
                       y=y.reshape(t, GDN_W), w_main=w_main, w_bd=w_bd)
            xs = _mm(rec["y"], full["od_w_out"][j], "nn", res=xs, name=f"od_out{layer}")
        xs, rec["ffn2"] = _ffn_fwd(xs, ffn2_norm[layer][None], full["ffn2_w_gate"][layer], full["ffn2_w_up"][layer],
                                   full["ffn2_w_down"][layer], f"b{layer}")
        saved.append(rec)

    loss_part, dx, d_final = _final_loss(xs, final_norm[None], loss_target.reshape(t, D_MODEL), name="loss_head")
    loss = lax.psum(loss_part[0, 0], ("x", "y", "c"))

    grads = {n: [None] * wts[n].shape[0] for n in WEIGHTS if n != "final_norm"}
    grads["final_norm"] = d_final[0]
    d_lbs = [None] * 2
    for layer in reversed(range(DEPTH)):
        j = layer // 2
        rec = saved[layer]
        dx, dn, dwg, dwu, dwd = _ffn_bwd(rec["ffn2"], ffn2_norm[layer][None], full["ffn2_w_gate"][layer],
                                         full["ffn2_w_up"][layer], full["ffn2_w_down"][layer], dx, f"b{layer}")
        grads["ffn2_norm"][layer], grads["ffn2_w_gate"][layer] = dn[0], dwg
        grads["ffn2_w_up"][layer], grads["ffn2_w_down"][layer] = dwu, dwd
        h = rec["h"]
        if layer % 2 == 0:
            w_out = full["ev_w_out"][j]
            dy_a = _mm(dx, w_out[:HG_W], "nt", name=f"ev_dya{layer}")
            dy_b = _mm(dx, w_out[HG_W:], "nt", name=f"ev_dyb{layer}")
            dw_out = jnp.concatenate([_mm(rec["y_a"], dx, "tn", name=f"ev_dwo_a{layer}"),
                                      _mm(rec["y_b"], dx, "tn", name=f"ev_dwo_b{layer}")], axis=0)
            grads["ev_w_out"][j] = dw_out
            wb_re, wb_im = rec["wb"]
            wc_re, wc_im = rec["wc"]
            dych, du, dd, dwglu = _s5_post_bwd(rec["ych"], rec["u"], s5_d[j][None], full["s5_w_glu"][j], dy_b,
                                               name=f"s5_post_bwd{layer}")
            grads["s5_d"][j], grads["s5_w_glu"][j] = dd[0], dwglu
            h_re2, h_im2 = rec["h_re"].reshape(t, S5_N), rec["h_im"].reshape(t, S5_N)
            dwc_re = _mm(h_re2, dych, "tn", name=f"s5_dwc_re{layer}")
            dwc_im = _mm(h_im2, dych, "tn", name=f"s5_dwc_im{layer}")
            dh_re = _mm(dych, wc_re, "nt", name=f"s5_dh_re{layer}").reshape(bsz, seq, S5_N)
            dh_im = _mm(dych, wc_im, "nt", name=f"s5_dh_im{layer}").reshape(bsz, seq, S5_N)
            g_re, g_im = _s5_scan(dh_re, dh_im, rec["pw_b"], reverse=True, name=f"s5_scan_bwd{layer}")
            da = _s5_dabar(g_re, g_im, rec["h_re"], rec["h_im"], name=f"s5_dabar{layer}").sum(axis=0)
            g_re2, g_im2 = g_re.reshape(t, S5_N), g_im.reshape(t, S5_N)
            dwb_re = _mm(rec["u"], g_re2, "tn", name=f"s5_dwb_re{layer}")
            dwb_im = _mm(rec["u"], g_im2, "tn", name=f"s5_dwb_im{layer}")
            du = _mm(g_re2, wb_re, "nt", a2=g_im2, b2=wb_im, res=du, name=f"s5_du{layer}")
            diag_in = lambda m: jnp.einsum("gpgn->gnp", m.reshape(S5_GROUPS, S5_GROUP, S5_GROUPS, S5_STATE))
            diag_out = lambda m: jnp.einsum("gngp->gpn", m.reshape(S5_GROUPS, S5_STATE, S5_GROUPS, S5_GROUP))
            d_disc = (da[0].reshape(S5_GROUPS, S5_STATE), da[1].reshape(S5_GROUPS, S5_STATE), diag_in(dwb_re),
                      diag_in(dwb_im))
            ga_re, ga_im, gb_re, gb_im, g_dt = rec["disc_vjp"](d_disc)
            grads["s5_a_re"][j], grads["s5_a_im"][j], grads["s5_b_re"][j] = ga_re, ga_im, gb_re
            grads["s5_b_im"][j], grads["s5_log_dt"][j] = gb_im, g_dt
            grads["s5_c_re"][j], grads["s5_c_im"][j] = diag_out(dwc_re), -diag_out(dwc_im)
            dproj_hg, dlb, dnw = _hg_bwd(rec["proj"], lbs[j][None], hg_norm_w[j][None], rec["hg_states"],
                                         dy_a.reshape(bsz, seq, HG_W), name=f"hgrn2_bwd{layer}")
            d_lbs[j] = dlb.sum(axis=0).reshape(HG_W)
            grads["hg_norm_w"][j] = dnw.sum(axis=(0, 1))
            dproj = jnp.concatenate([dproj_hg.reshape(t, 4 * HG_W), du], axis=1)
            w_in = ev_cols(full["ev_w_in"][j])
            dw_in = _mm(h, dproj, "tn", name=f"ev_dwin{layer}")
            grads["ev_w_in"][j] = ev_cols_back(dw_in)
            dh = _mm(dproj, w_in, "nt", name=f"ev_dh{layer}")
        else:
            dy = _mm(dx, full["od_w_out"][j], "nt", name=f"od_dy{layer}")
            grads["od_w_out"][j] = _mm(rec["y"], dx, "tn", name=f"od_dwo{layer}")
            dqkvg, dbd_h, dab, ddt, dnw = _gdn_bwd(rec["qkvg"], rec["bd"], rec["ab"], rec["dtb"],
                                                   gdn_norm_w[j][None], rec["states"],
                                                   dy.reshape(bsz, seq, GDN_W), name=f"gdn_bwd{layer}")
            grads["gdn_norm_w"][j] = dnw.sum(axis=(0, 1))
            grads["gdn_a_log"][j] = dab.sum(axis=(0, 1))[GDN_HEADS:2 * GDN_HEADS]
            grads["gdn_dt_bias"][j] = ddt.sum(axis=(0, 1))[GDN_HEADS:2 * GDN_HEADS]
            dbd = dbd_h.sum(axis=1).reshape(t, HEAD)
            dproj, dconv = _conv_bwd(rec["proj"], rec["conv_w"], dqkvg, name=f"gdn_conv_bwd{layer}")
            dconv = dconv.sum(axis=0)[:CONV_W].reshape(CONV_W, GDN_HEADS, 4, HEAD)[:, :, :3]
            grads["gdn_conv_w"][j] = dconv.transpose(0, 2, 1, 3).reshape(CONV_W, 3 * GDN_W)
            dproj = dproj.reshape(t, 4 * GDN_W)
            dw_main = _by_part(_mm(h, dproj, "tn", name=f"od_dwin{layer}"), 4, GDN_HEADS)
            dw_bd = _mm(h, dbd, "tn", name=f"od_dwbd{layer}")[:, :2 * GDN_HEADS]
            grads["od_w_in"][j] = jnp.concatenate([dw_main, dw_bd], axis=1)
            dh = _mm(dproj, rec["w_main"], "nt", name=f"od_dh{layer}")
            dh = _mm(dbd, rec["w_bd"], "nt", res=dh, name=f"od_dh_bd{layer}")
        dx, dn = _rms_bwd(rec["x_mix"], mix_norm[layer][None], dh, dx, name=f"rms_bwd_mix{layer}")
        grads["mix_norm"][layer] = dn[0]
        dx, dn, dwg, dwu, dwd = _ffn_bwd(rec["ffn1"], ffn1_norm[layer][None], full["ffn1_w_gate"][layer],
                                         full["ffn1_w_up"][layer], full["ffn1_w_down"][layer], dx, f"a{layer}")
        grads["ffn1_norm"][layer], grads["ffn1_w_gate"][layer] = dn[0], dwg
        grads["ffn1_w_up"][layer], grads["ffn1_w_down"][layer] = dwu, dwd
    grad_x = dx.reshape(x.shape)
    (grads["hg_lb_logits"],) = lbs_vjp(jnp.stack(d_lbs))
    wide_grads = {n: grads[n] for n in WIDE}
    for n in WEIGHTS:
        if isinstance(grads[n], list) and n not in WIDE:
            grads[n] = jnp.stack(grads[n])

    g_a = jnp.stack([wide_grads[n][i].reshape(D_MODEL, N_CHIPS, -1).transpose(1, 0, 2)
                     for n in WIDE for i in range(n_wide)], axis=1)
    g_a = g_a.reshape((N_CHIPS, 2, g_a.shape[1] // 2) + g_a.shape[2:])
    g_b = _rows_pack([_to_shards(grads[n], axis) for n, axis in ROWS], 1)
    sib_a, sib_b = _swap_halves([g_a, g_b], name="reduce_swap_halves")
    wire_a, own_a = _add_sibling(g_a, sib_a, name="reduce_add_sibling_wide")
    wire_b, own_b = _add_sibling(g_b, sib_b, name="reduce_add_sibling_rows")
    oth_a, oth_b = _exchange_chips([wire_a, wire_b], name="reduce_exchange_chips")
    half_a = _add_chips(own_a, oth_a, name="reduce_add_chips_wide")
    half_b = _add_chips(own_b, oth_b, name="reduce_add_chips_rows")
    peer_a, peer_b = _swap_reduced([half_a, half_b], name="reduce_swap_reduced")
    south = lax.axis_index("c") == 0
    both = lambda mine, peer: jnp.where(south, jnp.stack([mine, peer]), jnp.stack([peer, mine]))
    red_a = both(half_a, peer_a).reshape((len(WIDE) * n_wide,) + half_a.shape[1:])
    for k, n in enumerate(WIDE):
        grads[n] = red_a[k * n_wide:(k + 1) * n_wide]
    for (n, _), piece in zip(ROWS, _rows_unpack(both(half_b, peer_b), [wts[n].shape for n, _ in ROWS], 0)):
        grads[n] = piece

    small = jnp.concatenate([grads[n].reshape(-1) for n in SMALL])
    small = jnp.pad(small, (0, SMALL_ROWS * PACK_COLS - small.shape[0])).reshape(SMALL_ROWS, PACK_COLS)
    small = _sum_all(_gather_all(small, name="gather_small_grads"), name="sum_small_grads").reshape(-1)
    off = 0
    for n in SMALL:
        size = int(np.prod(wts[n].shape))
        grads[n] = small[off:off + size].reshape(wts[n].shape)
        off += size

    small_w = jnp.concatenate([wts[n].reshape(-1) for n in SMALL])
    small_m = jnp.concatenate([mom[n].reshape(-1) for n in SMALL])
    small_v = jnp.concatenate([var[n].reshape(-1) for n in SMALL])
    n_small = small_w.shape[0]
    padv = lambda a, fill: jnp.pad(a, (0, SMALL_ROWS * PACK_COLS - n_small), constant_values=fill).reshape(
        SMALL_ROWS, PACK_COLS)
    sd, sm, sv = _adamw(padv(small_w, 0.0), small.reshape(SMALL_ROWS, PACK_COLS), padv(small_m, 0.0),
                        padv(small_v, 1.0), name="adamw_small")
    delta, new_m, new_v = {}, {}, {}
    off = 0
    for n in SMALL:
        size = int(np.prod(wts[n].shape))
        delta[n] = sd.reshape(-1)[off:off + size].reshape(wts[n].shape)
        new_m[n] = sm.reshape(-1)[off:off + size].reshape(wts[n].shape)
        new_v[n] = sv.reshape(-1)[off:off + size].reshape(wts[n].shape)
        off += size
    for n, _ in SHARDED:
        delta[n], new_m[n], new_v[n] = _adamw(wts[n], grads[n], mom[n], var[n], name=f"adamw_{n}")

    return (loss, grad_x, *[grads[n] for n in WEIGHTS], *[delta[n] for n in WEIGHTS],
            *[new_m[n] for n in WEIGHTS], *[new_v[n] for n in WEIGHTS])
```

```python
import functools
import math

import jax
import jax.numpy as jnp
import numpy as np
from jax import lax
from jax.experimental import pallas as pl
from jax.experimental.pallas import tpu as pltpu

F32 = jnp.float32
BF16 = jnp.bfloat16
LHS_EXACT = 2
MESH_ID = pl.DeviceIdType.MESH

D_MODEL = 1024
D_FF = 2816
DEPTH = 4
NORM_EPS = 1e-6
F_MIN = 1e-6
CHUNK = 64
HEAD = 128
HG_HEADS = 4
HG_W = 512
S5_W = 512
S5_GROUP = 16
S5_GROUPS = 32
S5_STATE = 64
S5_N = S5_GROUPS * S5_STATE
GDN_HEADS = 8
GDN_W = 1024
CONV_W = 4
N_CHIPS = 4
N_DEV = 8
LANE = 128
SUBLANE = 8
VMEM_LIMIT = 56 * 1024 * 1024
PACK_COLS = 1024
ROW_BLOCK = 512
ROW_BLOCKS = 10
SMALL_ROWS = 288

ADAM_LR = 0.001
ADAM_B1 = 0.9
ADAM_B2 = 0.999
ADAM_EPS = 1e-08
ADAM_WD = 0.01
ADAM_STEP = 10

SHARDED = (("ffn1_w_gate", 2), ("ffn1_w_up", 2), ("ffn1_w_down", 1), ("ffn2_w_gate", 2), ("ffn2_w_up", 2),
           ("ffn2_w_down", 1), ("ev_w_in", 2), ("s5_w_glu", 1), ("ev_w_out", 1), ("od_w_in", 2),
           ("gdn_conv_w", 2), ("od_w_out", 1))
WIDE = ("ffn1_w_gate", "ffn1_w_up", "ffn2_w_gate", "ffn2_w_up")
ROWS = (("ffn1_w_down", 1), ("ffn2_w_down", 1), ("ev_w_out", 1), ("od_w_out", 1), ("s5_w_glu", 1), ("ev_w_in", 2),
        ("od_w_in", 2), ("gdn_conv_w", 2))
SMALL = ("ffn1_norm", "mix_norm", "ffn2_norm", "hg_lb_logits", "hg_norm_w", "s5_a_re", "s5_a_im", "s5_b_re",
         "s5_b_im", "s5_c_re", "s5_c_im", "s5_d", "s5_log_dt", "gdn_a_log", "gdn_dt_bias", "gdn_norm_w",
         "final_norm")
WEIGHTS = ("ffn1_norm", "ffn1_w_gate", "ffn1_w_up", "ffn1_w_down", "mix_norm", "ffn2_norm", "ffn2_w_gate",
           "ffn2_w_up", "ffn2_w_down", "ev_w_in", "hg_lb_logits", "hg_norm_w", "s5_a_re", "s5_a_im", "s5_b_re",
           "s5_b_im", "s5_c_re", "s5_c_im", "s5_d", "s5_log_dt", "s5_w_glu", "ev_w_out", "od_w_in", "gdn_conv_w",
           "gdn_a_log", "gdn_dt_bias", "gdn_norm_w", "od_w_out", "final_norm")


def _cparams(sem=None):
    return pltpu.CompilerParams(dimension_semantics=sem, vmem_limit_bytes=VMEM_LIMIT)


def _dg(a, b, ca, cb, hi):
    if a.ndim == 3 or b.ndim == 3:
        n = a.shape[0] if a.ndim == 3 else b.shape[0]
        a = a if a.ndim == 3 else jnp.broadcast_to(a, (n,) + a.shape)
        b = b if b.ndim == 3 else jnp.broadcast_to(b, (n,) + b.shape)
        dims = (((ca + 1,), (cb + 1,)), ((0,), (0,)))
    else:
        dims = (((ca,), (cb,)), ((), ()))
    dot = lambda p, q: lax.dot_general(p, q, dims, preferred_element_type=F32)
    if hi:
        a_hi, b_hi = a.astype(BF16), b.astype(BF16)
        b_lo = (b - b_hi.astype(F32)).astype(BF16)
        if hi == LHS_EXACT:
            b_lo2 = (b - b_hi.astype(F32) - b_lo.astype(F32)).astype(BF16)
            return dot(a_hi, b_hi) + (dot(a_hi, b_lo) + dot(a_hi, b_lo2))
        a_lo = (a - a_hi.astype(F32)).astype(BF16)
        return dot(a_hi, b_hi) + (dot(a_hi, b_lo) + dot(a_lo, b_hi))
    return dot(a.astype(BF16), b.astype(BF16))


@functools.partial(jax.custom_vjp, nondiff_argnums=(2,))
def mm_nn(a, b, hi=False):
    return _dg(a, b, 1, 0, hi)


@functools.partial(jax.custom_vjp, nondiff_argnums=(2,))
def mm_nt(a, b, hi=False):
    return _dg(a, b, 1, 1, hi)


@functools.partial(jax.custom_vjp, nondiff_argnums=(2,))
def mm_tn(a, b, hi=False):
    return _dg(a, b, 0, 0, hi)


mm_nn.defvjp(lambda a, b, hi: (_dg(a, b, 1, 0, hi), (a, b)),
             lambda hi, r, g: (mm_nt(g, r[1], hi), mm_tn(r[0], g, hi)))
mm_nt.defvjp(lambda a, b, hi: (_dg(a, b, 1, 1, hi), (a, b)),
             lambda hi, r, g: (mm_nn(g, r[1], hi), mm_tn(g, r[0], hi)))
mm_tn.defvjp(lambda a, b, hi: (_dg(a, b, 0, 0, hi), (a, b)),
             lambda hi, r, g: (mm_nt(r[1], g, hi), mm_nn(r[0], g, hi)))


def _sigmoid(x):
    return 1.0 / (1.0 + jnp.exp(-x))


def _silu(x):
    return x * _sigmoid(x)


def _softplus(x):
    return jnp.maximum(x, 0.0) + jnp.log(1.0 + jnp.exp(-jnp.abs(x)))


def _gelu(x):
    return 0.5 * x * (1.0 + jnp.tanh(math.sqrt(2.0 / math.pi) * (x + 0.044715 * (x * x * x))))


def _iota(shape, dim):
    return lax.broadcasted_iota(jnp.int32, shape, dim)


def _l2norm(t):
    return t * lax.rsqrt(jnp.sum(t * t, axis=-1, keepdims=True) + NORM_EPS)


def gated_head_norm(o, gate, nw):
    o = o * lax.rsqrt(jnp.mean(o * o, axis=-1, keepdims=True) + NORM_EPS) * nw
    return o * _silu(gate)


HG_SUB = 8
HG_MID = 3


def hg_chunk(st, ql, fl, v, gl, lb, nw):
    c = CHUNK
    n = st.shape[0]
    q = _silu(ql)
    f = lb + (1.0 - lb) * _sigmoid(fl)
    lf = jnp.log(jnp.maximum(f, F_MIN))
    k = 1.0 - f
    t_i = _iota((2 * c, c), 0)
    s_i = _iota((2 * c, c), 1)
    lim = jnp.where(t_i < c, t_i, ((t_i - c) // HG_SUB) * HG_SUB + HG_MID)
    sums = jnp.broadcast_to(jnp.where(s_i <= lim, 1.0, 0.0).astype(F32), (n, 2 * c, c))
    cum = mm_nn(sums, lf, LHS_EXACT)
    b, rb = cum[:, :c], cum[:, c:]
    qp = q * jnp.exp(b - rb)
    row = _iota((1, c, HEAD), 1)
    blocks = []
    for i in range(c // HG_SUB):
        r_i = jnp.tile(rb[:, i * HG_SUB:(i + 1) * HG_SUB], (1, c // HG_SUB, 1))
        seen = row < (i + 1) * HG_SUB
        kp = k * jnp.exp(jnp.where(seen, r_i - b, 0.0))
        blocks.append(mm_nt(qp[:, i * HG_SUB:(i + 1) * HG_SUB], kp))
    a = jnp.concatenate(blocks, axis=1)
    a = jnp.where(_iota((1, c, c), 2) <= _iota((1, c, c), 1), a, 0.0)
    o = mm_nn(a, v) + mm_nt(q * jnp.exp(b), st)
    bl = jnp.sum(lf, axis=1, keepdims=True)
    st_new = st * jnp.exp(bl) + mm_tn(v, k * jnp.exp(bl - b))
    return st_new, gated_head_norm(o, gl, nw)


def gdn_chunk(st, qc, kc, vc, gate, bd, ab, dtb, nw, mb, ma):
    c = CHUNK
    n = st.shape[0]
    beta = _sigmoid(jnp.sum(bd * mb, axis=-1, keepdims=True))
    a_lin = jnp.sum(bd * ma, axis=-1, keepdims=True)
    a_log = jnp.sum(ab * ma, axis=-1, keepdims=True)
    dt_b = jnp.sum(dtb * ma, axis=-1, keepdims=True)
    la = -jnp.exp(a_log) * _softplus(a_lin + dt_b)
    la_b = jnp.broadcast_to(la, (n, c, HEAD))
    t_i = _iota((1, c, c), 1)
    s_i = _iota((1, c, c), 2)
    lower = s_i <= t_i
    strict = s_i < t_i
    tri = jnp.broadcast_to(jnp.where(lower, 1.0, 0.0).astype(F32), (n, c, c))
    g = mm_nn(tri, la_b, LHS_EXACT)
    d = mm_nn(tri, jnp.where(strict, la_b[:, :, :c], 0.0), LHS_EXACT)
    lm = jnp.where(lower, jnp.exp(jnp.where(lower, d, 0.0)), 0.0)
    q = _l2norm(qc) * (HEAD ** -0.5)
    k = _l2norm(kc)
    kb = k * beta
    vb = vc * beta
    m = jnp.where(strict, mm_nt(kb, k) * lm, 0.0)
    eye = jnp.where(s_i == t_i, 1.0, 0.0).astype(F32)
    t_inv = eye - m
    p = m
    for _ in range(int(math.log2(c)) - 1):
        p = mm_nn(p, p, True)
        t_inv = t_inv + mm_nn(t_inv, p, True)
    u = mm_nn(t_inv, vb)
    w = mm_nn(t_inv, kb * jnp.exp(g))
    attn = mm_nt(q, k) * lm
    v_new = u - mm_nt(w, st)
    o = mm_nt(q * jnp.exp(g), st) + mm_nn(attn, v_new)
    g_last = jnp.sum(la_b, axis=1, keepdims=True)
    st_new = st * jnp.exp(g_last) + mm_tn(v_new, k * jnp.exp(g_last - g))
    return st_new, gated_head_norm(o, gate, nw)


def s5_post(ych, u, d, wglu):
    y = _gelu(ych + d * u)
    return y * _sigmoid(mm_nn(y, wglu))


def _pick(n, cands):
    for c in cands:
        if n % c == 0:
            return c
    return n


def _mm(a, b, mode, *, name, out_dtype=F32, res=None, scale=1.0, tm=None, tn=None, tk=None, a2=None, b2=None):
    if mode == "nn":
        (m, k), (k2, n) = a.shape, b.shape
    elif mode == "nt":
        (m, k), (n, k2) = a.shape, b.shape
    else:
        (k, m), (k2, n) = a.shape, b.shape
    assert k == k2, (a.shape, b.shape, mode)
    tm = tm or _pick(m, (512, 1408, 256, 128) if mode == "tn" else (512, 704, 256, 128))
    tn = tn or _pick(n, (1024, 1408, 512, 256, 128))
    tk = tk or _pick(k, (1024, 1408, 512, 256, 128))
    nk = k // tk
    grid = (n // tn, m // tm, nk)
    if mode == "tn":
        a_spec = pl.BlockSpec((tk, tm), lambda j, i, kk: (kk, i))
    else:
        a_spec = pl.BlockSpec((tm, tk), lambda j, i, kk: (i, kk))
    if mode == "nt":
        b_spec = pl.BlockSpec((tn, tk), lambda j, i, kk: (j, kk))
    else:
        b_spec = pl.BlockSpec((tk, tn), lambda j, i, kk: (kk, j))
    o_spec = pl.BlockSpec((tm, tn), lambda j, i, kk: (i, j))
    ca = 0 if mode == "tn" else 1
    cb = 1 if mode == "nt" else 0
    has_res = res is not None
    pairs = 1 if a2 is None else 2
    assert pairs == 1 or (a2.shape == a.shape and b2.shape == b.shape)

    def body(*refs):
        ab_refs, refs = refs[:2 * pairs], refs[2 * pairs:]
        if has_res:
            r_ref, o_ref, acc = refs
        else:
            o_ref, acc = refs
        kk = pl.program_id(2)

        @pl.when(kk == 0)
        def _():
            acc[...] = jnp.zeros_like(acc)

        part = _dg(ab_refs[0][...], ab_refs[1][...], ca, cb, False)
        if pairs == 2:
            part = part + _dg(ab_refs[2][...], ab_refs[3][...], ca, cb, False)
        acc[...] += part

        @pl.when(kk == nk - 1)
        def _():
            r = acc[...] * scale if scale != 1.0 else acc[...]
            if has_res:
                r = r_ref[...] + r
            o_ref[...] = r.astype(out_dtype)

    in_specs = [a_spec, b_spec] * pairs + ([o_spec] if has_res else [])
    args = (a, b) + ((a2, b2) if pairs == 2 else ()) + ((res,) if has_res else ())
    return pl.pallas_call(
        body, out_shape=jax.ShapeDtypeStruct((m, n), out_dtype), grid=grid, in_specs=in_specs, out_specs=o_spec,
        scratch_shapes=[pltpu.VMEM((tm, tn), F32)], name=name,
        compiler_params=_cparams(("parallel", "parallel", "arbitrary")))(*args)


def _rms_fwd(x, w, *, name):
    t, d = x.shape
    tm = _pick(t, (512, 256, 128))

    def body(x_ref, w_ref, o_ref):
        xv = x_ref[...]
        r = lax.rsqrt(jnp.mean(xv * xv, axis=-1, keepdims=True) + NORM_EPS)
        o_ref[...] = (xv * r * w_ref[...]).astype(BF16)

    return pl.pallas_call(
        body, out_shape=jax.ShapeDtypeStruct((t, d), BF16), grid=(t // tm,),
        in_specs=[pl.BlockSpec((tm, d), lambda i: (i, 0)), pl.BlockSpec((1, d), lambda i: (0, 0))],
        out_specs=pl.BlockSpec((tm, d), lambda i: (i, 0)), name=name, compiler_params=_cparams(("parallel",)))(x, w)


def _rms_bwd(x, w, dh, dres, *, name):
    t, d = x.shape
    tm = _pick(t, (512, 256, 128))

    def body(x_ref, w_ref, dh_ref, dr_ref, dx_ref, dw_ref):
        xv = x_ref[...]
        r = lax.rsqrt(jnp.mean(xv * xv, axis=-1, keepdims=True) + NORM_EPS)
        xh = xv * r
        dhv = dh_ref[...]
        dxh = dhv * w_ref[...]
        dx_ref[...] = dr_ref[...] + r * (dxh - xh * jnp.mean(dxh * xh, axis=-1, keepdims=True))

        @pl.when(pl.program_id(0) == 0)
        def _():
            dw_ref[...] = jnp.zeros_like(dw_ref)

        dw_ref[...] += jnp.sum(dhv * xh, axis=0, keepdims=True)

    row = pl.BlockSpec((tm, d), lambda i: (i, 0))
    vec = pl.BlockSpec((1, d), lambda i: (0, 0))
    return pl.pallas_call(
        body, out_shape=[jax.ShapeDtypeStruct((t, d), F32), jax.ShapeDtypeStruct((1, d), F32)], grid=(t // tm,),
        in_specs=[row, vec, row, row], out_specs=[row, vec], name=name,
        compiler_params=_cparams(("arbitrary",)))(x, w, dh, dres)


def _final_loss(x, w, target, *, name):
    t, d = x.shape
    tm = _pick(t, (512, 256, 128))

    def body(x_ref, w_ref, t_ref, l_ref, dx_ref, dw_ref):
        xv = x_ref[...]
        wv = w_ref[...]
        r = lax.rsqrt(jnp.mean(xv * xv, axis=-1, keepdims=True) + NORM_EPS)
        xh = xv * r
        err = xh * wv - t_ref[...]
        dy = err * (1.0 / d)
        dxh = dy * wv
        dx_ref[...] = r * (dxh - xh * jnp.mean(dxh * xh, axis=-1, keepdims=True))

        @pl.when(pl.program_id(0) == 0)
        def _():
            dw_ref[...] = jnp.zeros_like(dw_ref)
            l_ref[...] = jnp.zeros_like(l_ref)

        dw_ref[...] += jnp.sum(dy * xh, axis=0, keepdims=True)
        l_ref[...] += 0.5 * jnp.sum(jnp.mean(err * err, axis=-1, keepdims=True), axis=0, keepdims=True)

    row = pl.BlockSpec((tm, d), lambda i: (i, 0))
    vec = pl.BlockSpec((1, d), lambda i: (0, 0))
    return pl.pallas_call(
        body, out_shape=[jax.ShapeDtypeStruct((SUBLANE, LANE), F32), jax.ShapeDtypeStruct((t, d), F32),
                         jax.ShapeDtypeStruct((1, d), F32)], grid=(t // tm,),
        in_specs=[row, vec, row], out_specs=[pl.BlockSpec((SUBLANE, LANE), lambda i: (0, 0)), row, vec], name=name,
        compiler_params=_cparams(("arbitrary",)))(x, w, target)


def _ffn_up(h, wg, wu, *, name):
    t, d = h.shape
    f = wg.shape[1]
    tm = _pick(t, (512, 256, 128))
    tn = _pick(f, (1408, 512, 256, 128))

    def body(h_ref, wg_ref, wu_ref, g_ref, u_ref, a_ref):
        hv = h_ref[...]
        g = _dg(hv, wg_ref[...], 1, 0, False)
        u = _dg(hv, wu_ref[...], 1, 0, False)
        g_ref[...] = g
        u_ref[...] = u
        a_ref[...] = (_silu(g) * u).astype(BF16)

    hs = pl.BlockSpec((tm, d), lambda j, i: (i, 0))
    ws = pl.BlockSpec((d, tn), lambda j, i: (0, j))
    os_ = pl.BlockSpec((tm, tn), lambda j, i: (i, j))
    return pl.pallas_call(
        body, out_shape=[jax.ShapeDtypeStruct((t, f), F32), jax.ShapeDtypeStruct((t, f), F32),
                         jax.ShapeDtypeStruct((t, f), BF16)], grid=(f // tn, t // tm),
        in_specs=[hs, ws, ws], out_specs=[os_, os_, os_], name=name,
        compiler_params=_cparams(("parallel", "parallel")))(h, wg, wu)


def _ffn_dact(dxo, wd, g, u, *, name):
    t, d = dxo.shape
    f = wd.shape[0]
    tm = _pick(t, (512, 256, 128))
    tn = _pick(f, (1408, 512, 256, 128))

    def body(dx_ref, wd_ref, g_ref, u_ref, dg_ref, du_ref):
        da = 0.5 * _dg(dx_ref[...], wd_ref[...], 1, 1, False)
        gv = g_ref[...]
        sg = _sigmoid(gv)
        dg_ref[...] = (da * u_ref[...] * (sg * (1.0 + gv * (1.0 - sg)))).astype(BF16)
        du_ref[...] = (da * (gv * sg)).astype(BF16)

    xs = pl.BlockSpec((tm, d), lambda j, i: (i, 0))
    ws = pl.BlockSpec((tn, d), lambda j, i: (j, 0))
    os_ = pl.BlockSpec((tm, tn), lambda j, i: (i, j))
    return pl.pallas_call(
        body, out_shape=[jax.ShapeDtypeStruct((t, f), BF16), jax.ShapeDtypeStruct((t, f), BF16)],
        grid=(f // tn, t // tm), in_specs=[xs, ws, os_, os_], out_specs=[os_, os_], name=name,
        compiler_params=_cparams(("parallel", "parallel")))(dxo, wd, g, u)


def _ffn_fwd(x, nw, wg, wu, wd, tag):
    h = _rms_fwd(x, nw, name=f"rms_fwd_{tag}")
    g, u, a = _ffn_up(h, wg, wu, name=f"ffn_up_{tag}")
    y = _mm(a, wd, "nn", res=x, scale=0.5, name=f"ffn_down_{tag}")
    return y, (x, h, g, u, a)


def _ffn_bwd(saved, nw, wg, wu, wd, dxo, tag):
    x, h, g, u, a = saved
    dg, du = _ffn_dact(dxo, wd, g, u, name=f"ffn_dact_{tag}")
    dwd = _mm(a, dxo, "tn", scale=0.5, name=f"ffn_dwd_{tag}")
    dwg = _mm(h, dg, "tn", name=f"ffn_dwg_{tag}")
    dwu = _mm(h, du, "tn", name=f"ffn_dwu_{tag}")
    dh = _mm(dg, wg, "nt", a2=du, b2=wu, name=f"ffn_dh_{tag}")
    dx, dnw = _rms_bwd(x, nw, dh, dxo, name=f"rms_bwd_{tag}")
    return dx, dnw, dwg, dwu, dwd


MIX_HP = 4


def _pairs(ref, bsz, part):
    return jnp.stack([ref[b, :, 512 * hh + HEAD * part:512 * hh + HEAD * (part + 1)]
                      for b in range(bsz) for hh in range(MIX_HP)])


def _hg_fwd(proj, lbs, nw, *, name):
    bsz, l, _ = proj.shape
    nc = l // CHUNK
    groups = HG_HEADS // MIX_HP

    def body(p_ref, lb_ref, nw_ref, y_ref, s_ref, st):
        @pl.when(pl.program_id(1) == 0)
        def _():
            st[...] = jnp.zeros_like(st)

        prev = st[...]
        lb = jnp.stack([lb_ref[:, HEAD * hh:HEAD * (hh + 1)] for _ in range(bsz) for hh in range(MIX_HP)])
        st_new, y = hg_chunk(prev, *[_pairs(p_ref, bsz, part) for part in range(4)], lb, nw_ref[...])
        s_ref[...] = prev.reshape(bsz, MIX_HP, HEAD, HEAD)
        st[...] = st_new
        for b in range(bsz):
            for hh in range(MIX_HP):
                y_ref[b, :, HEAD * hh:HEAD * (hh + 1)] = y[b * MIX_HP + hh]

    return pl.pallas_call(
        body,
        out_shape=[jax.ShapeDtypeStruct((bsz, l, HG_W), F32),
                   jax.ShapeDtypeStruct((bsz, HG_HEADS, nc, HEAD, HEAD), F32)],
        grid=(groups, nc),
        in_specs=[pl.BlockSpec((bsz, CHUNK, MIX_HP * 512), lambda g, c: (0, c, g)),
                  pl.BlockSpec((1, MIX_HP * HEAD), lambda g, c: (0, g)),
                  pl.BlockSpec((1, HEAD), lambda g, c: (0, 0))],
        out_specs=[pl.BlockSpec((bsz, CHUNK, MIX_HP * HEAD), lambda g, c: (0, c, g)),
                   pl.BlockSpec((bsz, MIX_HP, None, HEAD, HEAD), lambda g, c: (0, g, c, 0, 0))],
        scratch_shapes=[pltpu.VMEM((bsz * MIX_HP, HEAD, HEAD), F32)], name=name,
        compiler_params=_cparams(("parallel", "arbitrary")))(proj, lbs, nw)


def _hg_bwd(proj, lbs, nw, states, dy, *, name):
    bsz, l, _ = proj.shape
    nc = l // CHUNK
    groups = HG_HEADS // MIX_HP
    n = bsz * MIX_HP

    def body(p_ref, lb_ref, nw_ref, s_ref, dy_ref, dp_ref, dlb_ref, dnw_ref, dst):
        @pl.when(pl.program_id(1) == 0)
        def _():
            dst[...] = jnp.zeros_like(dst)
            dlb_ref[...] = jnp.zeros_like(dlb_ref)
            dnw_ref[...] = jnp.zeros_like(dnw_ref)

        lb = jnp.stack([lb_ref[:, HEAD * hh:HEAD * (hh + 1)] for _ in range(bsz) for hh in range(MIX_HP)])
        dy3 = jnp.stack([dy_ref[b, :, HEAD * hh:HEAD * (hh + 1)] for b in range(bsz) for hh in range(MIX_HP)])
        _, vjp = jax.vjp(hg_chunk, s_ref[...].reshape(n, HEAD, HEAD), *[_pairs(p_ref, bsz, part) for part in range(4)],
                         lb, nw_ref[...])
        d_st, dq, df, dv, dgl, dlb, dnw = vjp((dst[...], dy3))
        dst[...] = d_st
        for b in range(bsz):
            for hh in range(MIX_HP):
                for part, d in enumerate((dq, df, dv, dgl)):
                    dp_ref[b, :, 512 * hh + HEAD * part:512 * hh + HEAD * (part + 1)] = d[b * MIX_HP + hh]
        dlb_ref[...] += dlb.reshape(bsz, MIX_HP, 1, HEAD)
        dnw_ref[...] += dnw

    rev = lambda c: nc - 1 - c
    return pl.pallas_call(
        body,
        out_shape=[jax.ShapeDtypeStruct((bsz, l, 4 * HG_W), F32),
                   jax.ShapeDtypeStruct((bsz, HG_HEADS, 1, HEAD), F32),
                   jax.ShapeDtypeStruct((groups, 1, HEAD), F32)],
        grid=(groups, nc),
        in_specs=[pl.BlockSpec((bsz, CHUNK, MIX_HP * 512), lambda g, c: (0, rev(c), g)),
                  pl.BlockSpec((1, MIX_HP * HEAD), lambda g, c: (0, g)),
                  pl.BlockSpec((1, HEAD), lambda g, c: (0, 0)),
                  pl.BlockSpec((bsz, MIX_HP, None, HEAD, HEAD), lambda g, c: (0, g, rev(c), 0, 0)),
                  pl.BlockSpec((bsz, CHUNK, MIX_HP * HEAD), lambda g, c: (0, rev(c), g))],
        out_specs=[pl.BlockSpec((bsz, CHUNK, MIX_HP * 512), lambda g, c: (0, rev(c), g)),
                   pl.BlockSpec((bsz, MIX_HP, 1, HEAD), lambda g, c: (0, g, 0, 0)),
                   pl.BlockSpec((None, 1, HEAD), lambda g, c: (g, 0, 0))],
        scratch_shapes=[pltpu.VMEM((n, HEAD, HEAD), F32)], name=name,
        compiler_params=_cparams(("parallel", "arbitrary")))(proj, lbs, nw, states, dy)


def _head_masks(group, bsz):
    n = bsz * MIX_HP
    head = group * MIX_HP + _iota((n, 1, HEAD), 0) % MIX_HP
    lane = _iota((n, 1, HEAD), 2)
    return jnp.where(lane == head, 1.0, 0.0).astype(F32), jnp.where(lane == GDN_HEADS + head, 1.0, 0.0).astype(F32)


def _gdn_fwd(qkvg, bd, ab, dtb, nw, *, name):
    bsz, l, _ = qkvg.shape
    nc = l // CHUNK
    groups = GDN_HEADS // MIX_HP

    def body(p_ref, bd_ref, ab_ref, dtb_ref, nw_ref, y_ref, s_ref, st):
        @pl.when(pl.program_id(1) == 0)
        def _():
            st[...] = jnp.zeros_like(st)

        mb, ma = _head_masks(pl.program_id(0), bsz)
        prev = st[...]
        bd3 = jnp.stack([bd_ref[b] for b in range(bsz) for _ in range(MIX_HP)])
        st_new, y = gdn_chunk(prev, *[_pairs(p_ref, bsz, part) for part in range(4)], bd3, ab_ref[...],
                              dtb_ref[...], nw_ref[...], mb, ma)
        s_ref[...] = prev.reshape(bsz, MIX_HP, HEAD, HEAD)
        st[...] = st_new
        for b in range(bsz):
            for hh in range(MIX_HP):
                y_ref[b, :, HEAD * hh:HEAD * (hh + 1)] = y[b * MIX_HP + hh]

    vec = pl.BlockSpec((1, HEAD), lambda g, c: (0, 0))
    return pl.pallas_call(
        body,
        out_shape=[jax.ShapeDtypeStruct((bsz, l, GDN_W), F32),
                   jax.ShapeDtypeStruct((bsz, GDN_HEADS, nc, HEAD, HEAD), F32)],
        grid=(groups, nc),
        in_specs=[pl.BlockSpec((bsz, CHUNK, MIX_HP * 512), lambda g, c: (0, c, g)),
                  pl.BlockSpec((bsz, CHUNK, HEAD), lambda g, c: (0, c, 0)), vec, vec, vec],
        out_specs=[pl.BlockSpec((bsz, CHUNK, MIX_HP * HEAD), lambda g, c: (0, c, g)),
                   pl.BlockSpec((bsz, MIX_HP, None, HEAD, HEAD), lambda g, c: (0, g, c, 0, 0))],
        scratch_shapes=[pltpu.VMEM((bsz * MIX_HP, HEAD, HEAD), F32)], name=name,
        compiler_params=_cparams(("parallel", "arbitrary")))(qkvg, bd, ab, dtb, nw)


def _gdn_bwd(qkvg, bd, ab, dtb, nw, states, dy, *, name):
    bsz, l, _ = qkvg.shape
    nc = l // CHUNK
    groups = GDN_HEADS // MIX_HP
    n = bsz * MIX_HP

    def body(p_ref, bd_ref, ab_ref, dtb_ref, nw_ref, s_ref, dy_ref, dp_ref, dbd_ref, dab_ref, ddt_ref, dnw_ref,
             dst):
        @pl.when(pl.program_id(1) == 0)
        def _():
            dst[...] = jnp.zeros_like(dst)
            dab_ref[...] = jnp.zeros_like(dab_ref)
            ddt_ref[...] = jnp.zeros_like(ddt_ref)
            dnw_ref[...] = jnp.zeros_like(dnw_ref)

        mb, ma = _head_masks(pl.program_id(0), bsz)
        fn = lambda st, q, k, v, gt, bdv, abv, dtv, nwv: gdn_chunk(st, q, k, v, gt, bdv, abv, dtv, nwv, mb, ma)
        bd3 = jnp.stack([bd_ref[b] for b in range(bsz) for _ in range(MIX_HP)])
        dy3 = jnp.stack([dy_ref[b, :, HEAD * hh:HEAD * (hh + 1)] for b in range(bsz) for hh in range(MIX_HP)])
        _, vjp = jax.vjp(fn, s_ref[...].reshape(n, HEAD, HEAD), *[_pairs(p_ref, bsz, part) for part in range(4)], bd3,
                         ab_ref[...], dtb_ref[...], nw_ref[...])
        d_st, dq, dk, dv, dgt, dbd, dab, ddt, dnw = vjp((dst[...], dy3))
        dst[...] = d_st
        for b in range(bsz):
            for hh in range(MIX_HP):
                for part, d in enumerate((dq, dk, dv, dgt)):
                    dp_ref[b, :, 512 * hh + HEAD * part:512 * hh + HEAD * (part + 1)] = d[b * MIX_HP + hh]
        dbd_ref[...] = jnp.sum(dbd.reshape(bsz, MIX_HP, CHUNK, HEAD), axis=1)
        dab_ref[...] += dab
        ddt_ref[...] += ddt
        dnw_ref[...] += dnw

    rev = lambda c: nc - 1 - c
    vec = pl.BlockSpec((1, HEAD), lambda g, c: (0, 0))
    acc = pl.BlockSpec((None, 1, HEAD), lambda g, c: (g, 0, 0))
    acc_shape = jax.ShapeDtypeStruct((groups, 1, HEAD), F32)
    return pl.pallas_call(
        body,
        out_shape=[jax.ShapeDtypeStruct((bsz, l, 4 * GDN_W), F32),
                   jax.ShapeDtypeStruct((bsz, groups, l, HEAD), F32), acc_shape, acc_shape, acc_shape],
        grid=(groups, nc),
        in_specs=[pl.BlockSpec((bsz, CHUNK, MIX_HP * 512), lambda g, c: (0, rev(c), g)),
                  pl.BlockSpec((bsz, CHUNK, HEAD), lambda g, c: (0, rev(c), 0)), vec, vec, vec,
                  pl.BlockSpec((bsz, MIX_HP, None, HEAD, HEAD), lambda g, c: (0, g, rev(c), 0, 0)),
                  pl.BlockSpec((bsz, CHUNK, MIX_HP * HEAD), lambda g, c: (0, rev(c), g))],
        out_specs=[pl.BlockSpec((bsz, CHUNK, MIX_HP * 512), lambda g, c: (0, rev(c), g)),
                   pl.BlockSpec((bsz, None, CHUNK, HEAD), lambda g, c: (0, g, rev(c), 0)), acc, acc, acc],
        scratch_shapes=[pltpu.VMEM((n, HEAD, HEAD), F32)], name=name,
        compiler_params=_cparams(("parallel", "arbitrary")))(qkvg, bd, ab, dtb, nw, states, dy)


CONV_TL = 256
QKV_LANES = 3 * HEAD


def _conv_fwd(x, w, *, name):
    bsz, l, wd = x.shape
    tl = min(CONV_TL, l)
    nr = l // tl
    hb = tl // SUBLANE

    def body(x_ref, xp_ref, w_ref, o_ref):
        xv = x_ref[...]
        halo = jnp.where(pl.program_id(2) > 0, xp_ref[...], 0.0)
        xc = jnp.concatenate([halo, xv], axis=0)
        wv = w_ref[...]
        z = wv[CONV_W - 1:CONV_W] * xv
        for j in range(CONV_W - 1):
            z = z + wv[j:j + 1] * pltpu.roll(xc, CONV_W - 1 - j, 0)[SUBLANE:]
        o_ref[...] = jnp.where(_iota((tl, 512), 1) < QKV_LANES, _silu(z), xv)

    return pl.pallas_call(
        body, out_shape=jax.ShapeDtypeStruct(x.shape, F32), grid=(bsz, wd // 512, nr),
        in_specs=[pl.BlockSpec((None, tl, 512), lambda b, h, r: (b, r, h)),
                  pl.BlockSpec((None, SUBLANE, 512), lambda b, h, r: (b, jnp.maximum(r * hb - 1, 0), h)),
                  pl.BlockSpec((SUBLANE, 512), lambda b, h, r: (0, h))],
        out_specs=pl.BlockSpec((None, tl, 512), lambda b, h, r: (b, r, h)), name=name,
        compiler_params=_cparams(("parallel", "parallel", "arbitrary")))(x, x, w)


def _conv_bwd(x, w, dy, *, name):
    bsz, l, wd = x.shape
    tl = min(CONV_TL, l)
    nr = l // tl
    hb = tl // SUBLANE

    def body(x_ref, xp_ref, xn_ref, w_ref, dy_ref, dyn_ref, dx_ref, dw_ref):
        r = pl.program_id(2)
        xv = x_ref[...]
        prev = jnp.where(r > 0, xp_ref[...], 0.0)
        last = r == nr - 1
        nxt = jnp.where(last, 0.0, xn_ref[...])
        dyn = jnp.where(last, 0.0, dyn_ref[...])
        xc = jnp.concatenate([prev, xv, nxt], axis=0)
        wv = w_ref[...]
        shifted = [pltpu.roll(xc, CONV_W - 1 - j, 0) for j in range(CONV_W - 1)] + [xc]
        z = wv[0:1] * shifted[0]
        for j in range(1, CONV_W):
            z = z + wv[j:j + 1] * shifted[j]
        z = z[SUBLANE:]
        sg = _sigmoid(z)
        dyc = jnp.concatenate([dy_ref[...], dyn], axis=0)
        dz = dyc * (sg * (1.0 + z * (1.0 - sg)))
        dx = wv[CONV_W - 1:CONV_W] * dz[:tl]
        for j in range(CONV_W - 1):
            s = CONV_W - 1 - j
            dx = dx + wv[j:j + 1] * pltpu.roll(dz, tl + SUBLANE - s, 0)[:tl]
        conv_lane = _iota((tl, 512), 1) < QKV_LANES
        dx_ref[...] = jnp.where(conv_lane, dx, dy_ref[...])

        @pl.when(r == 0)
        def _():
            dw_ref[...] = jnp.zeros_like(dw_ref)

        dzt = dz[:tl]
        rows = [jnp.sum(dzt * shifted[j][SUBLANE:SUBLANE + tl], axis=0, keepdims=True) for j in range(CONV_W)]
        rows.append(jnp.zeros((SUBLANE - CONV_W, 512), F32))
        dw_ref[...] += jnp.concatenate(rows, axis=0)

    blk = pl.BlockSpec((None, tl, 512), lambda b, h, r: (b, r, h))
    prev = pl.BlockSpec((None, SUBLANE, 512), lambda b, h, r: (b, jnp.maximum(r * hb - 1, 0), h))
    nxt = pl.BlockSpec((None, SUBLANE, 512), lambda b, h, r: (b, jnp.minimum((r + 1) * hb, l // SUBLANE - 1), h))
    return pl.pallas_call(
        body, out_shape=[jax.ShapeDtypeStruct(x.shape, F32), jax.ShapeDtypeStruct((bsz, SUBLANE, wd), F32)],
        grid=(bsz, wd // 512, nr),
        in_specs=[blk, prev, nxt, pl.BlockSpec((SUBLANE, 512), lambda b, h, r: (0, h)), blk, nxt],
        out_specs=[blk, pl.BlockSpec((None, SUBLANE, 512), lambda b, h, r: (b, 0, h))], name=name,
        compiler_params=_cparams(("parallel", "parallel", "arbitrary")))(x, x, x, w, dy, dy)


S5_LC = 256
S5_TN = 1024


def _s5_scan(xr, xi, pw, *, reverse, name):
    bsz, l, n = xr.shape
    lc = min(S5_LC, l)
    nc = l // lc
    steps = int(math.log2(lc))

    def shift(v, s):
        if reverse:
            if s >= SUBLANE:
                return jnp.concatenate([v[s:], jnp.zeros((s, v.shape[1]), F32)], axis=0)
            return jnp.where(_iota(v.shape, 0) < lc - s, pltpu.roll(v, lc - s, 0), 0.0)
        if s >= SUBLANE:
            return jnp.concatenate([jnp.zeros((s, v.shape[1]), F32), v[:lc - s]], axis=0)
        return jnp.where(_iota(v.shape, 0) >= s, pltpu.roll(v, s, 0), 0.0)

    def body(xr_ref, xi_ref, pw_ref, hr_ref, hi_ref, cr, ci):
        @pl.when(pl.program_id(2) == 0)
        def _():
            cr[...] = jnp.zeros_like(cr)
            ci[...] = jnp.zeros_like(ci)

        vr, vi = xr_ref[...], xi_ref[...]
        for k in range(steps):
            s = 1 << k
            ar, ai = pw_ref[0, s - 1:s, :], pw_ref[1, s - 1:s, :]
            sr, si = shift(vr, s), shift(vi, s)
            vr, vi = vr + ar * sr - ai * si, vi + ar * si + ai * sr
        pr, pi = pw_ref[2], pw_ref[3]
        c_r, c_i = cr[...], ci[...]
        vr, vi = vr + pr * c_r - pi * c_i, vi + pr * c_i + pi * c_r
        hr_ref[...] = vr
        hi_ref[...] = vi
        edge = 0 if reverse else lc - 1
        cr[...] = vr[edge:edge + 1]
        ci[...] = vi[edge:edge + 1]

    tmap = (lambda c: nc - 1 - c) if reverse else (lambda c: c)
    blk = pl.BlockSpec((None, lc, S5_TN), lambda b, j, c: (b, tmap(c), j))
    return pl.pallas_call(
        body, out_shape=[jax.ShapeDtypeStruct(xr.shape, F32), jax.ShapeDtypeStruct(xr.shape, F32)],
        grid=(bsz, n // S5_TN, nc),
        in_specs=[blk, blk, pl.BlockSpec((4, lc, S5_TN), lambda b, j, c: (0, 0, j))], out_specs=[blk, blk],
        scratch_shapes=[pltpu.VMEM((1, S5_TN), F32), pltpu.VMEM((1, S5_TN), F32)], name=name,
        compiler_params=_cparams(("parallel", "parallel", "arbitrary")))(xr, xi, pw)


def _s5_dabar(gr, gi, hr, hi, *, name):
    bsz, l, n = gr.shape
    lc = min(S5_LC, l)
    nc = l // lc
    hb = lc // SUBLANE

    def body(gr_ref, gi_ref, hr_ref, hi_ref, hrp_ref, hip_ref, o_ref):
        c = pl.program_id(2)

        @pl.when(c == 0)
        def _():
            o_ref[...] = jnp.zeros_like(o_ref)

        def prev(h_ref, hp_ref):
            first = jnp.where(c > 0, hp_ref[SUBLANE - 1:SUBLANE, :], 0.0)
            return jnp.where(_iota((lc, S5_TN), 0) == 0, first, pltpu.roll(h_ref[...], 1, 0))

        pr, pi = prev(hr_ref, hrp_ref), prev(hi_ref, hip_ref)
        g_r, g_i = gr_ref[...], gi_ref[...]
        d_re = jnp.sum(g_r * pr + g_i * pi, axis=0, keepdims=True)
        d_im = jnp.sum(g_i * pr - g_r * pi, axis=0, keepdims=True)
        o_ref[...] += jnp.concatenate([d_re, d_im, jnp.zeros((SUBLANE - 2, S5_TN), F32)], axis=0)

    blk = pl.BlockSpec((None, lc, S5_TN), lambda b, j, c: (b, c, j))
    prv = pl.BlockSpec((None, SUBLANE, S5_TN), lambda b, j, c: (b, jnp.maximum(c * hb - 1, 0), j))
    return pl.pallas_call(
        body, out_shape=jax.ShapeDtypeStruct((bsz, SUBLANE, n), F32), grid=(bsz, n // S5_TN, nc),
        in_specs=[blk, blk, blk, blk, prv, prv],
        out_specs=pl.BlockSpec((None, SUBLANE, S5_TN), lambda b, j, c: (b, 0, j)), name=name,
        compiler_params=_cparams(("parallel", "parallel", "arbitrary")))(gr, gi, hr, hi, hr, hi)


def _s5_post_fwd(ych, u, d, wglu, *, name):
    t, w = ych.shape
    tm = _pick(t, (512, 256, 128))

    def body(y_ref, u_ref, d_ref, w_ref, o_ref):
        o_ref[...] = s5_post(y_ref[...], u_ref[...], d_ref[...], w_ref[...])

    row = pl.BlockSpec((tm, w), lambda i: (i, 0))
    return pl.pallas_call(
        body, out_shape=jax.ShapeDtypeStruct((t, w), F32), grid=(t // tm,),
        in_specs=[row, row, pl.BlockSpec((1, w), lambda i: (0, 0)), pl.BlockSpec((w, w), lambda i: (0, 0))],
        out_specs=row, name=name, compiler_params=_cparams(("parallel",)))(ych, u, d, wglu)


def _s5_post_bwd(ych, u, d, wglu, dout, *, name):
    t, w = ych.shape
    tm = _pick(t, (512, 256, 128))

    def body(y_ref, u_ref, d_ref, w_ref, do_ref, dy_ref, du_ref, dd_ref, dw_ref):
        _, vjp = jax.vjp(s5_post, y_ref[...], u_ref[...], d_ref[...], w_ref[...])
        dy, du, dd, dw = vjp(do_ref[...])
        dy_ref[...] = dy
        du_ref[...] = du

        @pl.when(pl.program_id(0) == 0)
        def _():
            dd_ref[...] = jnp.zeros_like(dd_ref)
            dw_ref[...] = jnp.zeros_like(dw_ref)

        dd_ref[...] += dd
        dw_ref[...] += dw

    row = pl.BlockSpec((tm, w), lambda i: (i, 0))
    vec = pl.BlockSpec((1, w), lambda i: (0, 0))
    mat = pl.BlockSpec((w, w), lambda i: (0, 0))
    return pl.pallas_call(
        body, out_shape=[jax.ShapeDtypeStruct((t, w), F32), jax.ShapeDtypeStruct((t, w), F32),
                         jax.ShapeDtypeStruct((1, w), F32), jax.ShapeDtypeStruct((w, w), F32)], grid=(t // tm,),
        in_specs=[row, row, vec, mat, row], out_specs=[row, row, vec, mat], name=name,
        compiler_params=_cparams(("arbitrary",)))(ych, u, d, wglu, dout)


def _s5_disc(a_re, a_im, b_re, b_im, log_dt):
    dt = jnp.exp(log_dt)[:, None]
    mag = jnp.exp(dt * a_re)
    ang = dt * a_im
    abar_re = mag * jnp.cos(ang)
    abar_im = mag * jnp.sin(ang)
    den = a_re * a_re + a_im * a_im
    zr = abar_re - 1.0
    zi = abar_im
    coef_re = ((zr * a_re + zi * a_im) / den)[..., None]
    coef_im = ((zi * a_re - zr * a_im) / den)[..., None]
    return abar_re, abar_im, coef_re * b_re - coef_im * b_im, coef_re * b_im + coef_im * b_re


def _s5_powers(a_re, a_im, log_dt, lc):
    dt = jnp.exp(log_dt)[:, None]
    n = jnp.arange(1, lc + 1, dtype=F32)[:, None, None]
    mag = jnp.exp(n * (dt * a_re)[None])
    ang = n * (dt * a_im)[None]
    pr = (mag * jnp.cos(ang)).reshape(lc, S5_N)
    pi = (mag * jnp.sin(ang)).reshape(lc, S5_N)
    fwd = jnp.stack([pr, pi, pr, pi])
    bwd = jnp.stack([pr, -pi, pr[::-1], -pi[::-1]])
    return fwd, bwd


def _block_diag_in(bb):
    eye = jnp.eye(S5_GROUPS, dtype=bb.dtype)
    return jnp.einsum("gnp,gh->gphn", bb, eye).reshape(S5_W, S5_N)


def _block_diag_out(cc):
    eye = jnp.eye(S5_GROUPS, dtype=cc.dtype)
    return jnp.einsum("gpn,gh->gnhp", cc, eye).reshape(S5_N, S5_W)


def _adamw(w, g, m, v, *, name):
    shape = w.shape
    cols = shape[-1] if w.ndim > 1 else shape[0]
    rows = w.size // cols
    tr = _pick(rows, (512, 352, 256, 128, 64, 32, 16, 8)) if rows % SUBLANE == 0 else rows
    flat = lambda t: t.reshape(rows, cols)

    def body(w_ref, g_ref, m_ref, v_ref, d_ref, nm_ref, nv_ref):
        gv = g_ref[...]
        mn = ADAM_B1 * m_ref[...] + (1.0 - ADAM_B1) * gv
        vn = ADAM_B2 * v_ref[...] + (1.0 - ADAM_B2) * jnp.square(gv)
        m_hat = mn / (1.0 - ADAM_B1 ** ADAM_STEP)
        v_hat = vn / (1.0 - ADAM_B2 ** ADAM_STEP)
        d_ref[...] = -ADAM_LR * (m_hat / (jnp.sqrt(v_hat) + ADAM_EPS) + ADAM_WD * w_ref[...])
        nm_ref[...] = mn
        nv_ref[...] = vn

    blk = pl.BlockSpec((tr, cols), lambda i: (i, 0))
    sds = jax.ShapeDtypeStruct((rows, cols), F32)
    outs = pl.pallas_call(body, out_shape=[sds, sds, sds], grid=(rows // tr,), in_specs=[blk] * 4,
                          out_specs=[blk] * 3, name=name,
                          compiler_params=_cparams(("parallel",)))(flat(w), flat(g), flat(m), flat(v))
    return tuple(o.reshape(shape) for o in outs)


ANY = pl.BlockSpec(memory_space=pl.ANY)


def _coords():
    x, y, c = lax.axis_index("x"), lax.axis_index("y"), lax.axis_index("c")
    chips = [(1 - x, y), (x, 1 - y), (1 - x, 1 - y)]
    return x, y, c, chips


def _all_gather_chips(packs, *, name):
    n = len(packs)

    def body(*refs):
        p_refs, o_refs, (send, recv) = refs[:n], refs[n:2 * n], refs[2 * n:]
        x, y, c, chips = _coords()
        sibling = (x, y, 1 - c)

        def copy(i, k, src, dst, to):
            return pltpu.make_async_remote_copy(src_ref=src, dst_ref=dst, send_sem=send.at[6 * i + k],
                                                recv_sem=recv.at[6 * i + k], device_id=to, device_id_type=MESH_ID)

        def part(i, chip, half):
            return o_refs[i].at[2 * chip[0] + chip[1], half]

        first = [copy(i, k, p_refs[i].at[c], part(i, (x, y), c), (*chip, c))
                 for i in range(n) for k, chip in enumerate(chips)]
        for cp in first:
            cp.start()
        passed = []
        for i in range(n):
            for k, chip in enumerate(chips):
                copy(i, k, part(i, chip, c), part(i, chip, c), (x, y, c)).wait_recv()
                passed.append(copy(i, 3 + k, part(i, chip, c), part(i, chip, c), sibling))
                passed[-1].start()
        for i in range(n):
            for k, chip in enumerate(chips):
                copy(i, 3 + k, part(i, chip, 1 - c), part(i, chip, 1 - c), (x, y, c)).wait_recv()
        for cp in first + passed:
            cp.wait_send()

    return pl.pallas_call(
        body, out_shape=[jax.ShapeDtypeStruct((N_CHIPS,) + p.shape, p.dtype) for p in packs], in_specs=[ANY] * n,
        out_specs=[ANY] * n,
        scratch_shapes=[pltpu.SemaphoreType.DMA((6 * n,)), pltpu.SemaphoreType.DMA((6 * n,))], name=name)(*packs)


def _swap_halves(gs, *, name):
    n = len(gs)

    def body(*refs):
        g_refs, o_refs, (send, recv) = refs[:n], refs[n:2 * n], refs[2 * n:]
        x, y, c, _ = _coords()
        cps = [pltpu.make_async_remote_copy(src_ref=g_refs[i].at[:, 1 - c], dst_ref=o_refs[i], send_sem=send.at[i],
                                            recv_sem=recv.at[i], device_id=(x, y, 1 - c), device_id_type=MESH_ID)
               for i in range(n)]
        for cp in cps:
            cp.start()
        for cp in cps:
            cp.wait()

    return pl.pallas_call(
        body, out_shape=[jax.ShapeDtypeStruct(g.shape[:1] + g.shape[2:], g.dtype) for g in gs], in_specs=[ANY] * n,
        out_specs=[ANY] * n, scratch_shapes=[pltpu.SemaphoreType.DMA((n,)), pltpu.SemaphoreType.DMA((n,))],
        name=name)(*gs)


def _add_sibling(g, got, *, name):
    n, _, m, r, w = g.shape

    def body(g0_ref, g1_ref, r_ref, a_ref, own_ref):
        c = lax.axis_index("c")
        mine = 2 * lax.axis_index("x") + lax.axis_index("y")
        j = pl.program_id(1)
        s = jnp.where(c == 0, g0_ref[...], g1_ref[...]) + r_ref[...]
        a_ref[...] = s.astype(BF16)

        @pl.when(j == 0)
        def _():
            own_ref[...] = jnp.zeros_like(own_ref)

        own_ref[...] = jnp.where(j == mine, s, own_ref[...])

    return pl.pallas_call(
        body, out_shape=[jax.ShapeDtypeStruct((n, m, r, w), BF16), jax.ShapeDtypeStruct((m, r, w), F32)],
        grid=(m, n),
        in_specs=[pl.BlockSpec((None, None, None, r, w), lambda e, j: (j, 0, e, 0, 0)),
                  pl.BlockSpec((None, None, None, r, w), lambda e, j: (j, 1, e, 0, 0)),
                  pl.BlockSpec((None, None, r, w), lambda e, j: (j, e, 0, 0))],
        out_specs=[pl.BlockSpec((None, None, r, w), lambda e, j: (j, e, 0, 0)),
                   pl.BlockSpec((None, r, w), lambda e, j: (e, 0, 0))],
        name=name, compiler_params=_cparams(("parallel", "arbitrary")))(g, g, got)


def _exchange_chips(a16s, *, name):
    n = len(a16s)

    def body(*refs):
        a_refs, o_refs, (send, recv) = refs[:n], refs[n:2 * n], refs[2 * n:]
        x, y, c, chips = _coords()
        cps = [pltpu.make_async_remote_copy(src_ref=a_refs[i].at[2 * chip[0] + chip[1]], dst_ref=o_refs[i].at[k],
                                            send_sem=send.at[3 * i + k], recv_sem=recv.at[3 * i + k],
                                            device_id=(*chip, c), device_id_type=MESH_ID)
               for i in range(n) for k, chip in enumerate(chips)]
        for cp in cps:
            cp.start()
        for cp in cps:
            cp.wait()

    return pl.pallas_call(
        body, out_shape=[jax.ShapeDtypeStruct((N_CHIPS - 1,) + a.shape[1:], a.dtype) for a in a16s],
        in_specs=[ANY] * n, out_specs=[ANY] * n,
        scratch_shapes=[pltpu.SemaphoreType.DMA((3 * n,)), pltpu.SemaphoreType.DMA((3 * n,))], name=name)(*a16s)


def _add_chips(own, got, *, name):
    m, r, w = own.shape

    def body(o_ref, g_ref, s_ref):
        s = o_ref[...]
        for k in range(N_CHIPS - 1):
            s = s + g_ref[k].astype(F32)
        s_ref[...] = s

    return pl.pallas_call(
        body, out_shape=jax.ShapeDtypeStruct((m, r, w), F32), grid=(m,),
        in_specs=[pl.BlockSpec((None, r, w), lambda i: (i, 0, 0)),
                  pl.BlockSpec((N_CHIPS - 1, None, r, w), lambda i: (0, i, 0, 0))],
        out_specs=pl.BlockSpec((None, r, w), lambda i: (i, 0, 0)), name=name,
        compiler_params=_cparams(("parallel",)))(own, got)


def _swap_reduced(halves, *, name):
    n = len(halves)

    def body(*refs):
        h_refs, o_refs, (send, recv) = refs[:n], refs[n:2 * n], refs[2 * n:]
        x, y, c, _ = _coords()
        cps = [pltpu.make_async_remote_copy(src_ref=h_refs[i], dst_ref=o_refs[i], send_sem=send.at[i],
                                            recv_sem=recv.at[i], device_id=(x, y, 1 - c), device_id_type=MESH_ID)
               for i in range(n)]
        for cp in cps:
            cp.start()
        for cp in cps:
            cp.wait()

    return pl.pallas_call(
        body, out_shape=[jax.ShapeDtypeStruct(h.shape, h.dtype) for h in halves], in_specs=[ANY] * n,
        out_specs=[ANY] * n, scratch_shapes=[pltpu.SemaphoreType.DMA((n,)), pltpu.SemaphoreType.DMA((n,))],
        name=name)(*halves)


def _gather_all(vec, *, name):
    s, w = vec.shape

    def body(v_ref, o_ref, send, recv, local):
        x, y, c, _ = _coords()
        me = 4 * x + 2 * y + c
        mine = pltpu.make_async_copy(v_ref, o_ref.at[me], local)
        mine.start()
        cps = []
        for k in range(1, N_DEV):
            px = 1 - x if k & 4 else x
            py = 1 - y if k & 2 else y
            pc = 1 - c if k & 1 else c
            peer = 4 * px + 2 * py + pc
            cps.append((pltpu.make_async_remote_copy(src_ref=v_ref, dst_ref=o_ref.at[me], send_sem=send.at[k - 1],
                                                     recv_sem=recv.at[k - 1], device_id=(px, py, pc),
                                                     device_id_type=MESH_ID),
                        pltpu.make_async_remote_copy(src_ref=v_ref, dst_ref=o_ref.at[peer], send_sem=send.at[k - 1],
                                                     recv_sem=recv.at[k - 1], device_id=(px, py, pc),
                                                     device_id_type=MESH_ID)))
        for snd, _ in cps:
            snd.start()
        for snd, rcv in cps:
            rcv.wait_recv()
            snd.wait_send()
        mine.wait()

    return pl.pallas_call(
        body, out_shape=jax.ShapeDtypeStruct((N_DEV, s, w), vec.dtype), in_specs=[ANY], out_specs=ANY,
        scratch_shapes=[pltpu.SemaphoreType.DMA((N_DEV - 1,)), pltpu.SemaphoreType.DMA((N_DEV - 1,)),
                        pltpu.SemaphoreType.DMA], name=name)(vec)


def _sum_all(parts, *, name):
    n, s, w = parts.shape
    tr = _pick(s, (96, 72, 48, 32, 24, 16, 8))

    def body(p_ref, o_ref):
        acc = p_ref[0]
        for k in range(1, n):
            acc = acc + p_ref[k]
        o_ref[...] = acc

    return pl.pallas_call(
        body, out_shape=jax.ShapeDtypeStruct((s, w), F32), grid=(s // tr,),
        in_specs=[pl.BlockSpec((n, tr, w), lambda i: (0, i, 0))], out_specs=pl.BlockSpec((tr, w), lambda i: (i, 0)),
        name=name, compiler_params=_cparams(("parallel",)))(parts)


def _rows_pack(pieces, lead):
    rows = [p.reshape(p.shape[:lead] + (-1, PACK_COLS)) for p in pieces]
    spare = 2 * ROW_BLOCKS * ROW_BLOCK - sum(r.shape[lead] for r in rows)
    rows.append(jnp.zeros(rows[0].shape[:lead] + (spare, PACK_COLS), rows[0].dtype))
    return jnp.concatenate(rows, axis=lead).reshape(rows[0].shape[:lead] + (2, ROW_BLOCKS, ROW_BLOCK, PACK_COLS))


def _rows_unpack(packed, shapes, lead):
    flat = packed.reshape(packed.shape[:lead] + (-1, PACK_COLS))
    out, off = [], 0
    for shp in shapes:
        rows = int(np.prod(shp)) // PACK_COLS
        out.append(lax.slice_in_dim(flat, off, off + rows, axis=lead).reshape(packed.shape[:lead] + tuple(shp)))
        off += rows
    return out


def _to_shards(full, axis):
    return jnp.stack(jnp.split(full, N_CHIPS, axis=axis))


def _by_head(w, parts, heads):
    r = w.shape[0]
    return w.reshape(r, parts, heads, HEAD).transpose(0, 2, 1, 3).reshape(r, parts * heads * HEAD)


def _by_part(w, parts, heads):
    r = w.shape[0]
    return w.reshape(r, heads, parts, HEAD).transpose(0, 2, 1, 3).reshape(r, parts * heads * HEAD)


def kernel(x, ffn1_norm, ffn1_w_gate, ffn1_w_up, ffn1_w_down, mix_norm, ffn2_norm, ffn2_w_gate, ffn2_w_up, ffn2_w_down, ev_w_in, hg_lb_logits, hg_norm_w, s5_a_re, s5_a_im, s5_b_re, s5_b_im, s5_c_re, s5_c_im, s5_d, s5_log_dt, s5_w_glu, ev_w_out, od_w_in, gdn_conv_w, gdn_a_log, gdn_dt_bias, gdn_norm_w, od_w_out, final_norm, loss_target, m_ffn1_norm, m_ffn1_w_gate, m_ffn1_w_up, m_ffn1_w_down, m_mix_norm, m_ffn2_norm, m_ffn2_w_gate, m_ffn2_w_up, m_ffn2_w_down, m_ev_w_in, m_hg_lb_logits, m_hg_norm_w, m_s5_a_re, m_s5_a_im, m_s5_b_re, m_s5_b_im, m_s5_c_re, m_s5_c_im, m_s5_d, m_s5_log_dt, m_s5_w_glu, m_ev_w_out, m_od_w_in, m_gdn_conv_w, m_gdn_a_log, m_gdn_dt_bias, m_gdn_norm_w, m_od_w_out, m_final_norm, v_ffn1_norm, v_ffn1_w_gate, v_ffn1_w_up, v_ffn1_w_down, v_mix_norm, v_ffn2_norm, v_ffn2_w_gate, v_ffn2_w_up, v_ffn2_w_down, v_ev_w_in, v_hg_lb_logits, v_hg_norm_w, v_s5_a_re, v_s5_a_im, v_s5_b_re, v_s5_b_im, v_s5_c_re, v_s5_c_im, v_s5_d, v_s5_log_dt, v_s5_w_glu, v_ev_w_out, v_od_w_in, v_gdn_conv_w, v_gdn_a_log, v_gdn_dt_bias, v_gdn_norm_w, v_od_w_out, v_final_norm):
    args = locals()
    wts = {n: args[n] for n in WEIGHTS}
    mom = {n: args["m_" + n] for n in WEIGHTS}
    var = {n: args["v_" + n] for n in WEIGHTS}
    bsz, seq, _ = x.shape
    t = bsz * seq

    def wire(n):
        if n == "gdn_conv_w":
            return lax.bitcast_convert_type(wts[n], BF16)
        return wts[n].astype(BF16)

    chip = 2 * lax.axis_index("x") + lax.axis_index("y")
    is_own = (jnp.arange(N_CHIPS) == chip).reshape(N_CHIPS, 1, 1, 1, 1)
    n_wide = wts[WIDE[0]].shape[0]
    pack_a = jnp.concatenate([wts[n].astype(BF16) for n in WIDE], axis=0)
    pack_a = pack_a.reshape((2, len(WIDE) * n_wide // 2) + pack_a.shape[1:])
    pack_b = _rows_pack([wire(n) for n, _ in ROWS], 0)
    got_a, got_b = _all_gather_chips([pack_a, pack_b], name="gather_weights")
    got_a = jnp.where(is_own, pack_a[None], got_a).reshape((N_CHIPS, len(WIDE) * n_wide) + pack_a.shape[2:])
    got_b = jnp.where(is_own, pack_b[None], got_b)
    full = {}
    for k, n in enumerate(WIDE):
        full[n] = [jnp.concatenate([got_a[j, k * n_wide + i] for j in range(N_CHIPS)], axis=1) for i in range(n_wide)]
    for (n, axis), piece in zip(ROWS, _rows_unpack(got_b, [wire(n).shape for n, _ in ROWS], 1)):
        if n == "gdn_conv_w":
            piece = lax.bitcast_convert_type(piece, F32)
        full[n] = [jnp.concatenate([piece[j, i] for j in range(N_CHIPS)], axis=axis - 1)
                   for i in range(piece.shape[1])]

    ev_cols = lambda w: jnp.concatenate([_by_head(w[:, :4 * HG_W], 4, HG_HEADS), w[:, 4 * HG_W:]], axis=1)
    ev_cols_back = lambda w: jnp.concatenate([_by_part(w[:, :4 * HG_W], 4, HG_HEADS), w[:, 4 * HG_W:]], axis=1)

    lbs_fn = lambda lg: (lambda p: jnp.cumsum(p, axis=0) - p[0])(jax.nn.softmax(lg, axis=0))
    lbs, lbs_vjp = jax.vjp(lbs_fn, hg_lb_logits)
    lc = min(S5_LC, seq)

    xs = x.reshape(t, D_MODEL)
    saved = []
    for layer in range(DEPTH):
        j = layer // 2
        rec = {}
        xs, rec["ffn1"] = _ffn_fwd(xs, ffn1_norm[layer][None], full["ffn1_w_gate"][layer], full["ffn1_w_up"][layer],
                                   full["ffn1_w_down"][layer], f"a{layer}")
        rec["x_mix"] = xs
        h = _rms_fwd(xs, mix_norm[layer][None], name=f"rms_fwd_mix{layer}")
        rec["h"] = h
        if layer % 2 == 0:
            w_in = ev_cols(full["ev_w_in"][j])
            proj = _mm(h, w_in, "nn", name=f"ev_proj{layer}").reshape(bsz, seq, -1)
            rec["proj"] = proj
            y_a, rec["hg_states"] = _hg_fwd(proj, lbs[j][None], hg_norm_w[j][None], name=f"hgrn2_fwd{layer}")
            disc, rec["disc_vjp"] = jax.vjp(_s5_disc, s5_a_re[j], s5_a_im[j], s5_b_re[j], s5_b_im[j], s5_log_dt[j])
            wb_re, wb_im = _block_diag_in(disc[2]).astype(BF16), _block_diag_in(disc[3]).astype(BF16)
            wc_re = _block_diag_out(s5_c_re[j]).astype(BF16)
            wc_im = _block_diag_out(-s5_c_im[j]).astype(BF16)
            pw_f, pw_b = _s5_powers(s5_a_re[j], s5_a_im[j], s5_log_dt[j], lc)
            u = proj[:, :, 4 * HG_W:].reshape(t, S5_W)
            bu_re = _mm(u, wb_re, "nn", name=f"s5_bu_re{layer}").reshape(bsz, seq, S5_N)
            bu_im = _mm(u, wb_im, "nn", name=f"s5_bu_im{layer}").reshape(bsz, seq, S5_N)
            h_re, h_im = _s5_scan(bu_re, bu_im, pw_f, reverse=False, name=f"s5_scan_fwd{layer}")
            h_re2, h_im2 = h_re.reshape(t, S5_N), h_im.reshape(t, S5_N)
            ych = _mm(h_re2, wc_re, "nn", a2=h_im2, b2=wc_im, name=f"s5_out{layer}")
            w_glu = full["s5_w_glu"][j]
            y_b = _s5_post_fwd(ych, u, s5_d[j][None], w_glu, name=f"s5_post_fwd{layer}")
            rec.update(u=u, h_re=h_re, h_im=h_im, ych=ych, wb=(wb_re, wb_im), wc=(wc_re, wc_im), pw_b=pw_b,
                       y_a=y_a.reshape(t, HG_W), y_b=y_b)
            w_out = full["ev_w_out"][j]
            xs = _mm(rec["y_a"], w_out[:HG_W], "nn", a2=y_b, b2=w_out[HG_W:], res=xs, name=f"ev_out{layer}")
        else:
            w_in = full["od_w_in"][j]
            w_main = _by_head(w_in[:, :4 * GDN_W], 4, GDN_HEADS)
            w_bd = jnp.pad(w_in[:, 4 * GDN_W:], ((0, 0), (0, HEAD - 2 * GDN_HEADS)))
            proj = _mm(h, w_main, "nn", name=f"od_proj{layer}").reshape(bsz, seq, -1)
            bd = _mm(h, w_bd, "nn", name=f"od_proj_bd{layer}").reshape(bsz, seq, HEAD)
            conv_w = full["gdn_conv_w"][j].reshape(CONV_W, 3, GDN_HEADS, HEAD).transpose(0, 2, 1, 3)
            conv_w = jnp.pad(conv_w, ((0, SUBLANE - CONV_W), (0, 0), (0, 1), (0, 0))).reshape(SUBLANE, 4 * GDN_W)
            qkvg = _conv_fwd(proj, conv_w, name=f"gdn_conv_fwd{layer}")
            ab = jnp.pad(gdn_a_log[j][None], ((0, 0), (GDN_HEADS, HEAD - 2 * GDN_HEADS)))
            dtb = jnp.pad(gdn_dt_bias[j][None], ((0, 0), (GDN_HEADS, HEAD - 2 * GDN_HEADS)))
            y, states = _gdn_fwd(qkvg, bd, ab, dtb, gdn_norm_w[j][None], name=f"gdn_fwd{layer}")
            rec.update(proj=proj, bd=bd, conv_w=conv_w, qkvg=qkvg, ab=ab, dtb=dtb, states=states,
                       y=y.reshape(t, GDN_W), w_main=w_main, w_bd=w_bd)
            xs = _mm(rec["y"], full["od_w_out"][j], "nn", res=xs, name=f"od_out{layer}")
        xs, rec["ffn2"] = _ffn_fwd(xs, ffn2_norm[layer][None], full["ffn2_w_gate"][layer], full["ffn2_w_up"][layer],
                                   full["ffn2_w_down"][layer], f"b{layer}")
        saved.append(rec)

    loss_part, dx, d_final = _final_loss(xs, final_norm[None], loss_target.reshape(t, D_MODEL), name="loss_head")
    loss = lax.psum(loss_part[0, 0], ("x", "y", "c"))

    grads = {n: [None] * wts[n].shape[0] for n in WEIGHTS if n != "final_norm"}
    grads["final_norm"] = d_final[0]
    d_lbs = [None] * 2
    for layer in reversed(range(DEPTH)):
        j = layer // 2
        rec = saved[layer]
        dx, dn, dwg, dwu, dwd = _ffn_bwd(rec["ffn2"], ffn2_norm[layer][None], full["ffn2_w_gate"][layer],
                                         full["ffn2_w_up"][layer], full["ffn2_w_down"][layer], dx, f"b{layer}")
        grads["ffn2_norm"][layer], grads["ffn2_w_gate"][layer] = dn[0], dwg
        grads["ffn2_w_up"][layer], grads["ffn2_w_down"][layer] = dwu, dwd
        h = rec["h"]
        if layer % 2 == 0:
            w_out = full["ev_w_out"][j]
            dy_a = _mm(dx, w_out[:HG_W], "nt", name=f"ev_dya{layer}")
            dy_b = _mm(dx, w_out[HG_W:], "nt", name=f"ev_dyb{layer}")
            dw_out = jnp.concatenate([_mm(rec["y_a"], dx, "tn", name=f"ev_dwo_a{layer}"),
                                      _mm(rec["y_b"], dx, "tn", name=f"ev_dwo_b{layer}")], axis=0)
            grads["ev_w_out"][j] = dw_out
            wb_re, wb_im = rec["wb"]
            wc_re, wc_im = rec["wc"]
            dych, du, dd, dwglu = _s5_post_bwd(rec["ych"], rec["u"], s5_d[j][None], full["s5_w_glu"][j], dy_b,
                                               name=f"s5_post_bwd{layer}")
            grads["s5_d"][j], grads["s5_w_glu"][j] = dd[0], dwglu
            h_re2, h_im2 = rec["h_re"].reshape(t, S5_N), rec["h_im"].reshape(t, S5_N)
            dwc_re = _mm(h_re2, dych, "tn", name=f"s5_dwc_re{layer}")
            dwc_im = _mm(h_im2, dych, "tn", name=f"s5_dwc_im{layer}")
            dh_re = _mm(dych, wc_re, "nt", name=f"s5_dh_re{layer}").reshape(bsz, seq, S5_N)
            dh_im = _mm(dych, wc_im, "nt", name=f"s5_dh_im{layer}").reshape(bsz, seq, S5_N)
            g_re, g_im = _s5_scan(dh_re, dh_im, rec["pw_b"], reverse=True, name=f"s5_scan_bwd{layer}")
            da = _s5_dabar(g_re, g_im, rec["h_re"], rec["h_im"], name=f"s5_dabar{layer}").sum(axis=0)
            g_re2, g_im2 = g_re.reshape(t, S5_N), g_im.reshape(t, S5_N)
            dwb_re = _mm(rec["u"], g_re2, "tn", name=f"s5_dwb_re{layer}")
            dwb_im = _mm(rec["u"], g_im2, "tn", name=f"s5_dwb_im{layer}")
            du = _mm(g_re2, wb_re, "nt", a2=g_im2, b2=wb_im, res=du, name=f"s5_du{layer}")
            diag_in = lambda m: jnp.einsum("gpgn->gnp", m.reshape(S5_GROUPS, S5_GROUP, S5_GROUPS, S5_STATE))
            diag_out = lambda m: jnp.einsum("gngp->gpn", m.reshape(S5_GROUPS, S5_STATE, S5_GROUPS, S5_GROUP))
            d_disc = (da[0].reshape(S5_GROUPS, S5_STATE), da[1].reshape(S5_GROUPS, S5_STATE), diag_in(dwb_re),
                      diag_in(dwb_im))
            ga_re, ga_im, gb_re, gb_im, g_dt = rec["disc_vjp"](d_disc)
            grads["s5_a_re"][j], grads["s5_a_im"][j], grads["s5_b_re"][j] = ga_re, ga_im, gb_re
            grads["s5_b_im"][j], grads["s5_log_dt"][j] = gb_im, g_dt
            grads["s5_c_re"][j], grads["s5_c_im"][j] = diag_out(dwc_re), -diag_out(dwc_im)
            dproj_hg, dlb, dnw = _hg_bwd(rec["proj"], lbs[j][None], hg_norm_w[j][None], rec["hg_states"],
                                         dy_a.reshape(bsz, seq, HG_W), name=f"hgrn2_bwd{layer}")
            d_lbs[j] = dlb.sum(axis=0).reshape(HG_W)
            grads["hg_norm_w"][j] = dnw.sum(axis=(0, 1))
            dproj = jnp.concatenate([dproj_hg.reshape(t, 4 * HG_W), du], axis=1)
            w_in = ev_cols(full["ev_w_in"][j])
            dw_in = _mm(h, dproj, "tn", name=f"ev_dwin{layer}")
            grads["ev_w_in"][j] = ev_cols_back(dw_in)
            dh = _mm(dproj, w_in, "nt", name=f"ev_dh{layer}")
        else:
            dy = _mm(dx, full["od_w_out"][j], "nt", name=f"od_dy{layer}")
            grads["od_w_out"][j] = _mm(rec["y"], dx, "tn", name=f"od_dwo{layer}")
            dqkvg, dbd_h, dab, ddt, dnw = _gdn_bwd(rec["qkvg"], rec["bd"], rec["ab"], rec["dtb"],
                                                   gdn_norm_w[j][None], rec["states"],
                                                   dy.reshape(bsz, seq, GDN_W), name=f"gdn_bwd{layer}")
            grads["gdn_norm_w"][j] = dnw.sum(axis=(0, 1))
            grads["gdn_a_log"][j] = dab.sum(axis=(0, 1))[GDN_HEADS:2 * GDN_HEADS]
            grads["gdn_dt_bias"][j] = ddt.sum(axis=(0, 1))[GDN_HEADS:2 * GDN_HEADS]
            dbd = dbd_h.sum(axis=1).reshape(t, HEAD)
            dproj, dconv = _conv_bwd(rec["proj"], rec["conv_w"], dqkvg, name=f"gdn_conv_bwd{layer}")
            dconv = dconv.sum(axis=0)[:CONV_W].reshape(CONV_W, GDN_HEADS, 4, HEAD)[:, :, :3]
            grads["gdn_conv_w"][j] = dconv.transpose(0, 2, 1, 3).reshape(CONV_W, 3 * GDN_W)
            dproj = dproj.reshape(t, 4 * GDN_W)
            dw_main = _by_part(_mm(h, dproj, "tn", name=f"od_dwin{layer}"), 4, GDN_HEADS)
            dw_bd = _mm(h, dbd, "tn", name=f"od_dwbd{layer}")[:, :2 * GDN_HEADS]
            grads["od_w_in"][j] = jnp.concatenate([dw_main, dw_bd], axis=1)
            dh = _mm(dproj, rec["w_main"], "nt", name=f"od_dh{layer}")
            dh = _mm(dbd, rec["w_bd"], "nt", res=dh, name=f"od_dh_bd{layer}")
        dx, dn = _rms_bwd(rec["x_mix"], mix_norm[layer][None], dh, dx, name=f"rms_bwd_mix{layer}")
        grads["mix_norm"][layer] = dn[0]
        dx, dn, dwg, dwu, dwd = _ffn_bwd(rec["ffn1"], ffn1_norm[layer][None], full["ffn1_w_gate"][layer],
                                         full["ffn1_w_up"][layer], full["ffn1_w_down"][layer], dx, f"a{layer}")
        grads["ffn1_norm"][layer], grads["ffn1_w_gate"][layer] = dn[0], dwg
        grads["ffn1_w_up"][layer], grads["ffn1_w_down"][layer] = dwu, dwd
    grad_x = dx.reshape(x.shape)
    (grads["hg_lb_logits"],) = lbs_vjp(jnp.stack(d_lbs))
    wide_grads = {n: grads[n] for n in WIDE}
    for n in WEIGHTS:
        if isinstance(grads[n], list) and n not in WIDE:
            grads[n] = jnp.stack(grads[n])

    g_a = jnp.stack([wide_grads[n][i].reshape(D_MODEL, N_CHIPS, -1).transpose(1, 0, 2)
                     for n in WIDE for i in range(n_wide)], axis=1)
    g_a = g_a.reshape((N_CHIPS, 2, g_a.shape[1] // 2) + g_a.shape[2:])
    g_b = _rows_pack([_to_shards(grads[n], axis) for n, axis in ROWS], 1)
    sib_a, sib_b = _swap_halves([g_a, g_b], name="reduce_swap_halves")
    wire_a, own_a = _add_sibling(g_a, sib_a, name="reduce_add_sibling_wide")
    wire_b, own_b = _add_sibling(g_b, sib_b, name="reduce_add_sibling_rows")
    oth_a, oth_b = _exchange_chips([wire_a, wire_b], name="reduce_exchange_chips")
    half_a = _add_chips(own_a, oth_a, name="reduce_add_chips_wide")
    half_b = _add_chips(own_b, oth_b, name="reduce_add_chips_rows")
    peer_a, peer_b = _swap_reduced([half_a, half_b], name="reduce_swap_reduced")
    south = lax.axis_index("c") == 0
    both = lambda mine, peer: jnp.where(south, jnp.stack([mine, peer]), jnp.stack([peer, mine]))
    red_a = both(half_a, peer_a).reshape((len(WIDE) * n_wide,) + half_a.shape[1:])
    for k, n in enumerate(WIDE):
        grads[n] = red_a[k * n_wide:(k + 1) * n_wide]
    for (n, _), piece in zip(ROWS, _rows_unpack(both(half_b, peer_b), [wts[n].shape for n, _ in ROWS], 0)):
        grads[n] = piece

    small = jnp.concatenate([grads[n].reshape(-1) for n in SMALL])
    small = jnp.pad(small, (0, SMALL_ROWS * PACK_COLS - small.shape[0])).reshape(SMALL_ROWS, PACK_COLS)
    small = _sum_all(_gather_all(small, name="gather_small_grads"), name="sum_small_grads").reshape(-1)
    off = 0
    for n in SMALL:
        size = int(np.prod(wts[n].shape))
        grads[n] = small[off:off + size].reshape(wts[n].shape)
        off += size

    small_w = jnp.concatenate([wts[n].reshape(-1) for n in SMALL])
    small_m = jnp.concatenate([mom[n].reshape(-1) for n in SMALL])
    small_v = jnp.concatenate([var[n].reshape(-1) for n in SMALL])
    n_small = small_w.shape[0]
    padv = lambda a, fill: jnp.pad(a, (0, SMALL_ROWS * PACK_COLS - n_small), constant_values=fill).reshape(
        SMALL_ROWS, PACK_COLS)
    sd, sm, sv = _adamw(padv(small_w, 0.0), small.reshape(SMALL_ROWS, PACK_COLS), padv(small_m, 0.0),
                        padv(small_v, 1.0), name="adamw_small")
    delta, new_m, new_v = {}, {}, {}
    off = 0
    for n in SMALL:
        size = int(np.prod(wts[n].shape))
        delta[n] = sd.reshape(-1)[off:off + size].reshape(wts[n].shape)
        new_m[n] = sm.reshape(-1)[off:off + size].reshape(wts[n].shape)
        new_v[n] = sv.reshape(-1)[off:off + size].reshape(wts[n].shape)
        off += size
    for n, _ in SHARDED:
        delta[n], new_m[n], new_v[n] = _adamw(wts[n], grads[n], mom[n], var[n], name=f"adamw_{n}")

    return (loss, grad_x, *[grads[n] for n in WEIGHTS], *[delta[n] for n in WEIGHTS],
            *[new_m[n] for n in WEIGHTS], *[new_v[n] for n in WEIGHTS])
```
